```python
import math
import jax, jax.numpy as jnp
from jax import lax
import numpy as np

D_MODEL = 4096
BATCH = 4
SEQ = 2048
DEPTH = 4
DEC_BATCH = 8
DEC_SEQ = 1
PAST_LEN = 8192
PAGE_SIZE = 128

N_MIXERS = 3
EPS = 1e-6
NEG = -1e30
TINY = 1e-30

S5_WIDTH = 2 * D_MODEL
S5_GROUP = 16
S5_GROUPS = S5_WIDTH // S5_GROUP
S5_STATE = 64
S5_CHUNK = 128

RG_WIDTH = (4 * D_MODEL // 3) // 256 * 256
RG_BLOCKS = 16
RG_BLOCK = RG_WIDTH // RG_BLOCKS
RG_CONV = 4
RG_C = 8.0

N_HEADS = 32
HEAD_DIM = 128
N_KV = 4
Q_PER_KV = N_HEADS // N_KV
CMP_LEN = 32
CMP_STRIDE = 16
CMP_HIDDEN = 2 * HEAD_DIM
SLC_BLOCK = 64
SLC_TOP = 16
FORCE_SCORE = 1e3
WINDOW = 512
WIN_QBLOCK = 128
SLC_QBLOCK = 32
NSA_Q = N_HEADS * HEAD_DIM
NSA_KV = 4 * N_KV * HEAD_DIM
NSA_WIN = 2 * N_KV * HEAD_DIM
NSA_GATE = 3 * N_HEADS
NSA_IN = NSA_Q + NSA_KV + NSA_WIN + NSA_GATE + NSA_Q

kernel_name = 'hybrid_s5_rglru_nsa_decode_step'


def rms_norm(x, g):
    xf = x.astype(jnp.float32)
    y = xf * lax.rsqrt(jnp.mean(xf * xf, axis=-1, keepdims=True) + EPS)
    return (y * g.astype(jnp.float32)).astype(x.dtype)


def masked_softmax(s, mask):
    s = jnp.where(mask, s.astype(jnp.float32), NEG)
    p = jnp.exp(s - jnp.max(s, axis=-1, keepdims=True)) * mask
    return p / jnp.maximum(jnp.sum(p, axis=-1, keepdims=True), TINY)


def _cplx_combine(e1, e2):
    a1r, a1i, b1r, b1i = e1
    a2r, a2i, b2r, b2i = e2
    return (a2r * a1r - a2i * a1i, a2r * a1i + a2i * a1r,
            a2r * b1r - a2i * b1i + b2r, a2r * b1i + a2i * b1r + b2i)


def s5_scan(u, h0, a_re, a_im, log_dt, b, c, d):
    f32 = jnp.float32
    n, t, _ = u.shape
    dt = jnp.exp(log_dt.astype(f32))[:, None]
    ar, ai = a_re.astype(f32), a_im.astype(f32)
    mag = jnp.exp(ar * dt)
    abar_r, abar_i = mag * jnp.cos(ai * dt), mag * jnp.sin(ai * dt)
    den = ar * ar + ai * ai
    coef_r = ((abar_r - 1.0) * ar + abar_i * ai) / den
    coef_i = (abar_i * ar - (abar_r - 1.0) * ai) / den
    b_r, b_i = b[0].astype(f32), b[1].astype(f32)
    bbar_r = coef_r[..., None] * b_r - coef_i[..., None] * b_i
    bbar_i = coef_r[..., None] * b_i + coef_i[..., None] * b_r
    c_r, c_i = c[0].astype(f32), c[1].astype(f32)
    tc = min(S5_CHUNK, t)
    nc = t // tc
    uf = u.astype(f32)
    u_blocks = uf.reshape(n, nc, tc, S5_GROUPS, S5_GROUP).transpose(1, 2, 0, 3, 4)
    a_r = jnp.broadcast_to(abar_r, (tc, n, S5_GROUPS, S5_STATE))
    a_i = jnp.broadcast_to(abar_i, (tc, n, S5_GROUPS, S5_STATE))

    def block(carry, u_blk):
        hr, hi = carry
        bu_r = jnp.einsum('gpc,tngc->tngp', bbar_r, u_blk)
        bu_i = jnp.einsum('gpc,tngc->tngp', bbar_i, u_blk)
        pr, pi, sr, si = lax.associative_scan(_cplx_combine, (a_r, a_i, bu_r, bu_i), axis=0)
        hr_t = sr + pr * hr - pi * hi
        hi_t = si + pr * hi + pi * hr
        y = jnp.einsum('gcp,tngp->tngc', c_r, hr_t) - jnp.einsum('gcp,tngp->tngc', c_i, hi_t)
        return (hr_t[-1], hi_t[-1]), y

    (hr, hi), ys = lax.scan(block, (h0[:, 0].astype(f32), h0[:, 1].astype(f32)), u_blocks)
    y = ys.transpose(2, 0, 1, 3, 4).reshape(n, t, S5_WIDTH) + d.astype(f32) * uf
    return y.astype(u.dtype), jnp.stack([hr, hi], axis=1).astype(h0.dtype)


def s5_layer(x, h0, w_in, a_re, a_im, log_dt, b, c, d, w_glu, b_glu, w_out):
    u, z = jnp.split(x @ w_in, 2, axis=-1)
    y, h = s5_scan(u, h0, a_re, a_im, log_dt, b, c, d)
    g = jax.nn.gelu(y)
    g = g * jax.nn.sigmoid(g @ w_glu + b_glu)
    return (g * jax.nn.silu(z)) @ w_out, (h,)


def _real_combine(e1, e2):
    a1, b1 = e1
    a2, b2 = e2
    return a2 * a1, a2 * b1 + b2


def block_diag(x, w, bias):
    n, t, _ = x.shape
    y = jnp.einsum('ntbi,bij->ntbj', x.reshape(n, t, RG_BLOCKS, RG_BLOCK), w)
    return y.reshape(n, t, RG_WIDTH) + bias


def rglru_layer(x, conv_buf, h0, w_in, conv_w, conv_b, w_rg, b_rg, w_ig, b_ig, lam, w_out):
    f32 = jnp.float32
    u, z = jnp.split(x @ w_in, 2, axis=-1)
    full = jnp.concatenate([conv_buf.astype(u.dtype), u], axis=1)
    uc = lax.conv_general_dilated(full, conv_w[:, None, :].astype(u.dtype), window_strides=(1,),
                                  padding='VALID', dimension_numbers=('NWC', 'WIO', 'NWC'),
                                  feature_group_count=RG_WIDTH) + conv_b
    ucf = uc.astype(f32)
    r = jax.nn.sigmoid(block_diag(ucf, w_rg.astype(f32), b_rg.astype(f32)))
    ig = jax.nn.sigmoid(block_diag(ucf, w_ig.astype(f32), b_ig.astype(f32)))
    log_a = -RG_C * r * jax.nn.softplus(-lam.astype(f32))
    a = jnp.exp(log_a)
    xin = jnp.sqrt(-jnp.expm1(2.0 * log_a)) * (ig * ucf)
    pa, pb = lax.associative_scan(_real_combine, (a, xin), axis=1)
    h = pb + pa * h0.astype(f32)[:, None, :]
    y = (h.astype(x.dtype) * jax.nn.silu(z)) @ w_out
    return y, (full[:, -(RG_CONV - 1):], h[:, -1].astype(h0.dtype))


def gather_pages(pool, page_table):
    pages = pool[page_table]
    return pages.reshape((page_table.shape[0], page_table.shape[1] * pool.shape[1]) + pool.shape[2:])


def compress(kv, pe, w1, b1, w2, b2):
    tk = kv.shape[1]
    n_cmp = (tk - CMP_LEN) // CMP_STRIDE + 1
    idx = (jnp.arange(n_cmp) * CMP_STRIDE)[:, None] + jnp.arange(CMP_LEN)[None, :]
    blk = kv[:, idx].astype(jnp.float32) + pe[:, None, :]
    hid = jax.nn.gelu(jnp.einsum('nclhd,ldf->nchf', blk, w1) + b1)
    return jnp.einsum('nchf,fd->nchd', hid, w2) + b2


def cmp_slc_branches(qg, q_pos, kv_all, cmp_pe, cmp_w1, cmp_b1, cmp_w2, cmp_b2):
    f32 = jnp.float32
    n, tq = qg.shape[:2]
    tk = kv_all.shape[1]
    kc = compress(kv_all[:, :, 0], cmp_pe[0], cmp_w1[0], cmp_b1[0], cmp_w2[0], cmp_b2[0])
    vc = compress(kv_all[:, :, 1], cmp_pe[1], cmp_w1[1], cmp_b1[1], cmp_w2[1], cmp_b2[1])
    n_cmp = kc.shape[1]
    c_start = jnp.arange(n_cmp) * CMP_STRIDE
    s = jnp.einsum('nqhgd,nchd->nqhgc', qg, kc)
    cmask = (c_start[None, :] + CMP_LEN - 1 <= q_pos[:, None])[None, :, None, None, :]
    p_cmp = masked_softmax(s, cmask)
    o_cmp = jnp.einsum('nqhgc,nchd->nqhgd', p_cmp, vc)
    n_sel = -(-tk // SLC_BLOCK)
    s_start = jnp.arange(n_sel) * SLC_BLOCK
    overlap = ((c_start[:, None] < s_start[None, :] + SLC_BLOCK) &
               (c_start[:, None] + CMP_LEN > s_start[None, :])).astype(f32)
    imp = jnp.einsum('nqhgc,cs->nqhs', p_cmp, overlap)
    cur = (q_pos // SLC_BLOCK)[:, None]
    blk = jnp.arange(n_sel)[None, :]
    forced = ((blk == 0) | (blk == cur) | (blk == cur - 1))[None, :, None, :]
    future = (blk > cur)[None, :, None, :]
    imp = jnp.where(forced, FORCE_SCORE, jnp.where(future, -1.0, imp))
    n_top = min(SLC_TOP, n_sel)
    _, sel = lax.top_k(imp, n_top)
    pad = n_sel * SLC_BLOCK - tk

    def to_blocks(a):
        a = jnp.pad(a, ((0, 0), (0, pad), (0, 0), (0, 0)))
        return a.reshape(n, n_sel, SLC_BLOCK, N_KV, HEAD_DIM).transpose(0, 3, 1, 2, 4)

    kb = to_blocks(kv_all[:, :, 2])
    vb = to_blocks(kv_all[:, :, 3])
    qc = min(SLC_QBLOCK, tq)
    nqc = tq // qc
    q_chunks = qg.reshape(n, nqc, qc, N_KV, Q_PER_KV, HEAD_DIM).swapaxes(0, 1)
    sel_chunks = sel.reshape(n, nqc, qc, N_KV, n_top).swapaxes(0, 1)
    pos_chunks = q_pos.reshape(nqc, qc)
    take = jax.vmap(jax.vmap(lambda blocks, ix: blocks[ix]))

    def chunk(args):
        qi, si, pi = args
        idx = si.transpose(0, 2, 1, 3)
        kg = take(kb, idx).astype(f32)
        vg = take(vb, idx).astype(f32)
        sc = jnp.einsum('nqhgd,nhqtld->nqhgtl', qi, kg).reshape(n, qc, N_KV, Q_PER_KV, n_top * SLC_BLOCK)
        k_pos = idx[..., None] * SLC_BLOCK + jnp.arange(SLC_BLOCK)
        valid = (k_pos <= pi[None, None, :, None, None]).transpose(0, 2, 1, 3, 4)
        valid = valid.reshape(n, qc, N_KV, 1, n_top * SLC_BLOCK)
        p = masked_softmax(sc, valid).reshape(n, qc, N_KV, Q_PER_KV, n_top, SLC_BLOCK)
        return jnp.einsum('nqhgtl,nhqtld->nqhgd', p, vg)

    o_slc = lax.map(chunk, (q_chunks, sel_chunks, pos_chunks))
    o_slc = o_slc.swapaxes(0, 1).reshape(n, tq, N_KV, Q_PER_KV, HEAD_DIM)
    return o_cmp, o_slc


def window_attend(qg, q_pos, k, v, k_pos):
    s = jnp.einsum('nqhgd,nkhd->nqhgk', qg, k.astype(jnp.float32))
    diff = q_pos[:, None] - k_pos[None, :]
    mask = ((diff >= 0) & (diff <= WINDOW) & (k_pos[None, :] >= 0))[None, :, None, None, :]
    p = masked_softmax(s, mask)
    return jnp.einsum('nqhgk,nkhd->nqhgd', p, v.astype(jnp.float32))


def window_prompt(qg, win_new):
    n, t = qg.shape[:2]
    wq = min(WIN_QBLOCK, t)
    nb = t // wq
    kvp = jnp.pad(win_new, ((0, 0), (WINDOW, 0), (0, 0), (0, 0), (0, 0)))
    q_blocks = qg.reshape(n, nb, wq, N_KV, Q_PER_KV, HEAD_DIM).swapaxes(0, 1)

    def one(args):
        i, qi = args
        start = i * wq
        band = lax.dynamic_slice_in_dim(kvp, start, WINDOW + wq, axis=1)
        q_pos = start + jnp.arange(wq)
        k_pos = start - WINDOW + jnp.arange(WINDOW + wq)
        return window_attend(qi, q_pos, band[:, :, 0], band[:, :, 1], k_pos)

    o = lax.map(one, (jnp.arange(nb), q_blocks))
    return o.swapaxes(0, 1).reshape(n, t, N_KV, Q_PER_KV, HEAD_DIM)


def nsa_layer(x, kv_past, win_past, w_in, b_gate, cmp_pe, cmp_w1, cmp_b1, cmp_w2, cmp_b2, w_out):
    f32 = jnp.float32
    n, t, _ = x.shape
    proj = x @ w_in
    q, kv_new, win_new, gl, z = jnp.split(
        proj, [NSA_Q, NSA_Q + NSA_KV, NSA_Q + NSA_KV + NSA_WIN, NSA_Q + NSA_KV + NSA_WIN + NSA_GATE], axis=-1)
    kv_new = kv_new.reshape(n, t, 4, N_KV, HEAD_DIM)
    win_new = win_new.reshape(n, t, 2, N_KV, HEAD_DIM)
    past = 0 if kv_past is None else kv_past.shape[1]
    kv_all = kv_new if kv_past is None else jnp.concatenate([kv_past.astype(kv_new.dtype), kv_new], axis=1)
    q_pos = past + jnp.arange(t)
    qg = q.reshape(n, t, N_KV, Q_PER_KV, HEAD_DIM).astype(f32) * (HEAD_DIM ** -0.5)
    o_cmp, o_slc = cmp_slc_branches(qg, q_pos, kv_all, cmp_pe, cmp_w1, cmp_b1, cmp_w2, cmp_b2)
    if win_past is None:
        o_win = window_prompt(qg, win_new)
        win_state = win_new[:, -min(WINDOW, t):]
    else:
        keys = jnp.concatenate([win_past.astype(win_new.dtype), win_new], axis=1)
        lb = win_past.shape[1]
        k_pos = past - lb + jnp.arange(lb + t)
        o_win = window_attend(qg, q_pos, keys[:, :, 0], keys[:, :, 1], k_pos)
        win_state = keys[:, -min(WINDOW, lb + t):]
    g = jax.nn.sigmoid(gl.astype(f32).reshape(n, t, N_HEADS, 3) + b_gate.astype(f32))
    g = g.reshape(n, t, N_KV, Q_PER_KV, 3)
    o = g[..., 0:1] * o_cmp + g[..., 1:2] * o_slc + g[..., 2:3] * o_win
    o = o.reshape(n, t, NSA_Q).astype(x.dtype)
    return (o * jax.nn.silu(z)) @ w_out, (kv_new, win_state)


def setup_inputs(seed: int = 0) -> dict:
    key = jax.random.key(seed)
    ks = iter(jax.random.split(key, 64))
    f32 = jnp.float32

    def nrm(shape, scale):
        return scale * jax.random.normal(next(ks), shape, f32)

    n_pages = PAST_LEN // PAGE_SIZE
    n_used = DEC_BATCH * n_pages
    n_pool = n_used + n_used // 4
    win_len = min(WINDOW, PAST_LEN)
    inp = {}
    inp['x_prompt'] = nrm((BATCH, SEQ, D_MODEL), 1.0)
    inp['x_sample'] = nrm((DEC_BATCH, DEC_SEQ, D_MODEL), 1.0)
    inp['state_l0_ssm'] = nrm((DEC_BATCH, 2, S5_GROUPS, S5_STATE), 0.5)
    inp['state_l1_conv'] = nrm((DEC_BATCH, RG_CONV - 1, RG_WIDTH), 1.0)
    inp['state_l1_rnn'] = nrm((DEC_BATCH, RG_WIDTH), 0.5)
    inp['cache_l2_kv'] = nrm((n_pool, PAGE_SIZE, 4, N_KV, HEAD_DIM), 1.0)
    inp['cache_l2_win'] = nrm((DEC_BATCH, win_len, 2, N_KV, HEAD_DIM), 1.0)
    inp['state_l3_ssm'] = nrm((DEC_BATCH, 2, S5_GROUPS, S5_STATE), 0.5)
    perm = jax.random.permutation(next(ks), n_pool)[:n_used]
    inp['page_table'] = perm.reshape(DEC_BATCH, n_pages).astype(jnp.int32)

    def add_norms(p):
        inp[p + 'norm_pre'] = 1.0 + nrm((D_MODEL,), 0.05)
        inp[p + 'norm_post'] = 1.0 + nrm((D_MODEL,), 0.05)

    def add_s5(p):
        add_norms(p)
        inp[p + 'w_in'] = nrm((D_MODEL, 2 * S5_WIDTH), D_MODEL ** -0.5)
        n_idx = jnp.arange(S5_STATE, dtype=f32)[None, :]
        inp[p + 'a_re'] = -0.5 + nrm((S5_GROUPS, S5_STATE), 0.01)
        inp[p + 'a_im'] = math.pi * n_idx + nrm((S5_GROUPS, S5_STATE), 0.01)
        inp[p + 'log_dt'] = jax.random.uniform(next(ks), (S5_GROUPS,), f32, math.log(1e-3), math.log(1e-1))
        inp[p + 'b'] = nrm((2, S5_GROUPS, S5_STATE, S5_GROUP), (2 * S5_GROUP) ** -0.5)
        inp[p + 'c'] = nrm((2, S5_GROUPS, S5_GROUP, S5_STATE), (2 * S5_STATE) ** -0.5)
        inp[p + 'd'] = nrm((S5_WIDTH,), 1.0)
        inp[p + 'w_glu'] = nrm((S5_WIDTH, S5_WIDTH), S5_WIDTH ** -0.5)
        inp[p + 'b_glu'] = nrm((S5_WIDTH,), 0.01)
        inp[p + 'w_out'] = nrm((S5_WIDTH, D_MODEL), S5_WIDTH ** -0.5)

    add_s5('l0_')
    add_norms('l1_')
    inp['l1_w_in'] = nrm((D_MODEL, 2 * RG_WIDTH), D_MODEL ** -0.5)
    inp['l1_conv_w'] = nrm((RG_CONV, RG_WIDTH), 0.5)
    inp['l1_conv_b'] = nrm((RG_WIDTH,), 0.01)
    inp['l1_w_rg'] = nrm((RG_BLOCKS, RG_BLOCK, RG_BLOCK), RG_BLOCK ** -0.5)
    inp['l1_b_rg'] = nrm((RG_WIDTH,), 0.01)
    inp['l1_w_ig'] = nrm((RG_BLOCKS, RG_BLOCK, RG_BLOCK), RG_BLOCK ** -0.5)
    inp['l1_b_ig'] = nrm((RG_WIDTH,), 0.01)
    a_c = jax.random.uniform(next(ks), (RG_WIDTH,), f32, 0.9, 0.999)
    a0 = a_c ** (1.0 / RG_C)
    inp['l1_lam'] = jnp.log(a0) - jnp.log1p(-a0)
    inp['l1_w_out'] = nrm((RG_WIDTH, D_MODEL), RG_WIDTH ** -0.5)
    add_norms('l2_')
    inp['l2_w_in'] = nrm((D_MODEL, NSA_IN), D_MODEL ** -0.5)
    inp['l2_b_gate'] = nrm((N_HEADS, 3), 0.01)
    inp['l2_cmp_pe'] = nrm((2, CMP_LEN, HEAD_DIM), 0.1)
    inp['l2_cmp_w1'] = nrm((2, CMP_LEN, HEAD_DIM, CMP_HIDDEN), (CMP_LEN * HEAD_DIM) ** -0.5)
    inp['l2_cmp_b1'] = nrm((2, CMP_HIDDEN), 0.01)
    inp['l2_cmp_w2'] = nrm((2, CMP_HIDDEN, HEAD_DIM), CMP_HIDDEN ** -0.5)
    inp['l2_cmp_b2'] = nrm((2, HEAD_DIM), 0.01)
    inp['l2_w_out'] = nrm((NSA_Q, D_MODEL), NSA_Q ** -0.5)
    add_s5('l3_')
    return inp


def reference(x_prompt, x_sample, state_l0_ssm, state_l1_conv, state_l1_rnn, cache_l2_kv, cache_l2_win,
              state_l3_ssm, page_table,
              l0_norm_pre, l0_norm_post, l0_w_in, l0_a_re, l0_a_im, l0_log_dt, l0_b, l0_c, l0_d,
              l0_w_glu, l0_b_glu, l0_w_out,
              l1_norm_pre, l1_norm_post, l1_w_in, l1_conv_w, l1_conv_b, l1_w_rg, l1_b_rg, l1_w_ig, l1_b_ig,
              l1_lam, l1_w_out,
              l2_norm_pre, l2_norm_post, l2_w_in, l2_b_gate, l2_cmp_pe, l2_cmp_w1, l2_cmp_b1, l2_cmp_w2,
              l2_cmp_b2, l2_w_out,
              l3_norm_pre, l3_norm_post, l3_w_in, l3_a_re, l3_a_im, l3_log_dt, l3_b, l3_c, l3_d,
              l3_w_glu, l3_b_glu, l3_w_out):
    bp = x_prompt.shape[0]
    dt = x_prompt.dtype
    layers = (
        (l0_norm_pre, l0_norm_post,
         lambda h, st: s5_layer(h, st[0], l0_w_in, l0_a_re, l0_a_im, l0_log_dt, l0_b, l0_c, l0_d,
                                l0_w_glu, l0_b_glu, l0_w_out)),
        (l1_norm_pre, l1_norm_post,
         lambda h, st: rglru_layer(h, st[0], st[1], l1_w_in, l1_conv_w, l1_conv_b, l1_w_rg, l1_b_rg,
                                   l1_w_ig, l1_b_ig, l1_lam, l1_w_out)),
        (l2_norm_pre, l2_norm_post,
         lambda h, st: nsa_layer(h, st[0], st[1], l2_w_in, l2_b_gate, l2_cmp_pe, l2_cmp_w1, l2_cmp_b1,
                                 l2_cmp_w2, l2_cmp_b2, l2_w_out)),
        (l3_norm_pre, l3_norm_post,
         lambda h, st: s5_layer(h, st[0], l3_w_in, l3_a_re, l3_a_im, l3_log_dt, l3_b, l3_c, l3_d,
                                l3_w_glu, l3_b_glu, l3_w_out)),
    )
    prompt_state = (
        (jnp.zeros((bp, 2, S5_GROUPS, S5_STATE), dt),),
        (jnp.zeros((bp, RG_CONV - 1, RG_WIDTH), dt), jnp.zeros((bp, RG_WIDTH), dt)),
        (None, None),
        (jnp.zeros((bp, 2, S5_GROUPS, S5_STATE), dt),),
    )
    sample_state = (
        (state_l0_ssm,),
        (state_l1_conv, state_l1_rnn),
        (gather_pages(cache_l2_kv, page_table), cache_l2_win),
        (state_l3_ssm,),
    )
    hp, hs = x_prompt, x_sample
    new_p, new_s = [], []
    for i in range(DEPTH):
        g_pre, g_post, mix = layers[i]
        yp, sp = mix(rms_norm(hp, g_pre), prompt_state[i])
        ys, ss = mix(rms_norm(hs, g_pre), sample_state[i])
        hp = hp + rms_norm(yp, g_post)
        hs = hs + rms_norm(ys, g_post)
        new_p.append(sp)
        new_s.append(ss)
    (l0_ssm_p,), (l1_conv_p, l1_rnn_p), (l2_kv_p, l2_win_p), (l3_ssm_p,) = new_p
    (l0_ssm_s,), (l1_conv_s, l1_rnn_s), (l2_kv_s, l2_win_s), (l3_ssm_s,) = new_s
    return (hp, hs, l0_ssm_p, l0_ssm_s, l1_conv_p, l1_rnn_p, l1_conv_s, l1_rnn_s,
            l2_kv_p, l2_win_p, l2_kv_s, l2_win_s, l3_ssm_p, l3_ssm_s)
```

```python
import functools
import math

import jax
import jax.numpy as jnp
from jax import lax
from jax.experimental import pallas as pl
from jax.experimental.pallas import tpu as pltpu

F32 = jnp.float32
BF16 = jnp.bfloat16

EPS = 1e-6
NEG = -1e30
TINY = 1e-30

S5_GROUP = 16
S5_STATE = 64
S5_CHUNK = 128

RG_BLOCKS = 16
RG_CONV = 4
RG_C = 8.0

N_HEADS = 32
HEAD_DIM = 128
N_KV = 4
Q_PER_KV = N_HEADS // N_KV
CMP_LEN = 32
CMP_STRIDE = 16
SLC_BLOCK = 64
SLC_TOP = 16
FORCE_SCORE = 1e3
WINDOW = 512
WIN_QBLOCK = 128
SLC_QBLOCK = 32

V7X_VMEM_LIMIT_BYTES = 48 * 1024 * 1024


def _pick_tile(n, target, align):
    if n <= target:
        return n
    t = (target // align) * align
    while t >= align:
        if n % t == 0:
            return t
        t -= align
    return n


def _rmsnorm_kernel(x_ref, g_ref, o_ref):
    x = x_ref[...]
    ms = jnp.mean(x * x, axis=-1, keepdims=True)
    o_ref[...] = (x * lax.rsqrt(ms + EPS) * g_ref[...]).astype(o_ref.dtype)


def rmsnorm(x2d, g, out_dtype):
    m, d = x2d.shape
    tm = _pick_tile(m, 256, 8)
    return pl.pallas_call(
        _rmsnorm_kernel,
        grid=(m // tm,),
        in_specs=[pl.BlockSpec((tm, d), lambda i: (i, 0)),
                  pl.BlockSpec((1, d), lambda i: (0, 0))],
        out_specs=pl.BlockSpec((tm, d), lambda i: (i, 0)),
        out_shape=jax.ShapeDtypeStruct((m, d), out_dtype),
        compiler_params=pltpu.CompilerParams(dimension_semantics=("parallel",),
                                             vmem_limit_bytes=V7X_VMEM_LIMIT_BYTES),
        name="rmsnorm",
    )(x2d, g.reshape(1, d))


def _post_norm_residual_kernel(h_ref, y_ref, g_ref, o_ref):
    y = y_ref[...]
    ms = jnp.mean(y * y, axis=-1, keepdims=True)
    o_ref[...] = h_ref[...] + y * lax.rsqrt(ms + EPS) * g_ref[...]


def post_norm_residual(h2d, y2d, g):
    m, d = h2d.shape
    tm = _pick_tile(m, 256, 8)
    return pl.pallas_call(
        _post_norm_residual_kernel,
        grid=(m // tm,),
        in_specs=[pl.BlockSpec((tm, d), lambda i: (i, 0)),
                  pl.BlockSpec((tm, d), lambda i: (i, 0)),
                  pl.BlockSpec((1, d), lambda i: (0, 0))],
        out_specs=pl.BlockSpec((tm, d), lambda i: (i, 0)),
        out_shape=jax.ShapeDtypeStruct((m, d), F32),
        compiler_params=pltpu.CompilerParams(dimension_semantics=("parallel",),
                                             vmem_limit_bytes=V7X_VMEM_LIMIT_BYTES),
        name="post_norm_residual",
    )(h2d, y2d, g.reshape(1, d))


def _matmul_kernel(a_ref, b_ref, o_ref, acc_ref):
    k = pl.program_id(2)

    @pl.when(k == 0)
    def _():
        acc_ref[...] = jnp.zeros_like(acc_ref)

    acc_ref[...] += jnp.dot(a_ref[...], b_ref[...], preferred_element_type=F32)

    @pl.when(k == pl.num_programs(2) - 1)
    def _():
        o_ref[...] = acc_ref[...].astype(o_ref.dtype)


def matmul(a, b, out_dtype=F32):
    m, k = a.shape
    _, n = b.shape
    tm = _pick_tile(m, 1024, 8)
    tn = _pick_tile(n, 512, 128)
    tk = _pick_tile(k, 4096, 128)
    return pl.pallas_call(
        _matmul_kernel,
        grid=(m // tm, n // tn, k // tk),
        in_specs=[pl.BlockSpec((tm, tk), lambda i, j, kk: (i, kk)),
                  pl.BlockSpec((tk, tn), lambda i, j, kk: (kk, j))],
        out_specs=pl.BlockSpec((tm, tn), lambda i, j, kk: (i, j)),
        out_shape=jax.ShapeDtypeStruct((m, n), out_dtype),
        scratch_shapes=[pltpu.VMEM((tm, tn), F32)],
        compiler_params=pltpu.CompilerParams(
            dimension_semantics=("parallel", "parallel", "arbitrary"),
            vmem_limit_bytes=V7X_VMEM_LIMIT_BYTES),
        name="matmul",
    )(a, b)


def proj(x, w_bf16):
    n, t, k = x.shape
    return matmul(x.reshape(n * t, k).astype(BF16), w_bf16).reshape(n, t, -1)


def masked_softmax(s, mask):
    s = jnp.where(mask, s.astype(F32), NEG)
    p = jnp.exp(s - jnp.max(s, axis=-1, keepdims=True)) * mask
    return p / jnp.maximum(jnp.sum(p, axis=-1, keepdims=True), TINY)


def _cplx_combine(e1, e2):
    a1r, a1i, b1r, b1i = e1
    a2r, a2i, b2r, b2i = e2
    return (a2r * a1r - a2i * a1i, a2r * a1i + a2i * a1r,
            a2r * b1r - a2i * b1i + b2r, a2r * b1i + a2i * b1r + b2i)


def s5_scan(u, h0, a_re, a_im, log_dt, b, c, d):
    n, t, width = u.shape
    groups = width // S5_GROUP
    dt = jnp.exp(log_dt.astype(F32))[:, None]
    ar, ai = a_re.astype(F32), a_im.astype(F32)
    mag = jnp.exp(ar * dt)
    abar_r, abar_i = mag * jnp.cos(ai * dt), mag * jnp.sin(ai * dt)
    den = ar * ar + ai * ai
    coef_r = ((abar_r - 1.0) * ar + abar_i * ai) / den
    coef_i = (abar_i * ar - (abar_r - 1.0) * ai) / den
    b_r, b_i = b[0].astype(F32), b[1].astype(F32)
    bbar_r = coef_r[..., None] * b_r - coef_i[..., None] * b_i
    bbar_i = coef_r[..., None] * b_i + coef_i[..., None] * b_r
    c_r, c_i = c[0].astype(F32), c[1].astype(F32)
    tc = min(S5_CHUNK, t)
    nc = t // tc
    uf = u.astype(F32)
    u_blocks = uf.reshape(n, nc, tc, groups, S5_GROUP).transpose(1, 2, 0, 3, 4)
    a_r = jnp.broadcast_to(abar_r, (tc, n, groups, S5_STATE))
    a_i = jnp.broadcast_to(abar_i, (tc, n, groups, S5_STATE))

    def block(carry, u_blk):
        hr, hi = carry
        bu_r = jnp.einsum('gpc,tngc->tngp', bbar_r, u_blk)
        bu_i = jnp.einsum('gpc,tngc->tngp', bbar_i, u_blk)
        pr, pi, sr, si = lax.associative_scan(_cplx_combine, (a_r, a_i, bu_r, bu_i), axis=0)
        hr_t = sr + pr * hr - pi * hi
        hi_t = si + pr * hi + pi * hr
        y = jnp.einsum('gcp,tngp->tngc', c_r, hr_t) - jnp.einsum('gcp,tngp->tngc', c_i, hi_t)
        return (hr_t[-1], hi_t[-1]), y

    (hr, hi), ys = lax.scan(block, (h0[:, 0].astype(F32), h0[:, 1].astype(F32)), u_blocks)
    y = ys.transpose(2, 0, 1, 3, 4).reshape(n, t, width) + d.astype(F32) * uf
    return y.astype(u.dtype), jnp.stack([hr, hi], axis=1).astype(h0.dtype)


def s5_layer(x, h0, w_in, a_re, a_im, log_dt, b, c, d, w_glu, b_glu, w_out):
    u, z = jnp.split(proj(x, w_in), 2, axis=-1)
    y, h = s5_scan(u, h0, a_re, a_im, log_dt, b, c, d)
    g = jax.nn.gelu(y)
    g = g * jax.nn.sigmoid(proj(g, w_glu) + b_glu)
    return proj(g * jax.nn.silu(z), w_out), (h,)


def _real_combine(e1, e2):
    a1, b1 = e1
    a2, b2 = e2
    return a2 * a1, a2 * b1 + b2


def block_diag(x, w, bias):
    n, t, width = x.shape
    y = jnp.einsum('ntbi,bij->ntbj', x.reshape(n, t, RG_BLOCKS, width // RG_BLOCKS), w)
    return y.reshape(n, t, width) + bias


def rglru_layer(x, conv_buf, h0, w_in, conv_w, conv_b, w_rg, b_rg, w_ig, b_ig, lam, w_out):
    u, z = jnp.split(proj(x, w_in), 2, axis=-1)
    width = u.shape[-1]
    full = jnp.concatenate([conv_buf.astype(u.dtype), u], axis=1)
    uc = lax.conv_general_dilated(full, conv_w[:, None, :].astype(u.dtype), window_strides=(1,),
                                  padding='VALID', dimension_numbers=('NWC', 'WIO', 'NWC'),
                                  feature_group_count=width) + conv_b
    ucf = uc.astype(F32)
    r = jax.nn.sigmoid(block_diag(ucf, w_rg.astype(F32), b_rg.astype(F32)))
    ig = jax.nn.sigmoid(block_diag(ucf, w_ig.astype(F32), b_ig.astype(F32)))
    log_a = -RG_C * r * jax.nn.softplus(-lam.astype(F32))
    a = jnp.exp(log_a)
    xin = jnp.sqrt(-jnp.expm1(2.0 * log_a)) * (ig * ucf)
    pa, pb = lax.associative_scan(_real_combine, (a, xin), axis=1)
    h = pb + pa * h0.astype(F32)[:, None, :]
    y = proj(h * jax.nn.silu(z), w_out)
    return y, (full[:, -(RG_CONV - 1):], h[:, -1].astype(h0.dtype))


def gather_pages(pool, page_table):
    pages = pool[page_table]
    return pages.reshape((page_table.shape[0], page_table.shape[1] * pool.shape[1]) + pool.shape[2:])


def compress(kv, pe, w1, b1, w2, b2):
    tk = kv.shape[1]
    n_cmp = (tk - CMP_LEN) // CMP_STRIDE + 1
    idx = (jnp.arange(n_cmp) * CMP_STRIDE)[:, None] + jnp.arange(CMP_LEN)[None, :]
    blk = kv[:, idx].astype(F32) + pe[:, None, :]
    hid = jax.nn.gelu(jnp.einsum('nclhd,ldf->nchf', blk, w1) + b1)
    return jnp.einsum('nchf,fd->nchd', hid, w2) + b2


def cmp_slc_branches(qg, q_pos, kv_all, cmp_pe, cmp_w1, cmp_b1, cmp_w2, cmp_b2):
    n, tq = qg.shape[:2]
    tk = kv_all.shape[1]
    kc = compress(kv_all[:, :, 0], cmp_pe[0], cmp_w1[0], cmp_b1[0], cmp_w2[0], cmp_b2[0])
    vc = compress(kv_all[:, :, 1], cmp_pe[1], cmp_w1[1], cmp_b1[1], cmp_w2[1], cmp_b2[1])
    n_cmp = kc.shape[1]
    c_start = jnp.arange(n_cmp) * CMP_STRIDE
    s = jnp.einsum('nqhgd,nchd->nqhgc', qg, kc)
    cmask = (c_start[None, :] + CMP_LEN - 1 <= q_pos[:, None])[None, :, None, None, :]
    p_cmp = masked_softmax(s, cmask)
    o_cmp = jnp.einsum('nqhgc,nchd->nqhgd', p_cmp, vc)
    n_sel = -(-tk // SLC_BLOCK)
    s_start = jnp.arange(n_sel) * SLC_BLOCK
    overlap = ((c_start[:, None] < s_start[None, :] + SLC_BLOCK) &
               (c_start[:, None] + CMP_LEN > s_start[None, :])).astype(F32)
    imp = jnp.einsum('nqhgc,cs->nqhs', p_cmp, overlap)
    cur = (q_pos // SLC_BLOCK)[:, None]
    blk = jnp.arange(n_sel)[None, :]
    forced = ((blk == 0) | (blk == cur) | (blk == cur - 1))[None, :, None, :]
    future = (blk > cur)[None, :, None, :]
    imp = jnp.where(forced, FORCE_SCORE, jnp.where(future, -1.0, imp))
    n_top = min(SLC_TOP, n_sel)
    _, sel = lax.top_k(imp, n_top)
    pad = n_sel * SLC_BLOCK - tk

    def to_blocks(a):
        a = jnp.pad(a, ((0, 0), (0, pad), (0, 0), (0, 0)))
        return a.reshape(n, n_sel, SLC_BLOCK, N_KV, HEAD_DIM).transpose(0, 3, 1, 2, 4)

    kb = to_blocks(kv_all[:, :, 2])
    vb = to_blocks(kv_all[:, :, 3])
    qc = min(SLC_QBLOCK, tq)
    nqc = tq // qc
    q_chunks = qg.reshape(n, nqc, qc, N_KV, Q_PER_KV, HEAD_DIM).swapaxes(0, 1)
    sel_chunks = sel.reshape(n, nqc, qc, N_KV, n_top).swapaxes(0, 1)
    pos_chunks = q_pos.reshape(nqc, qc)
    take = jax.vmap(jax.vmap(lambda blocks, ix: blocks[ix]))

    def chunk(args):
        qi, si, pi = args
        idx = si.transpose(0, 2, 1, 3)
        kg = take(kb, idx).astype(F32)
        vg = take(vb, idx).astype(F32)
        sc = jnp.einsum('nqhgd,nhqtld->nqhgtl', qi, kg).reshape(n, qc, N_KV, Q_PER_KV, n_top * SLC_BLOCK)
        k_pos = idx[..., None] * SLC_BLOCK + jnp.arange(SLC_BLOCK)
        valid = (k_pos <= pi[None, None, :, None, None]).transpose(0, 2, 1, 3, 4)
        valid = valid.reshape(n, qc, N_KV, 1, n_top * SLC_BLOCK)
        p = masked_softmax(sc, valid).reshape(n, qc, N_KV, Q_PER_KV, n_top, SLC_BLOCK)
        return jnp.einsum('nqhgtl,nhqtld->nqhgd', p, vg)

    o_slc = lax.map(chunk, (q_chunks, sel_chunks, pos_chunks))
    o_slc = o_slc.swapaxes(0, 1).reshape(n, tq, N_KV, Q_PER_KV, HEAD_DIM)
    return o_cmp, o_slc


def window_attend(qg, q_pos, k, v, k_pos):
    s = jnp.einsum('nqhgd,nkhd->nqhgk', qg, k.astype(F32))
    diff = q_pos[:, None] - k_pos[None, :]
    mask = ((diff >= 0) & (diff <= WINDOW) & (k_pos[None, :] >= 0))[None, :, None, None, :]
    p = masked_softmax(s, mask)
    return jnp.einsum('nqhgk,nkhd->nqhgd', p, v.astype(F32))


def window_prompt(qg, win_new):
    n, t = qg.shape[:2]
    wq = min(WIN_QBLOCK, t)
    nb = t // wq
    kvp = jnp.pad(win_new, ((0, 0), (WINDOW, 0), (0, 0), (0, 0), (0, 0)))
    q_blocks = qg.reshape(n, nb, wq, N_KV, Q_PER_KV, HEAD_DIM).swapaxes(0, 1)

    def one(args):
        i, qi = args
        start = i * wq
        band = lax.dynamic_slice_in_dim(kvp, start, WINDOW + wq, axis=1)
        q_pos = start + jnp.arange(wq)
        k_pos = start - WINDOW + jnp.arange(WINDOW + wq)
        return window_attend(qi, q_pos, band[:, :, 0], band[:, :, 1], k_pos)

    o = lax.map(one, (jnp.arange(nb), q_blocks))
    return o.swapaxes(0, 1).reshape(n, t, N_KV, Q_PER_KV, HEAD_DIM)


def nsa_layer(x, kv_past, win_past, w_in, b_gate, cmp_pe, cmp_w1, cmp_b1, cmp_w2, cmp_b2, w_out):
    n, t, _ = x.shape
    nsa_q = N_HEADS * HEAD_DIM
    nsa_kv = 4 * N_KV * HEAD_DIM
    nsa_win = 2 * N_KV * HEAD_DIM
    nsa_gate = 3 * N_HEADS
    pj = proj(x, w_in)
    q, kv_new, win_new, z, gl = jnp.split(
        pj, [nsa_q, nsa_q + nsa_kv, nsa_q + nsa_kv + nsa_win, nsa_q + nsa_kv + nsa_win + nsa_q], axis=-1)
    gl = gl[..., :nsa_gate]
    kv_new = kv_new.reshape(n, t, 4, N_KV, HEAD_DIM)
    win_new = win_new.reshape(n, t, 2, N_KV, HEAD_DIM)
    past = 0 if kv_past is None else kv_past.shape[1]
    kv_all = kv_new if kv_past is None else jnp.concatenate([kv_past.astype(kv_new.dtype), kv_new], axis=1)
    q_pos = past + jnp.arange(t)
    qg = q.reshape(n, t, N_KV, Q_PER_KV, HEAD_DIM).astype(F32) * (HEAD_DIM ** -0.5)
    o_cmp, o_slc = cmp_slc_branches(qg, q_pos, kv_all, cmp_pe, cmp_w1, cmp_b1, cmp_w2, cmp_b2)
    if win_past is None:
        o_win = window_prompt(qg, win_new)
        win_state = win_new[:, -min(WINDOW, t):]
    else:
        keys = jnp.concatenate([win_past.astype(win_new.dtype), win_new], axis=1)
        lb = win_past.shape[1]
        k_pos = past - lb + jnp.arange(lb + t)
        o_win = window_attend(qg, q_pos, keys[:, :, 0], keys[:, :, 1], k_pos)
        win_state = keys[:, -min(WINDOW, lb + t):]
    g = jax.nn.sigmoid(gl.astype(F32).reshape(n, t, N_HEADS, 3) + b_gate.astype(F32))
    g = g.reshape(n, t, N_KV, Q_PER_KV, 3)
    o = g[..., 0:1] * o_cmp + g[..., 1:2] * o_slc + g[..., 2:3] * o_win
    o = o.reshape(n, t, nsa_q)
    return proj(o * jax.nn.silu(z), w_out), (kv_new, win_state)


def _nsa_w_in_layout(w_in):
    nsa_q = N_HEADS * HEAD_DIM
    nsa_kv = 4 * N_KV * HEAD_DIM
    nsa_win = 2 * N_KV * HEAD_DIM
    nsa_gate = 3 * N_HEADS
    a = nsa_q + nsa_kv + nsa_win
    gate_pad = -(a + nsa_q + nsa_gate) % 384
    return jnp.concatenate(
        [w_in[:, :a], w_in[:, a + nsa_gate:], w_in[:, a:a + nsa_gate],
         jnp.zeros((w_in.shape[0], gate_pad), w_in.dtype)], axis=1)


def kernel(x_prompt, x_sample, state_l0_ssm, state_l1_conv, state_l1_rnn, cache_l2_kv, cache_l2_win, state_l3_ssm, page_table, l0_norm_pre, l0_norm_post, l0_w_in, l0_a_re, l0_a_im, l0_log_dt, l0_b, l0_c, l0_d, l0_w_glu, l0_b_glu, l0_w_out, l1_norm_pre, l1_norm_post, l1_w_in, l1_conv_w, l1_conv_b, l1_w_rg, l1_b_rg, l1_w_ig, l1_b_ig, l1_lam, l1_w_out, l2_norm_pre, l2_norm_post, l2_w_in, l2_b_gate, l2_cmp_pe, l2_cmp_w1, l2_cmp_b1, l2_cmp_w2, l2_cmp_b2, l2_w_out, l3_norm_pre, l3_norm_post, l3_w_in, l3_a_re, l3_a_im, l3_log_dt, l3_b, l3_c, l3_d, l3_w_glu, l3_b_glu, l3_w_out):
    bp = x_prompt.shape[0]
    dt = x_prompt.dtype
    d_model = x_prompt.shape[-1]
    s5_groups = l0_a_re.shape[0]
    rg_width = l1_lam.shape[0]
    bf = lambda w: w.astype(BF16)
    l0_wi, l0_wg, l0_wo = bf(l0_w_in), bf(l0_w_glu), bf(l0_w_out)
    l1_wi, l1_wo = bf(l1_w_in), bf(l1_w_out)
    l2_wi, l2_wo = bf(_nsa_w_in_layout(l2_w_in)), bf(l2_w_out)
    l3_wi, l3_wg, l3_wo = bf(l3_w_in), bf(l3_w_glu), bf(l3_w_out)
    layers = (
        (l0_norm_pre, l0_norm_post,
         lambda h, st: s5_layer(h, st[0], l0_wi, l0_a_re, l0_a_im, l0_log_dt, l0_b, l0_c, l0_d,
                                l0_wg, l0_b_glu, l0_wo)),
        (l1_norm_pre, l1_norm_post,
         lambda h, st: rglru_layer(h, st[0], st[1], l1_wi, l1_conv_w, l1_conv_b, l1_w_rg, l1_b_rg,
                                   l1_w_ig, l1_b_ig, l1_lam, l1_wo)),
        (l2_norm_pre, l2_norm_post,
         lambda h, st: nsa_layer(h, st[0], st[1], l2_wi, l2_b_gate, l2_cmp_pe, l2_cmp_w1, l2_cmp_b1,
                                 l2_cmp_w2, l2_cmp_b2, l2_wo)),
        (l3_norm_pre, l3_norm_post,
         lambda h, st: s5_layer(h, st[0], l3_wi, l3_a_re, l3_a_im, l3_log_dt, l3_b, l3_c, l3_d,
                                l3_wg, l3_b_glu, l3_wo)),
    )
    prompt_state = (
        (jnp.zeros((bp, 2, s5_groups, S5_STATE), dt),),
        (jnp.zeros((bp, RG_CONV - 1, rg_width), dt), jnp.zeros((bp, rg_width), dt)),
        (None, None),
        (jnp.zeros((bp, 2, s5_groups, S5_STATE), dt),),
    )
    sample_state = (
        (state_l0_ssm,),
        (state_l1_conv, state_l1_rnn),
        (gather_pages(cache_l2_kv, page_table), cache_l2_win),
        (state_l3_ssm,),
    )
    hp, hs = x_prompt, x_sample
    new_p, new_s = [], []
    for i in range(4):
        g_pre, g_post, mix = layers[i]
        outs = []
        for h, st in ((hp, prompt_state[i]), (hs, sample_state[i])):
            n, t, _ = h.shape
            h2 = h.reshape(n * t, d_model)
            xn = rmsnorm(h2, g_pre, BF16).reshape(n, t, d_model)
            y, s_new = mix(xn, st)
            h_new = post_norm_residual(h2, y.reshape(n * t, d_model), g_post).reshape(n, t, d_model)
            outs.append((h_new, s_new))
        (hp, sp), (hs, ss) = outs
        new_p.append(sp)
        new_s.append(ss)
    (l0_ssm_p,), (l1_conv_p, l1_rnn_p), (l2_kv_p, l2_win_p), (l3_ssm_p,) = new_p
    (l0_ssm_s,), (l1_conv_s, l1_rnn_s), (l2_kv_s, l2_win_s), (l3_ssm_s,) = new_s
    return (hp, hs, l0_ssm_p, l0_ssm_s, l1_conv_p, l1_rnn_p, l1_conv_s, l1_rnn_s,
            l2_kv_p, l2_win_p, l2_kv_s, l2_win_s, l3_ssm_p, l3_ssm_s)
```

```python
import functools
import math

import jax
import jax.numpy as jnp
from jax import lax
from jax.experimental import pallas as pl
from jax.experimental.pallas import tpu as pltpu

F32 = jnp.float32
BF16 = jnp.bfloat16

EPS = 1e-6
NEG = -1e30
TINY = 1e-30

LANES = 128
SUBLANES = 8

S5_GROUP = 16
S5_STATE = 64
S5_TILE_GROUPS = SUBLANES * LANES // S5_STATE

RG_BLOCKS = 16
RG_CONV = 4
RG_C = 8.0

N_HEADS = 32
HEAD_DIM = 128
N_KV = 4
Q_PER_KV = N_HEADS // N_KV
CMP_LEN = 32
CMP_STRIDE = 16
SLC_BLOCK = 64
SLC_TOP = 16
FORCE_SCORE = 1e3
WINDOW = 512
WIN_QBLOCK = 128
SLC_QBLOCK = 32

V7X_VMEM_LIMIT_BYTES = 48 * 1024 * 1024


def _pick_tile(n, target, align):
    if n <= target:
        return n
    t = (target // align) * align
    while t >= align:
        if n % t == 0:
            return t
        t -= align
    return n


def _rmsnorm_kernel(x_ref, g_ref, o_ref):
    x = x_ref[...]
    ms = jnp.mean(x * x, axis=-1, keepdims=True)
    o_ref[...] = (x * lax.rsqrt(ms + EPS) * g_ref[...]).astype(o_ref.dtype)


def rmsnorm(x2d, g, out_dtype):
    m, d = x2d.shape
    tm = _pick_tile(m, 256, 8)
    return pl.pallas_call(
        _rmsnorm_kernel,
        grid=(m // tm,),
        in_specs=[pl.BlockSpec((tm, d), lambda i: (i, 0)),
                  pl.BlockSpec((1, d), lambda i: (0, 0))],
        out_specs=pl.BlockSpec((tm, d), lambda i: (i, 0)),
        out_shape=jax.ShapeDtypeStruct((m, d), out_dtype),
        compiler_params=pltpu.CompilerParams(dimension_semantics=("parallel",),
                                             vmem_limit_bytes=V7X_VMEM_LIMIT_BYTES),
        name="rmsnorm",
    )(x2d, g.reshape(1, d))


def _post_norm_residual_kernel(h_ref, y_ref, g_ref, o_ref):
    y = y_ref[...]
    ms = jnp.mean(y * y, axis=-1, keepdims=True)
    o_ref[...] = h_ref[...] + y * lax.rsqrt(ms + EPS) * g_ref[...]


def post_norm_residual(h2d, y2d, g):
    m, d = h2d.shape
    tm = _pick_tile(m, 256, 8)
    return pl.pallas_call(
        _post_norm_residual_kernel,
        grid=(m // tm,),
        in_specs=[pl.BlockSpec((tm, d), lambda i: (i, 0)),
                  pl.BlockSpec((tm, d), lambda i: (i, 0)),
                  pl.BlockSpec((1, d), lambda i: (0, 0))],
        out_specs=pl.BlockSpec((tm, d), lambda i: (i, 0)),
        out_shape=jax.ShapeDtypeStruct((m, d), F32),
        compiler_params=pltpu.CompilerParams(dimension_semantics=("parallel",),
                                             vmem_limit_bytes=V7X_VMEM_LIMIT_BYTES),
        name="post_norm_residual",
    )(h2d, y2d, g.reshape(1, d))


def _matmul_kernel(a_ref, b_ref, o_ref, acc_ref):
    k = pl.program_id(2)

    @pl.when(k == 0)
    def _():
        acc_ref[...] = jnp.zeros_like(acc_ref)

    acc_ref[...] += jnp.dot(a_ref[...], b_ref[...], preferred_element_type=F32)

    @pl.when(k == pl.num_programs(2) - 1)
    def _():
        o_ref[...] = acc_ref[...].astype(o_ref.dtype)


def matmul(a, b, out_dtype=F32):
    m, k = a.shape
    _, n = b.shape
    tm = _pick_tile(m, 1024, 8)
    tn = _pick_tile(n, 512, 128)
    tk = _pick_tile(k, 4096, 128)
    return pl.pallas_call(
        _matmul_kernel,
        grid=(m // tm, n // tn, k // tk),
        in_specs=[pl.BlockSpec((tm, tk), lambda i, j, kk: (i, kk)),
                  pl.BlockSpec((tk, tn), lambda i, j, kk: (kk, j))],
        out_specs=pl.BlockSpec((tm, tn), lambda i, j, kk: (i, j)),
        out_shape=jax.ShapeDtypeStruct((m, n), out_dtype),
        scratch_shapes=[pltpu.VMEM((tm, tn), F32)],
        compiler_params=pltpu.CompilerParams(
            dimension_semantics=("parallel", "parallel", "arbitrary"),
            vmem_limit_bytes=V7X_VMEM_LIMIT_BYTES),
        name="matmul",
    )(a, b)


def proj(x, w_bf16):
    n, t, k = x.shape
    return matmul(x.reshape(n * t, k).astype(BF16), w_bf16).reshape(n, t, -1)


def masked_softmax(s, mask):
    s = jnp.where(mask, s.astype(F32), NEG)
    p = jnp.exp(s - jnp.max(s, axis=-1, keepdims=True)) * mask
    return p / jnp.maximum(jnp.sum(p, axis=-1, keepdims=True), TINY)


def s5_discretize(a_re, a_im, log_dt, b, c):
    dt = jnp.exp(log_dt.astype(F32))[:, None]
    ar, ai = a_re.astype(F32), a_im.astype(F32)
    mag = jnp.exp(ar * dt)
    abar_r, abar_i = mag * jnp.cos(ai * dt), mag * jnp.sin(ai * dt)
    den = ar * ar + ai * ai
    coef_r = ((abar_r - 1.0) * ar + abar_i * ai) / den
    coef_i = (abar_i * ar - (abar_r - 1.0) * ai) / den
    b_r, b_i = b[0].astype(F32), b[1].astype(F32)
    bbar_r = coef_r[..., None] * b_r - coef_i[..., None] * b_i
    bbar_i = coef_r[..., None] * b_i + coef_i[..., None] * b_r
    return abar_r, abar_i, bbar_r, bbar_i, c[0].astype(F32), c[1].astype(F32)


def s5_tile_weights(bbar_r, bbar_i, c_r, c_i):
    g = bbar_r.shape[0]
    tg = S5_TILE_GROUPS
    nt = g // tg
    eye = jnp.eye(tg, dtype=bool)[None, :, None, :, None]

    def expand(x):
        xt = x.transpose(0, 1, 3, 2)[:, :, :, None, :]
        full = jnp.where(eye, xt, 0.0)
        return full.reshape(nt, tg * x.shape[3], tg * x.shape[2])

    br = expand(bbar_r.reshape(nt, tg, S5_STATE, S5_GROUP))
    bi = expand(bbar_i.reshape(nt, tg, S5_STATE, S5_GROUP))
    cr = expand(c_r.reshape(nt, tg, S5_GROUP, S5_STATE))
    ci = expand(c_i.reshape(nt, tg, S5_GROUP, S5_STATE))
    bcat = jnp.concatenate([br, bi], axis=-1).astype(BF16)
    ccat = jnp.concatenate([cr, -ci], axis=1).astype(BF16)
    return bcat, ccat


def _s5_prompt_kernel(u_ref, bcat_ref, ccat_ref, ar_ref, ai_ref, d_ref, h0_ref,
                      g_ref, hlast_ref, *scr, tc, tp, tg):
    bur_scrs, bui_scrs, h_scr = scr[:tg], scr[tg:2 * tg], scr[2 * tg]
    c = pl.program_id(2)
    sw = S5_TILE_GROUPS * S5_STATE
    cw = S5_TILE_GROUPS * S5_GROUP

    @pl.when(c == 0)
    def _():
        h_scr[...] = h0_ref[0]

    u = u_ref[0]
    ub = u.astype(BF16)
    for k in range(tg):
        bu = jnp.dot(ub[:, k * cw:(k + 1) * cw], bcat_ref[k], preferred_element_type=F32)
        for s in range(SUBLANES):
            bur_scrs[k][pl.ds(s * tp, tc), :] = bu[:, s * LANES:(s + 1) * LANES]
            bui_scrs[k][pl.ds(s * tp, tc), :] = bu[:, sw + s * LANES:sw + (s + 1) * LANES]

    ar = [ar_ref[k] for k in range(tg)]
    ai = [ai_ref[k] for k in range(tg)]

    def step(t, carry):
        new = []
        for k in range(tg):
            hr, hi = carry[2 * k], carry[2 * k + 1]
            br = bur_scrs[k][pl.ds(t, SUBLANES, stride=tp), :]
            bi = bui_scrs[k][pl.ds(t, SUBLANES, stride=tp), :]
            hr2 = ar[k] * hr - ai[k] * hi + br
            hi2 = ar[k] * hi + ai[k] * hr + bi
            bur_scrs[k][pl.ds(t, SUBLANES, stride=tp), :] = hr2
            bui_scrs[k][pl.ds(t, SUBLANES, stride=tp), :] = hi2
            new += [hr2, hi2]
        return tuple(new)

    init = tuple(h_scr[p, k] for k in range(tg) for p in (0, 1))
    fin = lax.fori_loop(0, tc, step, init)
    for k in range(tg):
        h_scr[0, k] = fin[2 * k]
        h_scr[1, k] = fin[2 * k + 1]

    for k in range(tg):
        pieces = [bur_scrs[k][pl.ds(s * tp, tc), :].astype(BF16) for s in range(SUBLANES)]
        pieces += [bui_scrs[k][pl.ds(s * tp, tc), :].astype(BF16) for s in range(SUBLANES)]
        hcat = jnp.concatenate(pieces, axis=-1)
        y = jnp.dot(hcat, ccat_ref[k], preferred_element_type=F32)
        yk = y + d_ref[:, k * cw:(k + 1) * cw] * u[:, k * cw:(k + 1) * cw]
        g_ref[0, :, k * cw:(k + 1) * cw] = jax.nn.gelu(yk)

    @pl.when(c == pl.num_programs(2) - 1)
    def _():
        hlast_ref[0] = h_scr[...]


def s5_prompt(uz, h0, abar_r, abar_i, bcat, ccat, d, *, tc=128, tg=4):
    n, t, w2 = uz.shape
    w = w2 // 2
    g = w // S5_GROUP
    nt = g // S5_TILE_GROUPS
    tg = min(tg, nt)
    tc = min(tc, t)
    tp = tc + SUBLANES
    cw = S5_TILE_GROUPS * S5_GROUP
    sw = S5_TILE_GROUPS * S5_STATE
    h0t = h0.reshape(n, 2, nt, SUBLANES, LANES)
    art = abar_r.reshape(nt, SUBLANES, LANES)
    ait = abar_i.reshape(nt, SUBLANES, LANES)
    kern = functools.partial(_s5_prompt_kernel, tc=tc, tp=tp, tg=tg)
    gout, hlast = pl.pallas_call(
        kern,
        grid=(nt // tg, n, t // tc),
        in_specs=[
            pl.BlockSpec((1, tc, tg * cw), lambda j, i, c: (i, c, j)),
            pl.BlockSpec((tg, cw, 2 * sw), lambda j, i, c: (j, 0, 0)),
            pl.BlockSpec((tg, 2 * sw, cw), lambda j, i, c: (j, 0, 0)),
            pl.BlockSpec((tg, SUBLANES, LANES), lambda j, i, c: (j, 0, 0)),
            pl.BlockSpec((tg, SUBLANES, LANES), lambda j, i, c: (j, 0, 0)),
            pl.BlockSpec((1, tg * cw), lambda j, i, c: (0, j)),
            pl.BlockSpec((1, 2, tg, SUBLANES, LANES), lambda j, i, c: (i, 0, j, 0, 0)),
        ],
        out_specs=[
            pl.BlockSpec((1, tc, tg * cw), lambda j, i, c: (i, c, j)),
            pl.BlockSpec((1, 2, tg, SUBLANES, LANES), lambda j, i, c: (i, 0, j, 0, 0)),
        ],
        out_shape=[jax.ShapeDtypeStruct((n, t, w), F32),
                   jax.ShapeDtypeStruct((n, 2, nt, SUBLANES, LANES), F32)],
        scratch_shapes=[pltpu.VMEM((SUBLANES * tp, LANES), F32)] * (2 * tg)
        + [pltpu.VMEM((2, tg, SUBLANES, LANES), F32)],
        compiler_params=pltpu.CompilerParams(
            dimension_semantics=("parallel", "parallel", "arbitrary"),
            vmem_limit_bytes=V7X_VMEM_LIMIT_BYTES),
        name="s5_prompt",
    )(uz, bcat, ccat, art, ait, d.reshape(1, w), h0t)
    return gout, hlast.reshape(n, 2, g, S5_STATE)


def _s5_step_kernel(u_ref, bcat_ref, ccat_ref, ar_ref, ai_ref, d_ref, h0_ref, g_ref, h_ref):
    sw = S5_TILE_GROUPS * S5_STATE
    u = u_ref[...]
    bu = jnp.dot(u.astype(BF16), bcat_ref[0], preferred_element_type=F32)
    hr, hi = h0_ref[:, 0, :], h0_ref[:, 1, :]
    ar, ai = ar_ref[0], ai_ref[0]
    hr2 = ar * hr - ai * hi + bu[:, :sw]
    hi2 = ar * hi + ai * hr + bu[:, sw:]
    h_ref[:, 0, :] = hr2
    h_ref[:, 1, :] = hi2
    hcat = jnp.concatenate([hr2.astype(BF16), hi2.astype(BF16)], axis=-1)
    y = jnp.dot(hcat, ccat_ref[0], preferred_element_type=F32)
    g_ref[...] = jax.nn.gelu(y + d_ref[...] * u)


def s5_step(uz, h0, abar_r, abar_i, bcat, ccat, d):
    n, w2 = uz.shape
    w = w2 // 2
    g = w // S5_GROUP
    nt = g // S5_TILE_GROUPS
    cw = S5_TILE_GROUPS * S5_GROUP
    sw = S5_TILE_GROUPS * S5_STATE
    gout, hnew = pl.pallas_call(
        _s5_step_kernel,
        grid=(nt,),
        in_specs=[
            pl.BlockSpec((n, cw), lambda j: (0, j)),
            pl.BlockSpec((1, cw, 2 * sw), lambda j: (j, 0, 0)),
            pl.BlockSpec((1, 2 * sw, cw), lambda j: (j, 0, 0)),
            pl.BlockSpec((1, 1, sw), lambda j: (j, 0, 0)),
            pl.BlockSpec((1, 1, sw), lambda j: (j, 0, 0)),
            pl.BlockSpec((1, cw), lambda j: (0, j)),
            pl.BlockSpec((n, 2, sw), lambda j: (0, 0, j)),
        ],
        out_specs=[pl.BlockSpec((n, cw), lambda j: (0, j)),
                   pl.BlockSpec((n, 2, sw), lambda j: (0, 0, j))],
        out_shape=[jax.ShapeDtypeStruct((n, w), F32),
                   jax.ShapeDtypeStruct((n, 2, g * S5_STATE), F32)],
        compiler_params=pltpu.CompilerParams(dimension_semantics=("parallel",),
                                             vmem_limit_bytes=V7X_VMEM_LIMIT_BYTES),
        name="s5_step",
    )(uz, bcat, ccat, abar_r.reshape(nt, 1, sw), abar_i.reshape(nt, 1, sw), d.reshape(1, w),
      h0.reshape(n, 2, g * S5_STATE))
    return gout, hnew.reshape(n, 2, g, S5_STATE)


def s5_params(w_in, a_re, a_im, log_dt, b, c, d, w_glu, b_glu, w_out):
    abar_r, abar_i, bbar_r, bbar_i, c_r, c_i = s5_discretize(a_re, a_im, log_dt, b, c)
    bcat, ccat = s5_tile_weights(bbar_r, bbar_i, c_r, c_i)
    return dict(w_in=w_in.astype(BF16), abar_r=abar_r, abar_i=abar_i, bcat=bcat, ccat=ccat,
                d=d.astype(F32), w_glu=w_glu.astype(BF16), b_glu=b_glu, w_out=w_out.astype(BF16))


def s5_layer(x, h0, p):
    n, t, _ = x.shape
    uz = proj(x, p["w_in"])
    w = uz.shape[-1] // 2
    if t == 1:
        g, h = s5_step(uz[:, 0], h0, p["abar_r"], p["abar_i"], p["bcat"], p["ccat"], p["d"])
        g = g[:, None]
    else:
        g, h = s5_prompt(uz, h0, p["abar_r"], p["abar_i"], p["bcat"], p["ccat"], p["d"])
    z = uz[..., w:]
    g = g * jax.nn.sigmoid(proj(g, p["w_glu"]) + p["b_glu"])
    return proj(g * jax.nn.silu(z), p["w_out"]), (h,)


def _real_combine(e1, e2):
    a1, b1 = e1
    a2, b2 = e2
    return a2 * a1, a2 * b1 + b2


def block_diag(x, w, bias):
    n, t, width = x.shape
    y = jnp.einsum('ntbi,bij->ntbj', x.reshape(n, t, RG_BLOCKS, width // RG_BLOCKS), w)
    return y.reshape(n, t, width) + bias


def rglru_layer(x, conv_buf, h0, w_in, conv_w, conv_b, w_rg, b_rg, w_ig, b_ig, lam, w_out):
    u, z = jnp.split(proj(x, w_in), 2, axis=-1)
    width = u.shape[-1]
    full = jnp.concatenate([conv_buf.astype(u.dtype), u], axis=1)
    uc = lax.conv_general_dilated(full, conv_w[:, None, :].astype(u.dtype), window_strides=(1,),
                                  padding='VALID', dimension_numbers=('NWC', 'WIO', 'NWC'),
                                  feature_group_count=width) + conv_b
    ucf = uc.astype(F32)
    r = jax.nn.sigmoid(block_diag(ucf, w_rg.astype(F32), b_rg.astype(F32)))
    ig = jax.nn.sigmoid(block_diag(ucf, w_ig.astype(F32), b_ig.astype(F32)))
    log_a = -RG_C * r * jax.nn.softplus(-lam.astype(F32))
    a = jnp.exp(log_a)
    xin = jnp.sqrt(-jnp.expm1(2.0 * log_a)) * (ig * ucf)
    pa, pb = lax.associative_scan(_real_combine, (a, xin), axis=1)
    h = pb + pa * h0.astype(F32)[:, None, :]
    y = proj(h * jax.nn.silu(z), w_out)
    return y, (full[:, -(RG_CONV - 1):], h[:, -1].astype(h0.dtype))


def gather_pages(pool, page_table):
    pages = pool[page_table]
    return pages.reshape((page_table.shape[0], page_table.shape[1] * pool.shape[1]) + pool.shape[2:])


def compress(kv, pe, w1, b1, w2, b2):
    tk = kv.shape[1]
    n_cmp = (tk - CMP_LEN) // CMP_STRIDE + 1
    idx = (jnp.arange(n_cmp) * CMP_STRIDE)[:, None] + jnp.arange(CMP_LEN)[None, :]
    blk = kv[:, idx].astype(F32) + pe[:, None, :]
    hid = jax.nn.gelu(jnp.einsum('nclhd,ldf->nchf', blk, w1) + b1)
    return jnp.einsum('nchf,fd->nchd', hid, w2) + b2


def cmp_slc_branches(qg, q_pos, kv_all, cmp_pe, cmp_w1, cmp_b1, cmp_w2, cmp_b2):
    n, tq = qg.shape[:2]
    tk = kv_all.shape[1]
    kc = compress(kv_all[:, :, 0], cmp_pe[0], cmp_w1[0], cmp_b1[0], cmp_w2[0], cmp_b2[0])
    vc = compress(kv_all[:, :, 1], cmp_pe[1], cmp_w1[1], cmp_b1[1], cmp_w2[1], cmp_b2[1])
    n_cmp = kc.shape[1]
    c_start = jnp.arange(n_cmp) * CMP_STRIDE
    s = jnp.einsum('nqhgd,nchd->nqhgc', qg, kc)
    cmask = (c_start[None, :] + CMP_LEN - 1 <= q_pos[:, None])[None, :, None, None, :]
    p_cmp = masked_softmax(s, cmask)
    o_cmp = jnp.einsum('nqhgc,nchd->nqhgd', p_cmp, vc)
    n_sel = -(-tk // SLC_BLOCK)
    s_start = jnp.arange(n_sel) * SLC_BLOCK
    overlap = ((c_start[:, None] < s_start[None, :] + SLC_BLOCK) &
               (c_start[:, None] + CMP_LEN > s_start[None, :])).astype(F32)
    imp = jnp.einsum('nqhgc,cs->nqhs', p_cmp, overlap)
    cur = (q_pos // SLC_BLOCK)[:, None]
    blk = jnp.arange(n_sel)[None, :]
    forced = ((blk == 0) | (blk == cur) | (blk == cur - 1))[None, :, None, :]
    future = (blk > cur)[None, :, None, :]
    imp = jnp.where(forced, FORCE_SCORE, jnp.where(future, -1.0, imp))
    n_top = min(SLC_TOP, n_sel)
    _, sel = lax.top_k(imp, n_top)
    pad = n_sel * SLC_BLOCK - tk

    def to_blocks(a):
        a = jnp.pad(a, ((0, 0), (0, pad), (0, 0), (0, 0)))
        return a.reshape(n, n_sel, SLC_BLOCK, N_KV, HEAD_DIM).transpose(0, 3, 1, 2, 4)

    kb = to_blocks(kv_all[:, :, 2])
    vb = to_blocks(kv_all[:, :, 3])
    qc = min(SLC_QBLOCK, tq)
    nqc = tq // qc
    q_chunks = qg.reshape(n, nqc, qc, N_KV, Q_PER_KV, HEAD_DIM).swapaxes(0, 1)
    sel_chunks = sel.reshape(n, nqc, qc, N_KV, n_top).swapaxes(0, 1)
    pos_chunks = q_pos.reshape(nqc, qc)
    take = jax.vmap(jax.vmap(lambda blocks, ix: blocks[ix]))

    def chunk(args):
        qi, si, pi = args
        idx = si.transpose(0, 2, 1, 3)
        kg = take(kb, idx).astype(F32)
        vg = take(vb, idx).astype(F32)
        sc = jnp.einsum('nqhgd,nhqtld->nqhgtl', qi, kg).reshape(n, qc, N_KV, Q_PER_KV, n_top * SLC_BLOCK)
        k_pos = idx[..., None] * SLC_BLOCK + jnp.arange(SLC_BLOCK)
        valid = (k_pos <= pi[None, None, :, None, None]).transpose(0, 2, 1, 3, 4)
        valid = valid.reshape(n, qc, N_KV, 1, n_top * SLC_BLOCK)
        p = masked_softmax(sc, valid).reshape(n, qc, N_KV, Q_PER_KV, n_top, SLC_BLOCK)
        return jnp.einsum('nqhgtl,nhqtld->nqhgd', p, vg)

    o_slc = lax.map(chunk, (q_chunks, sel_chunks, pos_chunks))
    o_slc = o_slc.swapaxes(0, 1).reshape(n, tq, N_KV, Q_PER_KV, HEAD_DIM)
    return o_cmp, o_slc


def window_attend(qg, q_pos, k, v, k_pos):
    s = jnp.einsum('nqhgd,nkhd->nqhgk', qg, k.astype(F32))
    diff = q_pos[:, None] - k_pos[None, :]
    mask = ((diff >= 0) & (diff <= WINDOW) & (k_pos[None, :] >= 0))[None, :, None, None, :]
    p = masked_softmax(s, mask)
    return jnp.einsum('nqhgk,nkhd->nqhgd', p, v.astype(F32))


def window_prompt(qg, win_new):
    n, t = qg.shape[:2]
    wq = min(WIN_QBLOCK, t)
    nb = t // wq
    kvp = jnp.pad(win_new, ((0, 0), (WINDOW, 0), (0, 0), (0, 0), (0, 0)))
    q_blocks = qg.reshape(n, nb, wq, N_KV, Q_PER_KV, HEAD_DIM).swapaxes(0, 1)

    def one(args):
        i, qi = args
        start = i * wq
        band = lax.dynamic_slice_in_dim(kvp, start, WINDOW + wq, axis=1)
        q_pos = start + jnp.arange(wq)
        k_pos = start - WINDOW + jnp.arange(WINDOW + wq)
        return window_attend(qi, q_pos, band[:, :, 0], band[:, :, 1], k_pos)

    o = lax.map(one, (jnp.arange(nb), q_blocks))
    return o.swapaxes(0, 1).reshape(n, t, N_KV, Q_PER_KV, HEAD_DIM)


def nsa_layer(x, kv_past, win_past, w_in, b_gate, cmp_pe, cmp_w1, cmp_b1, cmp_w2, cmp_b2, w_out):
    n, t, _ = x.shape
    nsa_q = N_HEADS * HEAD_DIM
    nsa_kv = 4 * N_KV * HEAD_DIM
    nsa_win = 2 * N_KV * HEAD_DIM
    nsa_gate = 3 * N_HEADS
    pj = proj(x, w_in)
    q, kv_new, win_new, z, gl = jnp.split(
        pj, [nsa_q, nsa_q + nsa_kv, nsa_q + nsa_kv + nsa_win, nsa_q + nsa_kv + nsa_win + nsa_q], axis=-1)
    gl = gl[..., :nsa_gate]
    kv_new = kv_new.reshape(n, t, 4, N_KV, HEAD_DIM)
    win_new = win_new.reshape(n, t, 2, N_KV, HEAD_DIM)
    past = 0 if kv_past is None else kv_past.shape[1]
    kv_all = kv_new if kv_past is None else jnp.concatenate([kv_past.astype(kv_new.dtype), kv_new], axis=1)
    q_pos = past + jnp.arange(t)
    qg = q.reshape(n, t, N_KV, Q_PER_KV, HEAD_DIM).astype(F32) * (HEAD_DIM ** -0.5)
    o_cmp, o_slc = cmp_slc_branches(qg, q_pos, kv_all, cmp_pe, cmp_w1, cmp_b1, cmp_w2, cmp_b2)
    if win_past is None:
        o_win = window_prompt(qg, win_new)
        win_state = win_new[:, -min(WINDOW, t):]
    else:
        keys = jnp.concatenate([win_past.astype(win_new.dtype), win_new], axis=1)
        lb = win_past.shape[1]
        k_pos = past - lb + jnp.arange(lb + t)
        o_win = window_attend(qg, q_pos, keys[:, :, 0], keys[:, :, 1], k_pos)
        win_state = keys[:, -min(WINDOW, lb + t):]
    g = jax.nn.sigmoid(gl.astype(F32).reshape(n, t, N_HEADS, 3) + b_gate.astype(F32))
    g = g.reshape(n, t, N_KV, Q_PER_KV, 3)
    o = g[..., 0:1] * o_cmp + g[..., 1:2] * o_slc + g[..., 2:3] * o_win
    o = o.reshape(n, t, nsa_q)
    return proj(o * jax.nn.silu(z), w_out), (kv_new, win_state)


def _nsa_w_in_layout(w_in):
    nsa_q = N_HEADS * HEAD_DIM
    nsa_kv = 4 * N_KV * HEAD_DIM
    nsa_win = 2 * N_KV * HEAD_DIM
    nsa_gate = 3 * N_HEADS
    a = nsa_q + nsa_kv + nsa_win
    gate_pad = -(a + nsa_q + nsa_gate) % 384
    return jnp.concatenate(
        [w_in[:, :a], w_in[:, a + nsa_gate:], w_in[:, a:a + nsa_gate],
         jnp.zeros((w_in.shape[0], gate_pad), w_in.dtype)], axis=1)


def kernel(x_prompt, x_sample, state_l0_ssm, state_l1_conv, state_l1_rnn, cache_l2_kv, cache_l2_win, state_l3_ssm, page_table, l0_norm_pre, l0_norm_post, l0_w_in, l0_a_re, l0_a_im, l0_log_dt, l0_b, l0_c, l0_d, l0_w_glu, l0_b_glu, l0_w_out, l1_norm_pre, l1_norm_post, l1_w_in, l1_conv_w, l1_conv_b, l1_w_rg, l1_b_rg, l1_w_ig, l1_b_ig, l1_lam, l1_w_out, l2_norm_pre, l2_norm_post, l2_w_in, l2_b_gate, l2_cmp_pe, l2_cmp_w1, l2_cmp_b1, l2_cmp_w2, l2_cmp_b2, l2_w_out, l3_norm_pre, l3_norm_post, l3_w_in, l3_a_re, l3_a_im, l3_log_dt, l3_b, l3_c, l3_d, l3_w_glu, l3_b_glu, l3_w_out):
    bp = x_prompt.shape[0]
    dt = x_prompt.dtype
    d_model = x_prompt.shape[-1]
    s5_groups = l0_a_re.shape[0]
    rg_width = l1_lam.shape[0]
    bf = lambda w: w.astype(BF16)
    l0_p = s5_params(l0_w_in, l0_a_re, l0_a_im, l0_log_dt, l0_b, l0_c, l0_d, l0_w_glu, l0_b_glu, l0_w_out)
    l3_p = s5_params(l3_w_in, l3_a_re, l3_a_im, l3_log_dt, l3_b, l3_c, l3_d, l3_w_glu, l3_b_glu, l3_w_out)
    l1_wi, l1_wo = bf(l1_w_in), bf(l1_w_out)
    l2_wi, l2_wo = bf(_nsa_w_in_layout(l2_w_in)), bf(l2_w_out)
    layers = (
        (l0_norm_pre, l0_norm_post, lambda h, st: s5_layer(h, st[0], l0_p)),
        (l1_norm_pre, l1_norm_post,
         lambda h, st: rglru_layer(h, st[0], st[1], l1_wi, l1_conv_w, l1_conv_b, l1_w_rg, l1_b_rg,
                                   l1_w_ig, l1_b_ig, l1_lam, l1_wo)),
        (l2_norm_pre, l2_norm_post,
         lambda h, st: nsa_layer(h, st[0], st[1], l2_wi, l2_b_gate, l2_cmp_pe, l2_cmp_w1, l2_cmp_b1,
                                 l2_cmp_w2, l2_cmp_b2, l2_wo)),
        (l3_norm_pre, l3_norm_post, lambda h, st: s5_layer(h, st[0], l3_p)),
    )
    prompt_state = (
        (jnp.zeros((bp, 2, s5_groups, S5_STATE), dt),),
        (jnp.zeros((bp, RG_CONV - 1, rg_width), dt), jnp.zeros((bp, rg_width), dt)),
        (None, None),
        (jnp.zeros((bp, 2, s5_groups, S5_STATE), dt),),
    )
    sample_state = (
        (state_l0_ssm,),
        (state_l1_conv, state_l1_rnn),
        (gather_pages(cache_l2_kv, page_table), cache_l2_win),
        (state_l3_ssm,),
    )
    hp, hs = x_prompt, x_sample
    new_p, new_s = [], []
    for i in range(4):
        g_pre, g_post, mix = layers[i]
        outs = []
        for h, st in ((hp, prompt_state[i]), (hs, sample_state[i])):
            n, t, _ = h.shape
            h2 = h.reshape(n * t, d_model)
            xn = rmsnorm(h2, g_pre, BF16).reshape(n, t, d_model)
            y, s_new = mix(xn, st)
            h_new = post_norm_residual(h2, y.reshape(n * t, d_model), g_post).reshape(n, t, d_model)
            outs.append((h_new, s_new))
        (hp, sp), (hs, ss) = outs
        new_p.append(sp)
        new_s.append(ss)
    (l0_ssm_p,), (l1_conv_p, l1_rnn_p), (l2_kv_p, l2_win_p), (l3_ssm_p,) = new_p
    (l0_ssm_s,), (l1_conv_s, l1_rnn_s), (l2_kv_s, l2_win_s), (l3_ssm_s,) = new_s
    return (hp, hs, l0_ssm_p, l0_ssm_s, l1_conv_p, l1_rnn_p, l1_conv_s, l1_rnn_s,
            l2_kv_p, l2_win_p, l2_kv_s, l2_win_s, l3_ssm_p, l3_ssm_s)
```

```python
import functools
import math

import numpy as np
import jax
import jax.numpy as jnp
from jax import lax
from jax.experimental import pallas as pl
from jax.experimental.pallas import tpu as pltpu

F32 = jnp.float32
BF16 = jnp.bfloat16

EPS = 1e-6
NEG = -1e30
TINY = 1e-30

LANES = 128
SUBLANES = 8

S5_GROUP = 16
S5_STATE = 64
S5_TILE_GROUPS = SUBLANES * LANES // S5_STATE

RG_BLOCKS = 16
RG_CONV = 4
RG_C = 8.0

N_HEADS = 32
HEAD_DIM = 128
N_KV = 4
Q_PER_KV = N_HEADS // N_KV
CMP_LEN = 32
CMP_STRIDE = 16
SLC_BLOCK = 64
SLC_TOP = 16
FORCE_SCORE = 1e3
WINDOW = 512
SLC_QBLOCK = 32

NSA_Q_COLS = N_HEADS * HEAD_DIM
COL_KV = NSA_Q_COLS // LANES
COL_WIN = COL_KV + 4 * N_KV
COL_Z = COL_WIN + 2 * N_KV
COL_GATE = COL_Z + N_HEADS
GROUP_COLS = Q_PER_KV * HEAD_DIM

V7X_VMEM_LIMIT_BYTES = 48 * 1024 * 1024


def _pick_tile(n, target, align):
    if n <= target:
        return n
    t = (target // align) * align
    while t >= align:
        if n % t == 0:
            return t
        t -= align
    return n


def _rmsnorm_kernel(x_ref, g_ref, o_ref):
    x = x_ref[...]
    ms = jnp.mean(x * x, axis=-1, keepdims=True)
    o_ref[...] = (x * lax.rsqrt(ms + EPS) * g_ref[...]).astype(o_ref.dtype)


def rmsnorm(x2d, g, out_dtype):
    m, d = x2d.shape
    tm = _pick_tile(m, 256, 8)
    return pl.pallas_call(
        _rmsnorm_kernel,
        grid=(m // tm,),
        in_specs=[pl.BlockSpec((tm, d), lambda i: (i, 0)),
                  pl.BlockSpec((1, d), lambda i: (0, 0))],
        out_specs=pl.BlockSpec((tm, d), lambda i: (i, 0)),
        out_shape=jax.ShapeDtypeStruct((m, d), out_dtype),
        compiler_params=pltpu.CompilerParams(dimension_semantics=("parallel",),
                                             vmem_limit_bytes=V7X_VMEM_LIMIT_BYTES),
        name="rmsnorm",
    )(x2d, g.reshape(1, d))


def _post_norm_residual_kernel(h_ref, y_ref, g_ref, o_ref):
    y = y_ref[...]
    ms = jnp.mean(y * y, axis=-1, keepdims=True)
    o_ref[...] = h_ref[...] + y * lax.rsqrt(ms + EPS) * g_ref[...]


def post_norm_residual(h2d, y2d, g):
    m, d = h2d.shape
    tm = _pick_tile(m, 256, 8)
    return pl.pallas_call(
        _post_norm_residual_kernel,
        grid=(m // tm,),
        in_specs=[pl.BlockSpec((tm, d), lambda i: (i, 0)),
                  pl.BlockSpec((tm, d), lambda i: (i, 0)),
                  pl.BlockSpec((1, d), lambda i: (0, 0))],
        out_specs=pl.BlockSpec((tm, d), lambda i: (i, 0)),
        out_shape=jax.ShapeDtypeStruct((m, d), F32),
        compiler_params=pltpu.CompilerParams(dimension_semantics=("parallel",),
                                             vmem_limit_bytes=V7X_VMEM_LIMIT_BYTES),
        name="post_norm_residual",
    )(h2d, y2d, g.reshape(1, d))


def _matmul_kernel(a_ref, b_ref, o_ref, acc_ref):
    k = pl.program_id(2)

    @pl.when(k == 0)
    def _():
        acc_ref[...] = jnp.zeros_like(acc_ref)

    acc_ref[...] += jnp.dot(a_ref[...], b_ref[...], preferred_element_type=F32)

    @pl.when(k == pl.num_programs(2) - 1)
    def _():
        o_ref[...] = acc_ref[...].astype(o_ref.dtype)


def matmul(a, b, out_dtype=F32):
    m, k = a.shape
    _, n = b.shape
    tm = _pick_tile(m, 1024, 8)
    tn = _pick_tile(n, 512, 128)
    tk = _pick_tile(k, 4096, 128)
    return pl.pallas_call(
        _matmul_kernel,
        grid=(m // tm, n // tn, k // tk),
        in_specs=[pl.BlockSpec((tm, tk), lambda i, j, kk: (i, kk)),
                  pl.BlockSpec((tk, tn), lambda i, j, kk: (kk, j))],
        out_specs=pl.BlockSpec((tm, tn), lambda i, j, kk: (i, j)),
        out_shape=jax.ShapeDtypeStruct((m, n), out_dtype),
        scratch_shapes=[pltpu.VMEM((tm, tn), F32)],
        compiler_params=pltpu.CompilerParams(
            dimension_semantics=("parallel", "parallel", "arbitrary"),
            vmem_limit_bytes=V7X_VMEM_LIMIT_BYTES),
        name="matmul",
    )(a, b)


def proj(x, w_bf16):
    n, t, k = x.shape
    return matmul(x.reshape(n * t, k).astype(BF16), w_bf16).reshape(n, t, -1)


def masked_softmax(s, mask):
    s = jnp.where(mask, s.astype(F32), NEG)
    p = jnp.exp(s - jnp.max(s, axis=-1, keepdims=True)) * mask
    return p / jnp.maximum(jnp.sum(p, axis=-1, keepdims=True), TINY)


def s5_discretize(a_re, a_im, log_dt, b, c):
    dt = jnp.exp(log_dt.astype(F32))[:, None]
    ar, ai = a_re.astype(F32), a_im.astype(F32)
    mag = jnp.exp(ar * dt)
    abar_r, abar_i = mag * jnp.cos(ai * dt), mag * jnp.sin(ai * dt)
    den = ar * ar + ai * ai
    coef_r = ((abar_r - 1.0) * ar + abar_i * ai) / den
    coef_i = (abar_i * ar - (abar_r - 1.0) * ai) / den
    b_r, b_i = b[0].astype(F32), b[1].astype(F32)
    bbar_r = coef_r[..., None] * b_r - coef_i[..., None] * b_i
    bbar_i = coef_r[..., None] * b_i + coef_i[..., None] * b_r
    return abar_r, abar_i, bbar_r, bbar_i, c[0].astype(F32), c[1].astype(F32)


def s5_tile_weights(bbar_r, bbar_i, c_r, c_i):
    g = bbar_r.shape[0]
    tg = S5_TILE_GROUPS
    nt = g // tg
    eye = jnp.eye(tg, dtype=bool)[None, :, None, :, None]

    def expand(x):
        xt = x.transpose(0, 1, 3, 2)[:, :, :, None, :]
        full = jnp.where(eye, xt, 0.0)
        return full.reshape(nt, tg * x.shape[3], tg * x.shape[2])

    br = expand(bbar_r.reshape(nt, tg, S5_STATE, S5_GROUP))
    bi = expand(bbar_i.reshape(nt, tg, S5_STATE, S5_GROUP))
    cr = expand(c_r.reshape(nt, tg, S5_GROUP, S5_STATE))
    ci = expand(c_i.reshape(nt, tg, S5_GROUP, S5_STATE))
    bcat = jnp.concatenate([br, bi], axis=-1).astype(BF16)
    ccat = jnp.concatenate([cr, -ci], axis=1).astype(BF16)
    return bcat, ccat


def _s5_prompt_kernel(u_ref, bcat_ref, ccat_ref, ar_ref, ai_ref, d_ref, h0_ref,
                      g_ref, hlast_ref, *scr, tc, tp, tg):
    bur_scrs, bui_scrs, h_scr = scr[:tg], scr[tg:2 * tg], scr[2 * tg]
    c = pl.program_id(2)
    sw = S5_TILE_GROUPS * S5_STATE
    cw = S5_TILE_GROUPS * S5_GROUP

    @pl.when(c == 0)
    def _():
        h_scr[...] = h0_ref[0]

    u = u_ref[0]
    ub = u.astype(BF16)
    for k in range(tg):
        bu = jnp.dot(ub[:, k * cw:(k + 1) * cw], bcat_ref[k], preferred_element_type=F32)
        for s in range(SUBLANES):
            bur_scrs[k][pl.ds(s * tp, tc), :] = bu[:, s * LANES:(s + 1) * LANES]
            bui_scrs[k][pl.ds(s * tp, tc), :] = bu[:, sw + s * LANES:sw + (s + 1) * LANES]

    ar = [ar_ref[k] for k in range(tg)]
    ai = [ai_ref[k] for k in range(tg)]

    def step(t, carry):
        new = []
        for k in range(tg):
            hr, hi = carry[2 * k], carry[2 * k + 1]
            br = bur_scrs[k][pl.ds(t, SUBLANES, stride=tp), :]
            bi = bui_scrs[k][pl.ds(t, SUBLANES, stride=tp), :]
            hr2 = ar[k] * hr - ai[k] * hi + br
            hi2 = ar[k] * hi + ai[k] * hr + bi
            bur_scrs[k][pl.ds(t, SUBLANES, stride=tp), :] = hr2
            bui_scrs[k][pl.ds(t, SUBLANES, stride=tp), :] = hi2
            new += [hr2, hi2]
        return tuple(new)

    init = tuple(h_scr[p, k] for k in range(tg) for p in (0, 1))
    fin = lax.fori_loop(0, tc, step, init)
    for k in range(tg):
        h_scr[0, k] = fin[2 * k]
        h_scr[1, k] = fin[2 * k + 1]

    for k in range(tg):
        pieces = [bur_scrs[k][pl.ds(s * tp, tc), :].astype(BF16) for s in range(SUBLANES)]
        pieces += [bui_scrs[k][pl.ds(s * tp, tc), :].astype(BF16) for s in range(SUBLANES)]
        hcat = jnp.concatenate(pieces, axis=-1)
        y = jnp.dot(hcat, ccat_ref[k], preferred_element_type=F32)
        yk = y + d_ref[:, k * cw:(k + 1) * cw] * u[:, k * cw:(k + 1) * cw]
        g_ref[0, :, k * cw:(k + 1) * cw] = jax.nn.gelu(yk)

    @pl.when(c == pl.num_programs(2) - 1)
    def _():
        hlast_ref[0] = h_scr[...]


def s5_prompt(uz, h0, abar_r, abar_i, bcat, ccat, d, *, tc=128, tg=4):
    n, t, w2 = uz.shape
    w = w2 // 2
    g = w // S5_GROUP
    nt = g // S5_TILE_GROUPS
    tg = min(tg, nt)
    tc = min(tc, t)
    tp = tc + SUBLANES
    cw = S5_TILE_GROUPS * S5_GROUP
    sw = S5_TILE_GROUPS * S5_STATE
    h0t = h0.reshape(n, 2, nt, SUBLANES, LANES)
    art = abar_r.reshape(nt, SUBLANES, LANES)
    ait = abar_i.reshape(nt, SUBLANES, LANES)
    kern = functools.partial(_s5_prompt_kernel, tc=tc, tp=tp, tg=tg)
    gout, hlast = pl.pallas_call(
        kern,
        grid=(nt // tg, n, t // tc),
        in_specs=[
            pl.BlockSpec((1, tc, tg * cw), lambda j, i, c: (i, c, j)),
            pl.BlockSpec((tg, cw, 2 * sw), lambda j, i, c: (j, 0, 0)),
            pl.BlockSpec((tg, 2 * sw, cw), lambda j, i, c: (j, 0, 0)),
            pl.BlockSpec((tg, SUBLANES, LANES), lambda j, i, c: (j, 0, 0)),
            pl.BlockSpec((tg, SUBLANES, LANES), lambda j, i, c: (j, 0, 0)),
            pl.BlockSpec((1, tg * cw), lambda j, i, c: (0, j)),
            pl.BlockSpec((1, 2, tg, SUBLANES, LANES), lambda j, i, c: (i, 0, j, 0, 0)),
        ],
        out_specs=[
            pl.BlockSpec((1, tc, tg * cw), lambda j, i, c: (i, c, j)),
            pl.BlockSpec((1, 2, tg, SUBLANES, LANES), lambda j, i, c: (i, 0, j, 0, 0)),
        ],
        out_shape=[jax.ShapeDtypeStruct((n, t, w), F32),
                   jax.ShapeDtypeStruct((n, 2, nt, SUBLANES, LANES), F32)],
        scratch_shapes=[pltpu.VMEM((SUBLANES * tp, LANES), F32)] * (2 * tg)
        + [pltpu.VMEM((2, tg, SUBLANES, LANES), F32)],
        compiler_params=pltpu.CompilerParams(
            dimension_semantics=("parallel", "parallel", "arbitrary"),
            vmem_limit_bytes=V7X_VMEM_LIMIT_BYTES),
        name="s5_prompt",
    )(uz, bcat, ccat, art, ait, d.reshape(1, w), h0t)
    return gout, hlast.reshape(n, 2, g, S5_STATE)


def _s5_step_kernel(u_ref, bcat_ref, ccat_ref, ar_ref, ai_ref, d_ref, h0_ref, g_ref, h_ref):
    sw = S5_TILE_GROUPS * S5_STATE
    u = u_ref[...]
    bu = jnp.dot(u.astype(BF16), bcat_ref[0], preferred_element_type=F32)
    hr, hi = h0_ref[:, 0, :], h0_ref[:, 1, :]
    ar, ai = ar_ref[0], ai_ref[0]
    hr2 = ar * hr - ai * hi + bu[:, :sw]
    hi2 = ar * hi + ai * hr + bu[:, sw:]
    h_ref[:, 0, :] = hr2
    h_ref[:, 1, :] = hi2
    hcat = jnp.concatenate([hr2.astype(BF16), hi2.astype(BF16)], axis=-1)
    y = jnp.dot(hcat, ccat_ref[0], preferred_element_type=F32)
    g_ref[...] = jax.nn.gelu(y + d_ref[...] * u)


def s5_step(uz, h0, abar_r, abar_i, bcat, ccat, d):
    n, w2 = uz.shape
    w = w2 // 2
    g = w // S5_GROUP
    nt = g // S5_TILE_GROUPS
    cw = S5_TILE_GROUPS * S5_GROUP
    sw = S5_TILE_GROUPS * S5_STATE
    gout, hnew = pl.pallas_call(
        _s5_step_kernel,
        grid=(nt,),
        in_specs=[
            pl.BlockSpec((n, cw), lambda j: (0, j)),
            pl.BlockSpec((1, cw, 2 * sw), lambda j: (j, 0, 0)),
            pl.BlockSpec((1, 2 * sw, cw), lambda j: (j, 0, 0)),
            pl.BlockSpec((1, 1, sw), lambda j: (j, 0, 0)),
            pl.BlockSpec((1, 1, sw), lambda j: (j, 0, 0)),
            pl.BlockSpec((1, cw), lambda j: (0, j)),
            pl.BlockSpec((n, 2, sw), lambda j: (0, 0, j)),
        ],
        out_specs=[pl.BlockSpec((n, cw), lambda j: (0, j)),
                   pl.BlockSpec((n, 2, sw), lambda j: (0, 0, j))],
        out_shape=[jax.ShapeDtypeStruct((n, w), F32),
                   jax.ShapeDtypeStruct((n, 2, g * S5_STATE), F32)],
        compiler_params=pltpu.CompilerParams(dimension_semantics=("parallel",),
                                             vmem_limit_bytes=V7X_VMEM_LIMIT_BYTES),
        name="s5_step",
    )(uz, bcat, ccat, abar_r.reshape(nt, 1, sw), abar_i.reshape(nt, 1, sw), d.reshape(1, w),
      h0.reshape(n, 2, g * S5_STATE))
    return gout, hnew.reshape(n, 2, g, S5_STATE)


def s5_params(w_in, a_re, a_im, log_dt, b, c, d, w_glu, b_glu, w_out):
    abar_r, abar_i, bbar_r, bbar_i, c_r, c_i = s5_discretize(a_re, a_im, log_dt, b, c)
    bcat, ccat = s5_tile_weights(bbar_r, bbar_i, c_r, c_i)
    return dict(w_in=w_in.astype(BF16), abar_r=abar_r, abar_i=abar_i, bcat=bcat, ccat=ccat,
                d=d.astype(F32), w_glu=w_glu.astype(BF16), b_glu=b_glu, w_out=w_out.astype(BF16))


def s5_layer(x, h0, p):
    n, t, _ = x.shape
    uz = proj(x, p["w_in"])
    w = uz.shape[-1] // 2
    if t == 1:
        g, h = s5_step(uz[:, 0], h0, p["abar_r"], p["abar_i"], p["bcat"], p["ccat"], p["d"])
        g = g[:, None]
    else:
        g, h = s5_prompt(uz, h0, p["abar_r"], p["abar_i"], p["bcat"], p["ccat"], p["d"])
    z = uz[..., w:]
    g = g * jax.nn.sigmoid(proj(g, p["w_glu"]) + p["b_glu"])
    return proj(g * jax.nn.silu(z), p["w_out"]), (h,)


def _real_combine(e1, e2):
    a1, b1 = e1
    a2, b2 = e2
    return a2 * a1, a2 * b1 + b2


def block_diag(x, w, bias):
    n, t, width = x.shape
    y = jnp.einsum('ntbi,bij->ntbj', x.reshape(n, t, RG_BLOCKS, width // RG_BLOCKS), w)
    return y.reshape(n, t, width) + bias


def rglru_layer(x, conv_buf, h0, w_in, conv_w, conv_b, w_rg, b_rg, w_ig, b_ig, lam, w_out):
    u, z = jnp.split(proj(x, w_in), 2, axis=-1)
    width = u.shape[-1]
    full = jnp.concatenate([conv_buf.astype(u.dtype), u], axis=1)
    uc = lax.conv_general_dilated(full, conv_w[:, None, :].astype(u.dtype), window_strides=(1,),
                                  padding='VALID', dimension_numbers=('NWC', 'WIO', 'NWC'),
                                  feature_group_count=width) + conv_b
    ucf = uc.astype(F32)
    r = jax.nn.sigmoid(block_diag(ucf, w_rg.astype(F32), b_rg.astype(F32)))
    ig = jax.nn.sigmoid(block_diag(ucf, w_ig.astype(F32), b_ig.astype(F32)))
    log_a = -RG_C * r * jax.nn.softplus(-lam.astype(F32))
    a = jnp.exp(log_a)
    xin = jnp.sqrt(-jnp.expm1(2.0 * log_a)) * (ig * ucf)
    pa, pb = lax.associative_scan(_real_combine, (a, xin), axis=1)
    h = pb + pa * h0.astype(F32)[:, None, :]
    y = proj(h * jax.nn.silu(z), w_out)
    return y, (full[:, -(RG_CONV - 1):], h[:, -1].astype(h0.dtype))


def gather_pages(pool, page_table):
    pages = pool[page_table]
    return pages.reshape((page_table.shape[0], page_table.shape[1] * pool.shape[1]) + pool.shape[2:])


def compress(kv, pe, w1, b1, w2, b2):
    tk = kv.shape[1]
    n_cmp = (tk - CMP_LEN) // CMP_STRIDE + 1
    idx = (jnp.arange(n_cmp) * CMP_STRIDE)[:, None] + jnp.arange(CMP_LEN)[None, :]
    blk = kv[:, idx].astype(F32) + pe[:, None, :]
    hid = jax.nn.gelu(jnp.einsum('nclhd,ldf->nchf', blk, w1) + b1)
    return jnp.einsum('nchf,fd->nchd', hid, w2) + b2


def cmp_slc_branches(qg, q_pos, kv_all, cmp_pe, cmp_w1, cmp_b1, cmp_w2, cmp_b2):
    n, tq = qg.shape[:2]
    tk = kv_all.shape[1]
    kc = compress(kv_all[:, :, 0], cmp_pe[0], cmp_w1[0], cmp_b1[0], cmp_w2[0], cmp_b2[0])
    vc = compress(kv_all[:, :, 1], cmp_pe[1], cmp_w1[1], cmp_b1[1], cmp_w2[1], cmp_b2[1])
    n_cmp = kc.shape[1]
    c_start = jnp.arange(n_cmp) * CMP_STRIDE
    s = jnp.einsum('nqhgd,nchd->nqhgc', qg, kc)
    cmask = (c_start[None, :] + CMP_LEN - 1 <= q_pos[:, None])[None, :, None, None, :]
    p_cmp = masked_softmax(s, cmask)
    o_cmp = jnp.einsum('nqhgc,nchd->nqhgd', p_cmp, vc)
    n_sel = -(-tk // SLC_BLOCK)
    s_start = jnp.arange(n_sel) * SLC_BLOCK
    overlap = ((c_start[:, None] < s_start[None, :] + SLC_BLOCK) &
               (c_start[:, None] + CMP_LEN > s_start[None, :])).astype(F32)
    imp = jnp.einsum('nqhgc,cs->nqhs', p_cmp, overlap)
    cur = (q_pos // SLC_BLOCK)[:, None]
    blk = jnp.arange(n_sel)[None, :]
    forced = ((blk == 0) | (blk == cur) | (blk == cur - 1))[None, :, None, :]
    future = (blk > cur)[None, :, None, :]
    imp = jnp.where(forced, FORCE_SCORE, jnp.where(future, -1.0, imp))
    n_top = min(SLC_TOP, n_sel)
    _, sel = lax.top_k(imp, n_top)
    pad = n_sel * SLC_BLOCK - tk

    def to_blocks(a):
        a = jnp.pad(a, ((0, 0), (0, pad), (0, 0), (0, 0)))
        return a.reshape(n, n_sel, SLC_BLOCK, N_KV, HEAD_DIM).transpose(0, 3, 1, 2, 4)

    kb = to_blocks(kv_all[:, :, 2])
    vb = to_blocks(kv_all[:, :, 3])
    qc = min(SLC_QBLOCK, tq)
    nqc = tq // qc
    q_chunks = qg.reshape(n, nqc, qc, N_KV, Q_PER_KV, HEAD_DIM).swapaxes(0, 1)
    sel_chunks = sel.reshape(n, nqc, qc, N_KV, n_top).swapaxes(0, 1)
    pos_chunks = q_pos.reshape(nqc, qc)
    take = jax.vmap(jax.vmap(lambda blocks, ix: blocks[ix]))

    def chunk(args):
        qi, si, pi = args
        idx = si.transpose(0, 2, 1, 3)
        kg = take(kb, idx).astype(F32)
        vg = take(vb, idx).astype(F32)
        sc = jnp.einsum('nqhgd,nhqtld->nqhgtl', qi, kg).reshape(n, qc, N_KV, Q_PER_KV, n_top * SLC_BLOCK)
        k_pos = idx[..., None] * SLC_BLOCK + jnp.arange(SLC_BLOCK)
        valid = (k_pos <= pi[None, None, :, None, None]).transpose(0, 2, 1, 3, 4)
        valid = valid.reshape(n, qc, N_KV, 1, n_top * SLC_BLOCK)
        p = masked_softmax(sc, valid).reshape(n, qc, N_KV, Q_PER_KV, n_top, SLC_BLOCK)
        return jnp.einsum('nqhgtl,nhqtld->nqhgd', p, vg)

    o_slc = lax.map(chunk, (q_chunks, sel_chunks, pos_chunks))
    o_slc = o_slc.swapaxes(0, 1).reshape(n, tq, N_KV, Q_PER_KV, HEAD_DIM)
    return o_cmp, o_slc


def window_attend(qg, q_pos, k, v, k_pos):
    s = jnp.einsum('nqhgd,nkhd->nqhgk', qg, k.astype(F32))
    diff = q_pos[:, None] - k_pos[None, :]
    mask = ((diff >= 0) & (diff <= WINDOW) & (k_pos[None, :] >= 0))[None, :, None, None, :]
    p = masked_softmax(s, mask)
    return jnp.einsum('nqhgk,nkhd->nqhgd', p, v.astype(F32))


def _masked_softmax_rows(s, ok):
    s = jnp.where(ok, s, NEG)
    mx = jnp.max(s, axis=-1, keepdims=True)
    p = jnp.where(ok, jnp.exp(s - mx), 0.0)
    den = jnp.maximum(jnp.sum(p, axis=-1, keepdims=True), TINY)
    return p / den


def _compress_kernel(x_ref, pe_ref, w1_ref, b1_ref, w2_ref, b2_ref, o_ref, *, nh):
    half = CMP_LEN // 2
    acc_lo = jnp.zeros((nh, w1_ref.shape[-1]), F32)
    acc_hi = jnp.zeros((nh, w1_ref.shape[-1]), F32)
    for l in range(half):
        x = x_ref[pl.ds(l, nh, stride=CMP_STRIDE), :]
        acc_lo += jnp.dot((x + pe_ref[0, l:l + 1, :]).astype(BF16), w1_ref[0, l],
                          preferred_element_type=F32)
        acc_hi += jnp.dot((x + pe_ref[0, half + l:half + l + 1, :]).astype(BF16), w1_ref[0, half + l],
                          preferred_element_type=F32)
    hid = jax.nn.gelu(acc_lo + pltpu.roll(acc_hi, nh - 1, axis=0) + b1_ref[0])
    out = jnp.dot(hid.astype(BF16), w2_ref[0], preferred_element_type=F32) + b2_ref[0]
    row = lax.broadcasted_iota(jnp.int32, out.shape, 0)
    o_ref[0, 0, 0] = jnp.where(row < nh - 1, out, 0.0)


def nsa_compress_prompt(pj, cmp_pe, cmp_w1, cmp_b1, cmp_w2, cmp_b2):
    n, t, _ = pj.shape
    nh = t // CMP_STRIDE
    hidden = cmp_w1.shape[-1]
    kern = functools.partial(_compress_kernel, nh=nh)
    return pl.pallas_call(
        kern,
        grid=(n, 2, N_KV),
        in_specs=[
            pl.BlockSpec((None, t, LANES), lambda i, ty, h: (i, 0, COL_KV + ty * N_KV + h)),
            pl.BlockSpec((1, CMP_LEN, HEAD_DIM), lambda i, ty, h: (ty, 0, 0)),
            pl.BlockSpec((1, CMP_LEN, HEAD_DIM, hidden), lambda i, ty, h: (ty, 0, 0, 0)),
            pl.BlockSpec((1, 1, hidden), lambda i, ty, h: (ty, 0, 0)),
            pl.BlockSpec((1, hidden, HEAD_DIM), lambda i, ty, h: (ty, 0, 0)),
            pl.BlockSpec((1, 1, HEAD_DIM), lambda i, ty, h: (ty, 0, 0)),
        ],
        out_specs=pl.BlockSpec((1, 1, 1, nh, HEAD_DIM), lambda i, ty, h: (i, ty, h, 0, 0)),
        out_shape=jax.ShapeDtypeStruct((n, 2, N_KV, nh, HEAD_DIM), F32),
        compiler_params=pltpu.CompilerParams(
            dimension_semantics=("parallel", "parallel", "parallel"),
            vmem_limit_bytes=V7X_VMEM_LIMIT_BYTES),
        name="nsa_compress_prompt",
    )(pj, cmp_pe, cmp_w1.astype(BF16), cmp_b1.reshape(2, 1, hidden), cmp_w2.astype(BF16),
      cmp_b2.reshape(2, 1, HEAD_DIM))


def _nsa_prompt_kernel(q_ref, kc_ref, vc_ref, ks_ref, vs_ref, kw_ref, vw_ref, z_ref, gl_ref, bg_ref,
                       ovl_ref, exp_ref, o_ref, m_scr, l_scr, acc_scr, *, bq, tk, t, n_sel, n_top, wlen):
    qb = pl.program_id(2)
    q = q_ref[0] * (HEAD_DIM ** -0.5)
    qs = jnp.concatenate([q[:, g * HEAD_DIM:(g + 1) * HEAD_DIM] for g in range(Q_PER_KV)],
                         axis=0).astype(BF16)
    nt_dims = (((1,), (1,)), ((), ()))

    def per_head(x):
        return jnp.concatenate([x] * Q_PER_KV, axis=0)

    qpos = qb * bq + lax.broadcasted_iota(jnp.int32, (bq, LANES), 0)
    lane = lax.broadcasted_iota(jnp.int32, (bq, LANES), 1)

    ncp = kc_ref.shape[3]
    s = lax.dot_general(qs, kc_ref[0, 0, 0].astype(BF16), nt_dims, preferred_element_type=F32)
    cpos = lax.broadcasted_iota(jnp.int32, (bq, ncp), 1) * CMP_STRIDE + (CMP_LEN - 1)
    qpos_c = qb * bq + lax.broadcasted_iota(jnp.int32, (bq, ncp), 0)
    p_cmp = _masked_softmax_rows(s, per_head(cpos <= qpos_c))
    o_cmp = jnp.dot(p_cmp.astype(BF16), vc_ref[0, 0, 0].astype(BF16), preferred_element_type=F32)

    psum = p_cmp[0:bq]
    for g in range(1, Q_PER_KV):
        psum = psum + p_cmp[g * bq:(g + 1) * bq]
    p_hi = psum.astype(BF16)
    p_lo = (psum - p_hi.astype(F32)).astype(BF16)
    imp = (jnp.dot(p_hi, ovl_ref[...], preferred_element_type=F32)
           + jnp.dot(p_lo, ovl_ref[...], preferred_element_type=F32))
    cur = qpos // SLC_BLOCK
    forced = (lane == 0) | (lane == cur) | (lane == cur - 1)
    imp = jnp.where(forced, FORCE_SCORE, jnp.where(lane > cur, -1.0, imp))
    imp = jnp.where(lane < n_sel, imp, -2.0)
    rank = jnp.zeros((bq, LANES), F32)
    for b in range(n_sel):
        col = jnp.broadcast_to(imp[:, b:b + 1], (bq, LANES))
        tie = jnp.where(lane > b, 1.0, 0.0)
        rank = rank + jnp.where(col > imp, 1.0, jnp.where(col == imp, tie, 0.0))
    sel = jnp.where(rank < n_top, 1.0, 0.0).astype(BF16)

    m_scr[...] = jnp.full(m_scr.shape, NEG, F32)
    l_scr[...] = jnp.zeros(l_scr.shape, F32)
    acc_scr[...] = jnp.zeros(acc_scr.shape, F32)
    reps = tk // LANES
    qpos_k = qb * bq + lax.broadcasted_iota(jnp.int32, (bq, tk), 0)
    koff = lax.broadcasted_iota(jnp.int32, (bq, tk), 1)

    def body(kt, carry):
        start = pl.multiple_of(kt * tk, tk)
        k = ks_ref[0, pl.ds(start, tk), :].astype(BF16)
        v = vs_ref[0, pl.ds(start, tk), :].astype(BF16)
        s = lax.dot_general(qs, k, nt_dims, preferred_element_type=F32)
        selk = jnp.dot(sel, exp_ref[kt], preferred_element_type=F32)
        ok = per_head((selk > 0.5) & (kt * tk + koff <= qpos_k))
        s = jnp.where(ok, s, NEG)
        m_old = m_scr[...]
        m_new = jnp.maximum(m_old, jnp.max(s, axis=-1, keepdims=True))
        alpha = jnp.exp(m_old - m_new)
        p = jnp.where(ok, jnp.exp(s - jnp.concatenate([m_new] * reps, axis=1)), 0.0)
        l_scr[...] = alpha * l_scr[...] + jnp.sum(p, axis=-1, keepdims=True)
        acc_scr[...] = alpha * acc_scr[...] + jnp.dot(p.astype(BF16), v, preferred_element_type=F32)
        m_scr[...] = m_new
        return carry

    n_tiles = ((qb + 1) * bq + tk - 1) // tk
    lax.fori_loop(0, n_tiles, body, 0)
    o_slc = acc_scr[...] / jnp.maximum(l_scr[...], TINY)

    wstart = pl.multiple_of(jnp.clip(qb * bq + bq - wlen, 0, t - wlen), bq)
    kw = kw_ref[0, pl.ds(wstart, wlen), :].astype(BF16)
    vw = vw_ref[0, pl.ds(wstart, wlen), :].astype(BF16)
    s = lax.dot_general(qs, kw, nt_dims, preferred_element_type=F32)
    diff = (qb * bq + lax.broadcasted_iota(jnp.int32, (bq, wlen), 0)
            - wstart - lax.broadcasted_iota(jnp.int32, (bq, wlen), 1))
    p_win = _masked_softmax_rows(s, per_head((diff >= 0) & (diff <= WINDOW)))
    o_win = jnp.dot(p_win.astype(BF16), vw, preferred_element_type=F32)

    gs = jax.nn.sigmoid(gl_ref[0] + bg_ref[0])
    outs = []
    for g in range(Q_PER_KV):
        r0, r1 = g * bq, (g + 1) * bq
        g_cmp = jnp.broadcast_to(gs[:, 3 * g:3 * g + 1], (bq, HEAD_DIM))
        g_slc = jnp.broadcast_to(gs[:, 3 * g + 1:3 * g + 2], (bq, HEAD_DIM))
        g_win = jnp.broadcast_to(gs[:, 3 * g + 2:3 * g + 3], (bq, HEAD_DIM))
        outs.append(g_cmp * o_cmp[r0:r1] + g_slc * o_slc[r0:r1] + g_win * o_win[r0:r1])
    o = jnp.concatenate(outs, axis=1)
    o_ref[0] = (o * jax.nn.silu(z_ref[0])).astype(o_ref.dtype)


def nsa_prompt_attention(pj, kvc, b_gate, *, bq=128, tk=512):
    n, t, _ = pj.shape
    tk = min(tk, t)
    ncp = kvc.shape[3]
    n_cmp = ncp - 1
    n_sel = -(-t // SLC_BLOCK)
    n_top = min(SLC_TOP, n_sel)
    wlen = min(t, WINDOW + bq)
    c = np.arange(ncp)[:, None]
    sblk = np.arange(LANES)[None, :]
    ovl = ((c * CMP_STRIDE < sblk * SLC_BLOCK + SLC_BLOCK) & (c * CMP_STRIDE + CMP_LEN > sblk * SLC_BLOCK)
           & (sblk < n_sel) & (c < n_cmp))
    ovl = jnp.asarray(ovl, BF16)
    kk = np.arange(t).reshape(t // tk, 1, tk)
    expand = jnp.asarray(kk // SLC_BLOCK == np.arange(LANES)[None, :, None], BF16)
    bg = jnp.zeros((N_KV, 1, LANES), F32).at[:, 0, :3 * Q_PER_KV].set(
        b_gate.astype(F32).reshape(N_KV, 3 * Q_PER_KV))
    kern = functools.partial(_nsa_prompt_kernel, bq=bq, tk=tk, t=t, n_sel=n_sel, n_top=n_top, wlen=wlen)
    rows = Q_PER_KV * bq
    gcb = GROUP_COLS // LANES
    return pl.pallas_call(
        kern,
        grid=(n, N_KV, t // bq),
        in_specs=[
            pl.BlockSpec((1, bq, GROUP_COLS), lambda i, h, qb: (i, qb, h)),
            pl.BlockSpec((1, 1, 1, ncp, HEAD_DIM), lambda i, h, qb: (i, 0, h, 0, 0)),
            pl.BlockSpec((1, 1, 1, ncp, HEAD_DIM), lambda i, h, qb: (i, 1, h, 0, 0)),
            pl.BlockSpec((1, t, LANES), lambda i, h, qb: (i, 0, COL_KV + 2 * N_KV + h)),
            pl.BlockSpec((1, t, LANES), lambda i, h, qb: (i, 0, COL_KV + 3 * N_KV + h)),
            pl.BlockSpec((1, t, LANES), lambda i, h, qb: (i, 0, COL_WIN + h)),
            pl.BlockSpec((1, t, LANES), lambda i, h, qb: (i, 0, COL_WIN + N_KV + h)),
            pl.BlockSpec((1, bq, GROUP_COLS), lambda i, h, qb: (i, qb, COL_Z // gcb + h)),
            pl.BlockSpec((1, bq, LANES), lambda i, h, qb: (i, qb, COL_GATE + h)),
            pl.BlockSpec((1, 1, LANES), lambda i, h, qb: (h, 0, 0)),
            pl.BlockSpec((ncp, LANES), lambda i, h, qb: (0, 0)),
            pl.BlockSpec((t // tk, LANES, tk), lambda i, h, qb: (0, 0, 0)),
        ],
        out_specs=pl.BlockSpec((1, bq, GROUP_COLS), lambda i, h, qb: (i, qb, h)),
        out_shape=jax.ShapeDtypeStruct((n, t, NSA_Q_COLS), BF16),
        scratch_shapes=[pltpu.VMEM((rows, LANES), F32), pltpu.VMEM((rows, LANES), F32),
                        pltpu.VMEM((rows, HEAD_DIM), F32)],
        compiler_params=pltpu.CompilerParams(
            dimension_semantics=("parallel", "parallel", "arbitrary"),
            vmem_limit_bytes=V7X_VMEM_LIMIT_BYTES),
        name="nsa_prompt_attention",
    )(pj, kvc, kvc, pj, pj, pj, pj, pj, pj, bg, ovl, expand)


def nsa_layer(x, kv_past, win_past, w_in, b_gate, cmp_pe, cmp_w1, cmp_b1, cmp_w2, cmp_b2, w_out):
    n, t, _ = x.shape
    pj = proj(x, w_in)
    kv_new = pj[..., COL_KV * LANES:COL_WIN * LANES].reshape(n, t, 4, N_KV, HEAD_DIM)
    win_new = pj[..., COL_WIN * LANES:COL_Z * LANES].reshape(n, t, 2, N_KV, HEAD_DIM)
    if kv_past is None:
        kvc = nsa_compress_prompt(pj, cmp_pe, cmp_w1, cmp_b1, cmp_w2, cmp_b2)
        o = nsa_prompt_attention(pj, kvc, b_gate)
        y = matmul(o.reshape(n * t, NSA_Q_COLS), w_out).reshape(n, t, -1)
        return y, (kv_new, win_new[:, -min(WINDOW, t):])
    q = pj[..., :NSA_Q_COLS]
    z = pj[..., COL_Z * LANES:COL_GATE * LANES]
    gl = pj[..., COL_GATE * LANES:].reshape(n, t, N_KV, LANES)[..., :3 * Q_PER_KV]
    past = kv_past.shape[1]
    kv_all = jnp.concatenate([kv_past.astype(kv_new.dtype), kv_new], axis=1)
    q_pos = past + jnp.arange(t)
    qg = q.reshape(n, t, N_KV, Q_PER_KV, HEAD_DIM).astype(F32) * (HEAD_DIM ** -0.5)
    o_cmp, o_slc = cmp_slc_branches(qg, q_pos, kv_all, cmp_pe, cmp_w1, cmp_b1, cmp_w2, cmp_b2)
    keys = jnp.concatenate([win_past.astype(win_new.dtype), win_new], axis=1)
    lb = win_past.shape[1]
    k_pos = past - lb + jnp.arange(lb + t)
    o_win = window_attend(qg, q_pos, keys[:, :, 0], keys[:, :, 1], k_pos)
    win_state = keys[:, -min(WINDOW, lb + t):]
    g = jax.nn.sigmoid(gl.astype(F32).reshape(n, t, N_HEADS, 3) + b_gate.astype(F32))
    g = g.reshape(n, t, N_KV, Q_PER_KV, 3)
    o = g[..., 0:1] * o_cmp + g[..., 1:2] * o_slc + g[..., 2:3] * o_win
    o = o.reshape(n, t, NSA_Q_COLS)
    return proj(o * jax.nn.silu(z), w_out), (kv_new, win_state)


def _nsa_w_in_layout(w_in):
    nsa_gate = 3 * N_HEADS
    a = COL_Z * LANES
    d = w_in.shape[0]
    w_gate = w_in[:, a:a + nsa_gate].reshape(d, N_KV, 3 * Q_PER_KV)
    w_gate = jnp.pad(w_gate, ((0, 0), (0, 0), (0, LANES - 3 * Q_PER_KV))).reshape(d, N_KV * LANES)
    return jnp.concatenate([w_in[:, :a], w_in[:, a + nsa_gate:], w_gate], axis=1)


def kernel(x_prompt, x_sample, state_l0_ssm, state_l1_conv, state_l1_rnn, cache_l2_kv, cache_l2_win, state_l3_ssm, page_table, l0_norm_pre, l0_norm_post, l0_w_in, l0_a_re, l0_a_im, l0_log_dt, l0_b, l0_c, l0_d, l0_w_glu, l0_b_glu, l0_w_out, l1_norm_pre, l1_norm_post, l1_w_in, l1_conv_w, l1_conv_b, l1_w_rg, l1_b_rg, l1_w_ig, l1_b_ig, l1_lam, l1_w_out, l2_norm_pre, l2_norm_post, l2_w_in, l2_b_gate, l2_cmp_pe, l2_cmp_w1, l2_cmp_b1, l2_cmp_w2, l2_cmp_b2, l2_w_out, l3_norm_pre, l3_norm_post, l3_w_in, l3_a_re, l3_a_im, l3_log_dt, l3_b, l3_c, l3_d, l3_w_glu, l3_b_glu, l3_w_out):
    bp = x_prompt.shape[0]
    dt = x_prompt.dtype
    d_model = x_prompt.shape[-1]
    s5_groups = l0_a_re.shape[0]
    rg_width = l1_lam.shape[0]
    bf = lambda w: w.astype(BF16)
    l0_p = s5_params(l0_w_in, l0_a_re, l0_a_im, l0_log_dt, l0_b, l0_c, l0_d, l0_w_glu, l0_b_glu, l0_w_out)
    l3_p = s5_params(l3_w_in, l3_a_re, l3_a_im, l3_log_dt, l3_b, l3_c, l3_d, l3_w_glu, l3_b_glu, l3_w_out)
    l1_wi, l1_wo = bf(l1_w_in), bf(l1_w_out)
    l2_wi, l2_wo = bf(_nsa_w_in_layout(l2_w_in)), bf(l2_w_out)
    layers = (
        (l0_norm_pre, l0_norm_post, lambda h, st: s5_layer(h, st[0], l0_p)),
        (l1_norm_pre, l1_norm_post,
         lambda h, st: rglru_layer(h, st[0], st[1], l1_wi, l1_conv_w, l1_conv_b, l1_w_rg, l1_b_rg,
                                   l1_w_ig, l1_b_ig, l1_lam, l1_wo)),
        (l2_norm_pre, l2_norm_post,
         lambda h, st: nsa_layer(h, st[0], st[1], l2_wi, l2_b_gate, l2_cmp_pe, l2_cmp_w1, l2_cmp_b1,
                                 l2_cmp_w2, l2_cmp_b2, l2_wo)),
        (l3_norm_pre, l3_norm_post, lambda h, st: s5_layer(h, st[0], l3_p)),
    )
    prompt_state = (
        (jnp.zeros((bp, 2, s5_groups, S5_STATE), dt),),
        (jnp.zeros((bp, RG_CONV - 1, rg_width), dt), jnp.zeros((bp, rg_width), dt)),
        (None, None),
        (jnp.zeros((bp, 2, s5_groups, S5_STATE), dt),),
    )
    sample_state = (
        (state_l0_ssm,),
        (state_l1_conv, state_l1_rnn),
        (gather_pages(cache_l2_kv, page_table), cache_l2_win),
        (state_l3_ssm,),
    )
    hp, hs = x_prompt, x_sample
    new_p, new_s = [], []
    for i in range(4):
        g_pre, g_post, mix = layers[i]
        outs = []
        for h, st in ((hp, prompt_state[i]), (hs, sample_state[i])):
            n, t, _ = h.shape
            h2 = h.reshape(n * t, d_model)
            xn = rmsnorm(h2, g_pre, BF16).reshape(n, t, d_model)
            y, s_new = mix(xn, st)
            h_new = post_norm_residual(h2, y.reshape(n * t, d_model), g_post).reshape(n, t, d_model)
            outs.append((h_new, s_new))
        (hp, sp), (hs, ss) = outs
        new_p.append(sp)
        new_s.append(ss)
    (l0_ssm_p,), (l1_conv_p, l1_rnn_p), (l2_kv_p, l2_win_p), (l3_ssm_p,) = new_p
    (l0_ssm_s,), (l1_conv_s, l1_rnn_s), (l2_kv_s, l2_win_s), (l3_ssm_s,) = new_s
    return (hp, hs, l0_ssm_p, l0_ssm_s, l1_conv_p, l1_rnn_p, l1_conv_s, l1_rnn_s,
            l2_kv_p, l2_win_p, l2_kv_s, l2_win_s, l3_ssm_p, l3_ssm_s)
```

```python
import functools
import math

import numpy as np
import jax
import jax.numpy as jnp
from jax import lax
from jax.experimental import pallas as pl
from jax.experimental.pallas import tpu as pltpu

F32 = jnp.float32
BF16 = jnp.bfloat16

EPS = 1e-6
NEG = -1e30
TINY = 1e-30

LANES = 128
SUBLANES = 8

S5_GROUP = 16
S5_STATE = 64
S5_TILE_GROUPS = SUBLANES * LANES // S5_STATE

RG_CONV = 4
RG_C = 8.0

N_HEADS = 32
HEAD_DIM = 128
N_KV = 4
Q_PER_KV = N_HEADS // N_KV
CMP_LEN = 32
CMP_STRIDE = 16
SLC_BLOCK = 64
SLC_TOP = 16
FORCE_SCORE = 1e3
WINDOW = 512
SLC_QBLOCK = 32

NSA_Q_COLS = N_HEADS * HEAD_DIM
COL_KV = NSA_Q_COLS // LANES
COL_WIN = COL_KV + 4 * N_KV
COL_Z = COL_WIN + 2 * N_KV
COL_GATE = COL_Z + N_HEADS
GROUP_COLS = Q_PER_KV * HEAD_DIM

V7X_VMEM_LIMIT_BYTES = 48 * 1024 * 1024


def _pick_tile(n, target, align):
    if n <= target:
        return n
    t = (target // align) * align
    while t >= align:
        if n % t == 0:
            return t
        t -= align
    return n


def _rmsnorm_kernel(x_ref, g_ref, o_ref):
    x = x_ref[...]
    ms = jnp.mean(x * x, axis=-1, keepdims=True)
    o_ref[...] = (x * lax.rsqrt(ms + EPS) * g_ref[...]).astype(o_ref.dtype)


def rmsnorm(x2d, g, out_dtype):
    m, d = x2d.shape
    tm = _pick_tile(m, 256, 8)
    return pl.pallas_call(
        _rmsnorm_kernel,
        grid=(m // tm,),
        in_specs=[pl.BlockSpec((tm, d), lambda i: (i, 0)),
                  pl.BlockSpec((1, d), lambda i: (0, 0))],
        out_specs=pl.BlockSpec((tm, d), lambda i: (i, 0)),
        out_shape=jax.ShapeDtypeStruct((m, d), out_dtype),
        compiler_params=pltpu.CompilerParams(dimension_semantics=("parallel",),
                                             vmem_limit_bytes=V7X_VMEM_LIMIT_BYTES),
        name="rmsnorm",
    )(x2d, g.reshape(1, d))


def _post_norm_residual_kernel(h_ref, y_ref, g_ref, o_ref):
    y = y_ref[...]
    ms = jnp.mean(y * y, axis=-1, keepdims=True)
    o_ref[...] = h_ref[...] + y * lax.rsqrt(ms + EPS) * g_ref[...]


def post_norm_residual(h2d, y2d, g):
    m, d = h2d.shape
    tm = _pick_tile(m, 256, 8)
    return pl.pallas_call(
        _post_norm_residual_kernel,
        grid=(m // tm,),
        in_specs=[pl.BlockSpec((tm, d), lambda i: (i, 0)),
                  pl.BlockSpec((tm, d), lambda i: (i, 0)),
                  pl.BlockSpec((1, d), lambda i: (0, 0))],
        out_specs=pl.BlockSpec((tm, d), lambda i: (i, 0)),
        out_shape=jax.ShapeDtypeStruct((m, d), F32),
        compiler_params=pltpu.CompilerParams(dimension_semantics=("parallel",),
                                             vmem_limit_bytes=V7X_VMEM_LIMIT_BYTES),
        name="post_norm_residual",
    )(h2d, y2d, g.reshape(1, d))


def _matmul_kernel(a_ref, b_ref, o_ref, acc_ref):
    k = pl.program_id(2)

    @pl.when(k == 0)
    def _():
        acc_ref[...] = jnp.zeros_like(acc_ref)

    acc_ref[...] += jnp.dot(a_ref[...], b_ref[...], preferred_element_type=F32)

    @pl.when(k == pl.num_programs(2) - 1)
    def _():
        o_ref[...] = acc_ref[...].astype(o_ref.dtype)


def matmul(a, b, out_dtype=F32):
    m, k = a.shape
    _, n = b.shape
    tm = _pick_tile(m, 1024, 8)
    tn = _pick_tile(n, 512, 128)
    tk = _pick_tile(k, 4096, 128)
    return pl.pallas_call(
        _matmul_kernel,
        grid=(m // tm, n // tn, k // tk),
        in_specs=[pl.BlockSpec((tm, tk), lambda i, j, kk: (i, kk)),
                  pl.BlockSpec((tk, tn), lambda i, j, kk: (kk, j))],
        out_specs=pl.BlockSpec((tm, tn), lambda i, j, kk: (i, j)),
        out_shape=jax.ShapeDtypeStruct((m, n), out_dtype),
        scratch_shapes=[pltpu.VMEM((tm, tn), F32)],
        compiler_params=pltpu.CompilerParams(
            dimension_semantics=("parallel", "parallel", "arbitrary"),
            vmem_limit_bytes=V7X_VMEM_LIMIT_BYTES),
        name="matmul",
    )(a, b)


def proj(x, w_bf16):
    n, t, k = x.shape
    return matmul(x.reshape(n * t, k).astype(BF16), w_bf16).reshape(n, t, -1)


def masked_softmax(s, mask):
    s = jnp.where(mask, s.astype(F32), NEG)
    p = jnp.exp(s - jnp.max(s, axis=-1, keepdims=True)) * mask
    return p / jnp.maximum(jnp.sum(p, axis=-1, keepdims=True), TINY)


def s5_discretize(a_re, a_im, log_dt, b, c):
    dt = jnp.exp(log_dt.astype(F32))[:, None]
    ar, ai = a_re.astype(F32), a_im.astype(F32)
    mag = jnp.exp(ar * dt)
    abar_r, abar_i = mag * jnp.cos(ai * dt), mag * jnp.sin(ai * dt)
    den = ar * ar + ai * ai
    coef_r = ((abar_r - 1.0) * ar + abar_i * ai) / den
    coef_i = (abar_i * ar - (abar_r - 1.0) * ai) / den
    b_r, b_i = b[0].astype(F32), b[1].astype(F32)
    bbar_r = coef_r[..., None] * b_r - coef_i[..., None] * b_i
    bbar_i = coef_r[..., None] * b_i + coef_i[..., None] * b_r
    return abar_r, abar_i, bbar_r, bbar_i, c[0].astype(F32), c[1].astype(F32)


def s5_tile_weights(bbar_r, bbar_i, c_r, c_i):
    g = bbar_r.shape[0]
    tg = S5_TILE_GROUPS
    nt = g // tg
    eye = jnp.eye(tg, dtype=bool)[None, :, None, :, None]

    def expand(x):
        xt = x.transpose(0, 1, 3, 2)[:, :, :, None, :]
        full = jnp.where(eye, xt, 0.0)
        return full.reshape(nt, tg * x.shape[3], tg * x.shape[2])

    br = expand(bbar_r.reshape(nt, tg, S5_STATE, S5_GROUP))
    bi = expand(bbar_i.reshape(nt, tg, S5_STATE, S5_GROUP))
    cr = expand(c_r.reshape(nt, tg, S5_GROUP, S5_STATE))
    ci = expand(c_i.reshape(nt, tg, S5_GROUP, S5_STATE))
    bcat = jnp.concatenate([br, bi], axis=-1).astype(BF16)
    ccat = jnp.concatenate([cr, -ci], axis=1).astype(BF16)
    return bcat, ccat


def _s5_prompt_kernel(u_ref, bcat_ref, ccat_ref, ar_ref, ai_ref, d_ref, h0_ref,
                      g_ref, hlast_ref, *scr, tc, tp, tg):
    bur_scrs, bui_scrs, h_scr = scr[:tg], scr[tg:2 * tg], scr[2 * tg]
    c = pl.program_id(2)
    sw = S5_TILE_GROUPS * S5_STATE
    cw = S5_TILE_GROUPS * S5_GROUP

    @pl.when(c == 0)
    def _():
        h_scr[...] = h0_ref[0]

    u = u_ref[0]
    ub = u.astype(BF16)
    for k in range(tg):
        bu = jnp.dot(ub[:, k * cw:(k + 1) * cw], bcat_ref[k], preferred_element_type=F32)
        for s in range(SUBLANES):
            bur_scrs[k][pl.ds(s * tp, tc), :] = bu[:, s * LANES:(s + 1) * LANES]
            bui_scrs[k][pl.ds(s * tp, tc), :] = bu[:, sw + s * LANES:sw + (s + 1) * LANES]

    ar = [ar_ref[k] for k in range(tg)]
    ai = [ai_ref[k] for k in range(tg)]

    def step(t, carry):
        new = []
        for k in range(tg):
            hr, hi = carry[2 * k], carry[2 * k + 1]
            br = bur_scrs[k][pl.ds(t, SUBLANES, stride=tp), :]
            bi = bui_scrs[k][pl.ds(t, SUBLANES, stride=tp), :]
            hr2 = ar[k] * hr - ai[k] * hi + br
            hi2 = ar[k] * hi + ai[k] * hr + bi
            bur_scrs[k][pl.ds(t, SUBLANES, stride=tp), :] = hr2
            bui_scrs[k][pl.ds(t, SUBLANES, stride=tp), :] = hi2
            new += [hr2, hi2]
        return tuple(new)

    init = tuple(h_scr[p, k] for k in range(tg) for p in (0, 1))
    fin = lax.fori_loop(0, tc, step, init)
    for k in range(tg):
        h_scr[0, k] = fin[2 * k]
        h_scr[1, k] = fin[2 * k + 1]

    for k in range(tg):
        pieces = [bur_scrs[k][pl.ds(s * tp, tc), :].astype(BF16) for s in range(SUBLANES)]
        pieces += [bui_scrs[k][pl.ds(s * tp, tc), :].astype(BF16) for s in range(SUBLANES)]
        hcat = jnp.concatenate(pieces, axis=-1)
        y = jnp.dot(hcat, ccat_ref[k], preferred_element_type=F32)
        yk = y + d_ref[:, k * cw:(k + 1) * cw] * u[:, k * cw:(k + 1) * cw]
        g_ref[0, :, k * cw:(k + 1) * cw] = jax.nn.gelu(yk).astype(g_ref.dtype)

    @pl.when(c == pl.num_programs(2) - 1)
    def _():
        hlast_ref[0] = h_scr[...]


def s5_prompt(uz, h0, abar_r, abar_i, bcat, ccat, d, *, tc=128, tg=4):
    n, t, w2 = uz.shape
    w = w2 // 2
    g = w // S5_GROUP
    nt = g // S5_TILE_GROUPS
    tg = min(tg, nt)
    tc = min(tc, t)
    tp = tc + SUBLANES
    cw = S5_TILE_GROUPS * S5_GROUP
    sw = S5_TILE_GROUPS * S5_STATE
    h0t = h0.reshape(n, 2, nt, SUBLANES, LANES)
    art = abar_r.reshape(nt, SUBLANES, LANES)
    ait = abar_i.reshape(nt, SUBLANES, LANES)
    kern = functools.partial(_s5_prompt_kernel, tc=tc, tp=tp, tg=tg)
    gout, hlast = pl.pallas_call(
        kern,
        grid=(nt // tg, n, t // tc),
        in_specs=[
            pl.BlockSpec((1, tc, tg * cw), lambda j, i, c: (i, c, j)),
            pl.BlockSpec((tg, cw, 2 * sw), lambda j, i, c: (j, 0, 0)),
            pl.BlockSpec((tg, 2 * sw, cw), lambda j, i, c: (j, 0, 0)),
            pl.BlockSpec((tg, SUBLANES, LANES), lambda j, i, c: (j, 0, 0)),
            pl.BlockSpec((tg, SUBLANES, LANES), lambda j, i, c: (j, 0, 0)),
            pl.BlockSpec((1, tg * cw), lambda j, i, c: (0, j)),
            pl.BlockSpec((1, 2, tg, SUBLANES, LANES), lambda j, i, c: (i, 0, j, 0, 0)),
        ],
        out_specs=[
            pl.BlockSpec((1, tc, tg * cw), lambda j, i, c: (i, c, j)),
            pl.BlockSpec((1, 2, tg, SUBLANES, LANES), lambda j, i, c: (i, 0, j, 0, 0)),
        ],
        out_shape=[jax.ShapeDtypeStruct((n, t, w), BF16),
                   jax.ShapeDtypeStruct((n, 2, nt, SUBLANES, LANES), F32)],
        scratch_shapes=[pltpu.VMEM((SUBLANES * tp, LANES), F32)] * (2 * tg)
        + [pltpu.VMEM((2, tg, SUBLANES, LANES), F32)],
        compiler_params=pltpu.CompilerParams(
            dimension_semantics=("parallel", "parallel", "arbitrary"),
            vmem_limit_bytes=V7X_VMEM_LIMIT_BYTES),
        name="s5_prompt",
    )(uz, bcat, ccat, art, ait, d.reshape(1, w), h0t)
    return gout, hlast.reshape(n, 2, g, S5_STATE)


def _s5_step_kernel(u_ref, bcat_ref, ccat_ref, ar_ref, ai_ref, d_ref, h0_ref, g_ref, h_ref):
    sw = S5_TILE_GROUPS * S5_STATE
    u = u_ref[...]
    bu = jnp.dot(u.astype(BF16), bcat_ref[0], preferred_element_type=F32)
    hr, hi = h0_ref[:, 0, :], h0_ref[:, 1, :]
    ar, ai = ar_ref[0], ai_ref[0]
    hr2 = ar * hr - ai * hi + bu[:, :sw]
    hi2 = ar * hi + ai * hr + bu[:, sw:]
    h_ref[:, 0, :] = hr2
    h_ref[:, 1, :] = hi2
    hcat = jnp.concatenate([hr2.astype(BF16), hi2.astype(BF16)], axis=-1)
    y = jnp.dot(hcat, ccat_ref[0], preferred_element_type=F32)
    g_ref[...] = jax.nn.gelu(y + d_ref[...] * u).astype(g_ref.dtype)


def s5_step(uz, h0, abar_r, abar_i, bcat, ccat, d):
    n, w2 = uz.shape
    w = w2 // 2
    g = w // S5_GROUP
    nt = g // S5_TILE_GROUPS
    cw = S5_TILE_GROUPS * S5_GROUP
    sw = S5_TILE_GROUPS * S5_STATE
    gout, hnew = pl.pallas_call(
        _s5_step_kernel,
        grid=(nt,),
        in_specs=[
            pl.BlockSpec((n, cw), lambda j: (0, j)),
            pl.BlockSpec((1, cw, 2 * sw), lambda j: (j, 0, 0)),
            pl.BlockSpec((1, 2 * sw, cw), lambda j: (j, 0, 0)),
            pl.BlockSpec((1, 1, sw), lambda j: (j, 0, 0)),
            pl.BlockSpec((1, 1, sw), lambda j: (j, 0, 0)),
            pl.BlockSpec((1, cw), lambda j: (0, j)),
            pl.BlockSpec((n, 2, sw), lambda j: (0, 0, j)),
        ],
        out_specs=[pl.BlockSpec((n, cw), lambda j: (0, j)),
                   pl.BlockSpec((n, 2, sw), lambda j: (0, 0, j))],
        out_shape=[jax.ShapeDtypeStruct((n, w), BF16),
                   jax.ShapeDtypeStruct((n, 2, g * S5_STATE), F32)],
        compiler_params=pltpu.CompilerParams(dimension_semantics=("parallel",),
                                             vmem_limit_bytes=V7X_VMEM_LIMIT_BYTES),
        name="s5_step",
    )(uz, bcat, ccat, abar_r.reshape(nt, 1, sw), abar_i.reshape(nt, 1, sw), d.reshape(1, w),
      h0.reshape(n, 2, g * S5_STATE))
    return gout, hnew.reshape(n, 2, g, S5_STATE)


def s5_params(w_in, a_re, a_im, log_dt, b, c, d, w_glu, b_glu, w_out):
    abar_r, abar_i, bbar_r, bbar_i, c_r, c_i = s5_discretize(a_re, a_im, log_dt, b, c)
    bcat, ccat = s5_tile_weights(bbar_r, bbar_i, c_r, c_i)
    return dict(w_in=w_in.astype(BF16), abar_r=abar_r, abar_i=abar_i, bcat=bcat, ccat=ccat,
                d=d.astype(F32), w_glu=w_glu.astype(BF16), b_glu=b_glu, w_out=w_out.astype(BF16))


def _matmul_glu_kernel(a_ref, b_ref, bias_ref, g_ref, z_ref, o_ref, acc_ref):
    k = pl.program_id(2)

    @pl.when(k == 0)
    def _():
        acc_ref[...] = jnp.zeros_like(acc_ref)

    acc_ref[...] += jnp.dot(a_ref[...], b_ref[...], preferred_element_type=F32)

    @pl.when(k == pl.num_programs(2) - 1)
    def _():
        gate = jax.nn.sigmoid(acc_ref[...] + bias_ref[...])
        o_ref[...] = (g_ref[...].astype(F32) * gate * jax.nn.silu(z_ref[...])).astype(o_ref.dtype)


def matmul_glu(g, w, bias, uz):
    m, k = g.shape
    _, n = w.shape
    tm = _pick_tile(m, 1024, 8)
    tn = _pick_tile(n, 512, LANES)
    tk = _pick_tile(k, 4096, LANES)
    z_off = n // tn
    return pl.pallas_call(
        _matmul_glu_kernel,
        grid=(m // tm, n // tn, k // tk),
        in_specs=[pl.BlockSpec((tm, tk), lambda i, j, kk: (i, kk)),
                  pl.BlockSpec((tk, tn), lambda i, j, kk: (kk, j)),
                  pl.BlockSpec((1, tn), lambda i, j, kk: (0, j)),
                  pl.BlockSpec((tm, tn), lambda i, j, kk: (i, j)),
                  pl.BlockSpec((tm, tn), lambda i, j, kk: (i, z_off + j))],
        out_specs=pl.BlockSpec((tm, tn), lambda i, j, kk: (i, j)),
        out_shape=jax.ShapeDtypeStruct((m, n), BF16),
        scratch_shapes=[pltpu.VMEM((tm, tn), F32)],
        compiler_params=pltpu.CompilerParams(
            dimension_semantics=("parallel", "parallel", "arbitrary"),
            vmem_limit_bytes=V7X_VMEM_LIMIT_BYTES),
        name="matmul_glu",
    )(g, w, bias.astype(F32).reshape(1, n), g, uz)


def s5_layer(x, h0, p):
    n, t, _ = x.shape
    uz = proj(x, p["w_in"])
    w = uz.shape[-1] // 2
    if t == 1:
        g, h = s5_step(uz[:, 0], h0, p["abar_r"], p["abar_i"], p["bcat"], p["ccat"], p["d"])
    else:
        g, h = s5_prompt(uz, h0, p["abar_r"], p["abar_i"], p["bcat"], p["ccat"], p["d"])
    gz = matmul_glu(g.reshape(n * t, w), p["w_glu"], p["b_glu"], uz.reshape(n * t, 2 * w))
    return matmul(gz, p["w_out"]).reshape(n, t, -1), (h,)


def _round_up(x, m):
    return -(-x // m) * m


def rg_pad_cols(x, nb, bp):
    lead = x.shape[:-1]
    blk = x.shape[-1] // nb
    x = x.reshape(lead + (nb, blk))
    x = jnp.pad(x, [(0, 0)] * len(lead) + [(0, 0), (0, bp - blk)])
    return x.reshape(lead + (nb * bp,))


def rg_unpad_cols(x, nb, blk):
    lead = x.shape[:-1]
    bp = x.shape[-1] // nb
    return x.reshape(lead + (nb, bp))[..., :blk].reshape(lead + (nb * blk,))


def rg_params(w_in, conv_w, conv_b, w_rg, b_rg, w_ig, b_ig, lam, w_out):
    nb, blk, _ = w_rg.shape
    bp = _round_up(blk, LANES)
    width = nb * blk
    w_in_p = jnp.concatenate([rg_pad_cols(w_in[:, :width], nb, bp), rg_pad_cols(w_in[:, width:], nb, bp)],
                             axis=1).astype(BF16)
    pad_sq = lambda w: jnp.pad(w, ((0, 0), (0, bp - blk), (0, bp - blk)))
    w_gate = jnp.concatenate([pad_sq(w_rg), pad_sq(w_ig)], axis=-1).astype(BF16)
    b_gate = jnp.stack([rg_pad_cols(b_rg.astype(F32), nb, bp).reshape(nb, bp),
                        rg_pad_cols(b_ig.astype(F32), nb, bp).reshape(nb, bp)], axis=1)
    b_gate = b_gate.reshape(nb, 1, 2 * bp)
    sp = rg_pad_cols(jax.nn.softplus(-lam.astype(F32)), nb, bp).reshape(1, nb * bp)
    w_out_p = jnp.pad(w_out.reshape(nb, blk, -1), ((0, 0), (0, bp - blk), (0, 0))).reshape(nb * bp, -1)
    return dict(nb=nb, blk=blk, bp=bp, w_in=w_in_p, conv_w=rg_pad_cols(conv_w.astype(F32), nb, bp),
                conv_b=rg_pad_cols(conv_b.astype(F32), nb, bp).reshape(1, nb * bp),
                w_gate=w_gate, b_gate=b_gate, sp=sp, w_out=w_out_p.astype(BF16))


def _rg_gates(uc, wg, bg, sp, bp):
    gates = jnp.dot(uc.astype(BF16), wg, preferred_element_type=F32) + bg
    r = jax.nn.sigmoid(gates[:, :bp])
    ig = jax.nn.sigmoid(gates[:, bp:])
    log_a = -RG_C * r * sp
    a = jnp.exp(log_a)
    xin = jnp.sqrt(1.0 - a * a) * (ig * uc)
    return a, xin


def _rg_prompt_kernel(u_ref, z_ref, cw_ref, cb_ref, wg_ref, bg_ref, sp_ref, cbuf_ref, h0_ref,
                      o_ref, tail_ref, hlast_ref, ext_scr, h_scr, *slabs, tc, pitch):
    c = pl.program_id(2)
    bp = u_ref.shape[-1]
    nl = bp // LANES
    a_scrs, x_scrs = slabs[:nl], slabs[nl:]
    sub = tc // SUBLANES

    @pl.when(c == 0)
    def _():
        ext_scr[0:SUBLANES, :] = cbuf_ref[0]
        h_scr[...] = h0_ref[0]

    ext_scr[pl.ds(SUBLANES, tc), :] = u_ref[0]
    w = cw_ref[...]
    uc = (w[3:4] * ext_scr[pl.ds(SUBLANES, tc), :] + w[2:3] * ext_scr[pl.ds(SUBLANES - 1, tc), :]
          + w[1:2] * ext_scr[pl.ds(SUBLANES - 2, tc), :] + w[0:1] * ext_scr[pl.ds(SUBLANES - 3, tc), :]
          + cb_ref[...])
    ext_scr[0:SUBLANES, :] = ext_scr[pl.ds(tc, SUBLANES), :]

    a, xin = _rg_gates(uc, wg_ref[0], bg_ref[0], sp_ref[...], bp)
    for l in range(nl):
        for s in range(SUBLANES):
            a_scrs[l][pl.ds(s * pitch, sub), :] = a[s * sub:(s + 1) * sub, l * LANES:(l + 1) * LANES]
            x_scrs[l][pl.ds(s * pitch, sub), :] = xin[s * sub:(s + 1) * sub, l * LANES:(l + 1) * LANES]

    def step(j, carry):
        new = []
        for l in range(nl):
            p, s_ = carry[2 * l], carry[2 * l + 1]
            aj = a_scrs[l][pl.ds(j, SUBLANES, stride=pitch), :]
            xj = x_scrs[l][pl.ds(j, SUBLANES, stride=pitch), :]
            p = aj * p
            s_ = aj * s_ + xj
            a_scrs[l][pl.ds(j, SUBLANES, stride=pitch), :] = p
            x_scrs[l][pl.ds(j, SUBLANES, stride=pitch), :] = s_
            new += [p, s_]
        return tuple(new)

    init = tuple(jnp.ones((SUBLANES, LANES), F32) if i % 2 == 0 else jnp.zeros((SUBLANES, LANES), F32)
                 for i in range(2 * nl))
    fin = lax.fori_loop(0, sub, step, init)

    for l in range(nl):
        p_end, s_end = fin[2 * l], fin[2 * l + 1]
        h = h_scr[:, l * LANES:(l + 1) * LANES]
        for s in range(SUBLANES):
            rows = pl.ds(s * pitch, sub)
            hs = x_scrs[l][rows, :] + a_scrs[l][rows, :] * h
            zs = z_ref[0, s * sub:(s + 1) * sub, l * LANES:(l + 1) * LANES]
            o_ref[0, s * sub:(s + 1) * sub, l * LANES:(l + 1) * LANES] = (hs * jax.nn.silu(zs)).astype(o_ref.dtype)
            h = s_end[s:s + 1, :] + p_end[s:s + 1, :] * h
        h_scr[:, l * LANES:(l + 1) * LANES] = h

    @pl.when(c == pl.num_programs(2) - 1)
    def _():
        tail_ref[0] = ext_scr[0:SUBLANES, :]
        hlast_ref[0] = h_scr[...]


def rg_prompt(uz, conv_buf, h0, p, *, tc=512):
    n, t, _ = uz.shape
    nb, blk, bp = p["nb"], p["blk"], p["bp"]
    tc = min(tc, t)
    sub = tc // SUBLANES
    pitch = sub + SUBLANES
    nl = bp // LANES
    cbuf = jnp.pad(rg_pad_cols(conv_buf.astype(F32), nb, bp), ((0, 0), (SUBLANES - (RG_CONV - 1), 0), (0, 0)))
    h0p = rg_pad_cols(h0.astype(F32), nb, bp).reshape(n, 1, nb * bp)
    kern = functools.partial(_rg_prompt_kernel, tc=tc, pitch=pitch)
    o, tail, hlast = pl.pallas_call(
        kern,
        grid=(n, nb, t // tc),
        in_specs=[
            pl.BlockSpec((1, tc, bp), lambda i, b, c: (i, c, b)),
            pl.BlockSpec((1, tc, bp), lambda i, b, c: (i, c, nb + b)),
            pl.BlockSpec((RG_CONV, bp), lambda i, b, c: (0, b)),
            pl.BlockSpec((1, bp), lambda i, b, c: (0, b)),
            pl.BlockSpec((1, bp, 2 * bp), lambda i, b, c: (b, 0, 0)),
            pl.BlockSpec((1, 1, 2 * bp), lambda i, b, c: (b, 0, 0)),
            pl.BlockSpec((1, bp), lambda i, b, c: (0, b)),
            pl.BlockSpec((1, SUBLANES, bp), lambda i, b, c: (i, 0, b)),
            pl.BlockSpec((1, 1, bp), lambda i, b, c: (i, 0, b)),
        ],
        out_specs=[
            pl.BlockSpec((1, tc, bp), lambda i, b, c: (i, c, b)),
            pl.BlockSpec((1, SUBLANES, bp), lambda i, b, c: (i, 0, b)),
            pl.BlockSpec((1, 1, bp), lambda i, b, c: (i, 0, b)),
        ],
        out_shape=[jax.ShapeDtypeStruct((n, t, nb * bp), BF16),
                   jax.ShapeDtypeStruct((n, SUBLANES, nb * bp), F32),
                   jax.ShapeDtypeStruct((n, 1, nb * bp), F32)],
        scratch_shapes=[pltpu.VMEM((tc + SUBLANES, bp), F32), pltpu.VMEM((1, bp), F32)]
        + [pltpu.VMEM((SUBLANES * pitch, LANES), F32)] * (2 * nl),
        compiler_params=pltpu.CompilerParams(
            dimension_semantics=("parallel", "parallel", "arbitrary"),
            vmem_limit_bytes=V7X_VMEM_LIMIT_BYTES),
        name="rg_prompt",
    )(uz, uz, p["conv_w"], p["conv_b"], p["w_gate"], p["b_gate"], p["sp"], cbuf, h0p)
    conv_state = rg_unpad_cols(tail[:, SUBLANES - (RG_CONV - 1):], nb, blk)
    return o, conv_state, rg_unpad_cols(hlast[:, 0], nb, blk)


def _rg_step_kernel(u_ref, z_ref, cw_ref, cb_ref, wg_ref, bg_ref, sp_ref, cbuf_ref, h0_ref, o_ref, h_ref):
    bp = u_ref.shape[-1]
    w = cw_ref[...]
    uc = (w[3:4] * u_ref[...] + w[2:3] * cbuf_ref[2] + w[1:2] * cbuf_ref[1] + w[0:1] * cbuf_ref[0]
          + cb_ref[...])
    a, xin = _rg_gates(uc, wg_ref[0], bg_ref[0], sp_ref[...], bp)
    h = a * h0_ref[...] + xin
    h_ref[...] = h
    o_ref[...] = (h * jax.nn.silu(z_ref[...])).astype(o_ref.dtype)


def rg_step(uz, conv_buf, h0, p):
    n, _ = uz.shape
    nb, blk, bp = p["nb"], p["blk"], p["bp"]
    cbuf = rg_pad_cols(conv_buf.astype(F32), nb, bp).transpose(1, 0, 2)
    h0p = rg_pad_cols(h0.astype(F32), nb, bp)
    o, hnew = pl.pallas_call(
        _rg_step_kernel,
        grid=(nb,),
        in_specs=[
            pl.BlockSpec((n, bp), lambda b: (0, b)),
            pl.BlockSpec((n, bp), lambda b: (0, nb + b)),
            pl.BlockSpec((RG_CONV, bp), lambda b: (0, b)),
            pl.BlockSpec((1, bp), lambda b: (0, b)),
            pl.BlockSpec((1, bp, 2 * bp), lambda b: (b, 0, 0)),
            pl.BlockSpec((1, 1, 2 * bp), lambda b: (b, 0, 0)),
            pl.BlockSpec((1, bp), lambda b: (0, b)),
            pl.BlockSpec((RG_CONV - 1, n, bp), lambda b: (0, 0, b)),
            pl.BlockSpec((n, bp), lambda b: (0, b)),
        ],
        out_specs=[pl.BlockSpec((n, bp), lambda b: (0, b)), pl.BlockSpec((n, bp), lambda b: (0, b))],
        out_shape=[jax.ShapeDtypeStruct((n, nb * bp), BF16), jax.ShapeDtypeStruct((n, nb * bp), F32)],
        compiler_params=pltpu.CompilerParams(dimension_semantics=("parallel",),
                                             vmem_limit_bytes=V7X_VMEM_LIMIT_BYTES),
        name="rg_step",
    )(uz, uz, p["conv_w"], p["conv_b"], p["w_gate"], p["b_gate"], p["sp"], cbuf, h0p)
    u_new = rg_unpad_cols(uz[:, :nb * bp], nb, blk)
    conv_state = jnp.concatenate([conv_buf[:, 1:].astype(F32), u_new[:, None]], axis=1)
    return o, conv_state, rg_unpad_cols(hnew, nb, blk)


def rglru_layer(x, conv_buf, h0, p):
    n, t, _ = x.shape
    uz = proj(x, p["w_in"])
    if t == 1:
        o, conv_state, h = rg_step(uz[:, 0], conv_buf, h0, p)
    else:
        o, conv_state, h = rg_prompt(uz, conv_buf, h0, p)
    y = matmul(o.reshape(n * t, -1), p["w_out"]).reshape(n, t, -1)
    return y, (conv_state, h)


def gather_pages(pool, page_table):
    pages = pool[page_table]
    return pages.reshape((page_table.shape[0], page_table.shape[1] * pool.shape[1]) + pool.shape[2:])


def compress(kv, pe, w1, b1, w2, b2):
    tk = kv.shape[1]
    n_cmp = (tk - CMP_LEN) // CMP_STRIDE + 1
    idx = (jnp.arange(n_cmp) * CMP_STRIDE)[:, None] + jnp.arange(CMP_LEN)[None, :]
    blk = kv[:, idx].astype(F32) + pe[:, None, :]
    hid = jax.nn.gelu(jnp.einsum('nclhd,ldf->nchf', blk, w1) + b1)
    return jnp.einsum('nchf,fd->nchd', hid, w2) + b2


def cmp_slc_branches(qg, q_pos, kv_all, cmp_pe, cmp_w1, cmp_b1, cmp_w2, cmp_b2):
    n, tq = qg.shape[:2]
    tk = kv_all.shape[1]
    kc = compress(kv_all[:, :, 0], cmp_pe[0], cmp_w1[0], cmp_b1[0], cmp_w2[0], cmp_b2[0])
    vc = compress(kv_all[:, :, 1], cmp_pe[1], cmp_w1[1], cmp_b1[1], cmp_w2[1], cmp_b2[1])
    n_cmp = kc.shape[1]
    c_start = jnp.arange(n_cmp) * CMP_STRIDE
    s = jnp.einsum('nqhgd,nchd->nqhgc', qg, kc)
    cmask = (c_start[None, :] + CMP_LEN - 1 <= q_pos[:, None])[None, :, None, None, :]
    p_cmp = masked_softmax(s, cmask)
    o_cmp = jnp.einsum('nqhgc,nchd->nqhgd', p_cmp, vc)
    n_sel = -(-tk // SLC_BLOCK)
    s_start = jnp.arange(n_sel) * SLC_BLOCK
    overlap = ((c_start[:, None] < s_start[None, :] + SLC_BLOCK) &
               (c_start[:, None] + CMP_LEN > s_start[None, :])).astype(F32)
    imp = jnp.einsum('nqhgc,cs->nqhs', p_cmp, overlap)
    cur = (q_pos // SLC_BLOCK)[:, None]
    blk = jnp.arange(n_sel)[None, :]
    forced = ((blk == 0) | (blk == cur) | (blk == cur - 1))[None, :, None, :]
    future = (blk > cur)[None, :, None, :]
    imp = jnp.where(forced, FORCE_SCORE, jnp.where(future, -1.0, imp))
    n_top = min(SLC_TOP, n_sel)
    _, sel = lax.top_k(imp, n_top)
    pad = n_sel * SLC_BLOCK - tk

    def to_blocks(a):
        a = jnp.pad(a, ((0, 0), (0, pad), (0, 0), (0, 0)))
        return a.reshape(n, n_sel, SLC_BLOCK, N_KV, HEAD_DIM).transpose(0, 3, 1, 2, 4)

    kb = to_blocks(kv_all[:, :, 2])
    vb = to_blocks(kv_all[:, :, 3])
    qc = min(SLC_QBLOCK, tq)
    nqc = tq // qc
    q_chunks = qg.reshape(n, nqc, qc, N_KV, Q_PER_KV, HEAD_DIM).swapaxes(0, 1)
    sel_chunks = sel.reshape(n, nqc, qc, N_KV, n_top).swapaxes(0, 1)
    pos_chunks = q_pos.reshape(nqc, qc)
    take = jax.vmap(jax.vmap(lambda blocks, ix: blocks[ix]))

    def chunk(args):
        qi, si, pi = args
        idx = si.transpose(0, 2, 1, 3)
        kg = take(kb, idx).astype(F32)
        vg = take(vb, idx).astype(F32)
        sc = jnp.einsum('nqhgd,nhqtld->nqhgtl', qi, kg).reshape(n, qc, N_KV, Q_PER_KV, n_top * SLC_BLOCK)
        k_pos = idx[..., None] * SLC_BLOCK + jnp.arange(SLC_BLOCK)
        valid = (k_pos <= pi[None, None, :, None, None]).transpose(0, 2, 1, 3, 4)
        valid = valid.reshape(n, qc, N_KV, 1, n_top * SLC_BLOCK)
        p = masked_softmax(sc, valid).reshape(n, qc, N_KV, Q_PER_KV, n_top, SLC_BLOCK)
        return jnp.einsum('nqhgtl,nhqtld->nqhgd', p, vg)

    o_slc = lax.map(chunk, (q_chunks, sel_chunks, pos_chunks))
    o_slc = o_slc.swapaxes(0, 1).reshape(n, tq, N_KV, Q_PER_KV, HEAD_DIM)
    return o_cmp, o_slc


def window_attend(qg, q_pos, k, v, k_pos):
    s = jnp.einsum('nqhgd,nkhd->nqhgk', qg, k.astype(F32))
    diff = q_pos[:, None] - k_pos[None, :]
    mask = ((diff >= 0) & (diff <= WINDOW) & (k_pos[None, :] >= 0))[None, :, None, None, :]
    p = masked_softmax(s, mask)
    return jnp.einsum('nqhgk,nkhd->nqhgd', p, v.astype(F32))


def _masked_softmax_rows(s, ok):
    s = jnp.where(ok, s, NEG)
    mx = jnp.max(s, axis=-1, keepdims=True)
    p = jnp.where(ok, jnp.exp(s - mx), 0.0)
    den = jnp.maximum(jnp.sum(p, axis=-1, keepdims=True), TINY)
    return p / den


def _compress_kernel(x_ref, pe_ref, w1_ref, b1_ref, w2_ref, b2_ref, o_ref, *, nh):
    half = CMP_LEN // 2
    acc_lo = jnp.zeros((nh, w1_ref.shape[-1]), F32)
    acc_hi = jnp.zeros((nh, w1_ref.shape[-1]), F32)
    for l in range(half):
        x = x_ref[pl.ds(l, nh, stride=CMP_STRIDE), :]
        acc_lo += jnp.dot((x + pe_ref[0, l:l + 1, :]).astype(BF16), w1_ref[0, l],
                          preferred_element_type=F32)
        acc_hi += jnp.dot((x + pe_ref[0, half + l:half + l + 1, :]).astype(BF16), w1_ref[0, half + l],
                          preferred_element_type=F32)
    hid = jax.nn.gelu(acc_lo + pltpu.roll(acc_hi, nh - 1, axis=0) + b1_ref[0])
    out = jnp.dot(hid.astype(BF16), w2_ref[0], preferred_element_type=F32) + b2_ref[0]
    row = lax.broadcasted_iota(jnp.int32, out.shape, 0)
    o_ref[0, 0, 0] = jnp.where(row < nh - 1, out, 0.0)


def nsa_compress_prompt(pj, cmp_pe, cmp_w1, cmp_b1, cmp_w2, cmp_b2):
    n, t, _ = pj.shape
    nh = t // CMP_STRIDE
    hidden = cmp_w1.shape[-1]
    kern = functools.partial(_compress_kernel, nh=nh)
    return pl.pallas_call(
        kern,
        grid=(n, 2, N_KV),
        in_specs=[
            pl.BlockSpec((None, t, LANES), lambda i, ty, h: (i, 0, COL_KV + ty * N_KV + h)),
            pl.BlockSpec((1, CMP_LEN, HEAD_DIM), lambda i, ty, h: (ty, 0, 0)),
            pl.BlockSpec((1, CMP_LEN, HEAD_DIM, hidden), lambda i, ty, h: (ty, 0, 0, 0)),
            pl.BlockSpec((1, 1, hidden), lambda i, ty, h: (ty, 0, 0)),
            pl.BlockSpec((1, hidden, HEAD_DIM), lambda i, ty, h: (ty, 0, 0)),
            pl.BlockSpec((1, 1, HEAD_DIM), lambda i, ty, h: (ty, 0, 0)),
        ],
        out_specs=pl.BlockSpec((1, 1, 1, nh, HEAD_DIM), lambda i, ty, h: (i, ty, h, 0, 0)),
        out_shape=jax.ShapeDtypeStruct((n, 2, N_KV, nh, HEAD_DIM), F32),
        compiler_params=pltpu.CompilerParams(
            dimension_semantics=("parallel", "parallel", "parallel"),
            vmem_limit_bytes=V7X_VMEM_LIMIT_BYTES),
        name="nsa_compress_prompt",
    )(pj, cmp_pe, cmp_w1.astype(BF16), cmp_b1.reshape(2, 1, hidden), cmp_w2.astype(BF16),
      cmp_b2.reshape(2, 1, HEAD_DIM))


def _nsa_prompt_kernel(q_ref, kc_ref, vc_ref, ks_ref, vs_ref, kw_ref, vw_ref, z_ref, gl_ref, bg_ref,
                       ovl_ref, exp_ref, o_ref, m_scr, l_scr, acc_scr, *, bq, tk, t, n_sel, n_top, wlen):
    qb = pl.program_id(2)
    q = q_ref[0] * (HEAD_DIM ** -0.5)
    qs = jnp.concatenate([q[:, g * HEAD_DIM:(g + 1) * HEAD_DIM] for g in range(Q_PER_KV)],
                         axis=0).astype(BF16)
    nt_dims = (((1,), (1,)), ((), ()))

    def per_head(x):
        return jnp.concatenate([x] * Q_PER_KV, axis=0)

    qpos = qb * bq + lax.broadcasted_iota(jnp.int32, (bq, LANES), 0)
    lane = lax.broadcasted_iota(jnp.int32, (bq, LANES), 1)

    ncp = kc_ref.shape[3]
    s = lax.dot_general(qs, kc_ref[0, 0, 0].astype(BF16), nt_dims, preferred_element_type=F32)
    cpos = lax.broadcasted_iota(jnp.int32, (bq, ncp), 1) * CMP_STRIDE + (CMP_LEN - 1)
    qpos_c = qb * bq + lax.broadcasted_iota(jnp.int32, (bq, ncp), 0)
    p_cmp = _masked_softmax_rows(s, per_head(cpos <= qpos_c))
    o_cmp = jnp.dot(p_cmp.astype(BF16), vc_ref[0, 0, 0].astype(BF16), preferred_element_type=F32)

    psum = p_cmp[0:bq]
    for g in range(1, Q_PER_KV):
        psum = psum + p_cmp[g * bq:(g + 1) * bq]
    p_hi = psum.astype(BF16)
    p_lo = (psum - p_hi.astype(F32)).astype(BF16)
    imp = (jnp.dot(p_hi, ovl_ref[...], preferred_element_type=F32)
           + jnp.dot(p_lo, ovl_ref[...], preferred_element_type=F32))
    cur = qpos // SLC_BLOCK
    forced = (lane == 0) | (lane == cur) | (lane == cur - 1)
    imp = jnp.where(forced, FORCE_SCORE, jnp.where(lane > cur, -1.0, imp))
    imp = jnp.where(lane < n_sel, imp, -2.0)
    rank = jnp.zeros((bq, LANES), F32)
    for b in range(n_sel):
        col = jnp.broadcast_to(imp[:, b:b + 1], (bq, LANES))
        tie = jnp.where(lane > b, 1.0, 0.0)
        rank = rank + jnp.where(col > imp, 1.0, jnp.where(col == imp, tie, 0.0))
    sel = jnp.where(rank < n_top, 1.0, 0.0).astype(BF16)

    m_scr[...] = jnp.full(m_scr.shape, NEG, F32)
    l_scr[...] = jnp.zeros(l_scr.shape, F32)
    acc_scr[...] = jnp.zeros(acc_scr.shape, F32)
    reps = tk // LANES
    qpos_k = qb * bq + lax.broadcasted_iota(jnp.int32, (bq, tk), 0)
    koff = lax.broadcasted_iota(jnp.int32, (bq, tk), 1)

    def body(kt, carry):
        start = pl.multiple_of(kt * tk, tk)
        k = ks_ref[0, pl.ds(start, tk), :].astype(BF16)
        v = vs_ref[0, pl.ds(start, tk), :].astype(BF16)
        s = lax.dot_general(qs, k, nt_dims, preferred_element_type=F32)
        selk = jnp.dot(sel, exp_ref[kt], preferred_element_type=F32)
        ok = per_head((selk > 0.5) & (kt * tk + koff <= qpos_k))
        s = jnp.where(ok, s, NEG)
        m_old = m_scr[...]
        m_new = jnp.maximum(m_old, jnp.max(s, axis=-1, keepdims=True))
        alpha = jnp.exp(m_old - m_new)
        p = jnp.where(ok, jnp.exp(s - jnp.concatenate([m_new] * reps, axis=1)), 0.0)
        l_scr[...] = alpha * l_scr[...] + jnp.sum(p, axis=-1, keepdims=True)
        acc_scr[...] = alpha * acc_scr[...] + jnp.dot(p.astype(BF16), v, preferred_element_type=F32)
        m_scr[...] = m_new
        return carry

    n_tiles = ((qb + 1) * bq + tk - 1) // tk
    lax.fori_loop(0, n_tiles, body, 0)
    o_slc = acc_scr[...] / jnp.maximum(l_scr[...], TINY)

    wstart = pl.multiple_of(jnp.clip(qb * bq + bq - wlen, 0, t - wlen), bq)
    kw = kw_ref[0, pl.ds(wstart, wlen), :].astype(BF16)
    vw = vw_ref[0, pl.ds(wstart, wlen), :].astype(BF16)
    s = lax.dot_general(qs, kw, nt_dims, preferred_element_type=F32)
    diff = (qb * bq + lax.broadcasted_iota(jnp.int32, (bq, wlen), 0)
            - wstart - lax.broadcasted_iota(jnp.int32, (bq, wlen), 1))
    p_win = _masked_softmax_rows(s, per_head((diff >= 0) & (diff <= WINDOW)))
    o_win = jnp.dot(p_win.astype(BF16), vw, preferred_element_type=F32)

    gs = jax.nn.sigmoid(gl_ref[0] + bg_ref[0])
    outs = []
    for g in range(Q_PER_KV):
        r0, r1 = g * bq, (g + 1) * bq
        g_cmp = jnp.broadcast_to(gs[:, 3 * g:3 * g + 1], (bq, HEAD_DIM))
        g_slc = jnp.broadcast_to(gs[:, 3 * g + 1:3 * g + 2], (bq, HEAD_DIM))
        g_win = jnp.broadcast_to(gs[:, 3 * g + 2:3 * g + 3], (bq, HEAD_DIM))
        outs.append(g_cmp * o_cmp[r0:r1] + g_slc * o_slc[r0:r1] + g_win * o_win[r0:r1])
    o = jnp.concatenate(outs, axis=1)
    o_ref[0] = (o * jax.nn.silu(z_ref[0])).astype(o_ref.dtype)


def nsa_prompt_attention(pj, kvc, b_gate, *, bq=128, tk=512):
    n, t, _ = pj.shape
    tk = min(tk, t)
    ncp = kvc.shape[3]
    n_cmp = ncp - 1
    n_sel = -(-t // SLC_BLOCK)
    n_top = min(SLC_TOP, n_sel)
    wlen = min(t, WINDOW + bq)
    c = np.arange(ncp)[:, None]
    sblk = np.arange(LANES)[None, :]
    ovl = ((c * CMP_STRIDE < sblk * SLC_BLOCK + SLC_BLOCK) & (c * CMP_STRIDE + CMP_LEN > sblk * SLC_BLOCK)
           & (sblk < n_sel) & (c < n_cmp))
    ovl = jnp.asarray(ovl, BF16)
    kk = np.arange(t).reshape(t // tk, 1, tk)
    expand = jnp.asarray(kk // SLC_BLOCK == np.arange(LANES)[None, :, None], BF16)
    bg = jnp.zeros((N_KV, 1, LANES), F32).at[:, 0, :3 * Q_PER_KV].set(
        b_gate.astype(F32).reshape(N_KV, 3 * Q_PER_KV))
    kern = functools.partial(_nsa_prompt_kernel, bq=bq, tk=tk, t=t, n_sel=n_sel, n_top=n_top, wlen=wlen)
    rows = Q_PER_KV * bq
    gcb = GROUP_COLS // LANES
    return pl.pallas_call(
        kern,
        grid=(n, N_KV, t // bq),
        in_specs=[
            pl.BlockSpec((1, bq, GROUP_COLS), lambda i, h, qb: (i, qb, h)),
            pl.BlockSpec((1, 1, 1, ncp, HEAD_DIM), lambda i, h, qb: (i, 0, h, 0, 0)),
            pl.BlockSpec((1, 1, 1, ncp, HEAD_DIM), lambda i, h, qb: (i, 1, h, 0, 0)),
            pl.BlockSpec((1, t, LANES), lambda i, h, qb: (i, 0, COL_KV + 2 * N_KV + h)),
            pl.BlockSpec((1, t, LANES), lambda i, h, qb: (i, 0, COL_KV + 3 * N_KV + h)),
            pl.BlockSpec((1, t, LANES), lambda i, h, qb: (i, 0, COL_WIN + h)),
            pl.BlockSpec((1, t, LANES), lambda i, h, qb: (i, 0, COL_WIN + N_KV + h)),
            pl.BlockSpec((1, bq, GROUP_COLS), lambda i, h, qb: (i, qb, COL_Z // gcb + h)),
            pl.BlockSpec((1, bq, LANES), lambda i, h, qb: (i, qb, COL_GATE + h)),
            pl.BlockSpec((1, 1, LANES), lambda i, h, qb: (h, 0, 0)),
            pl.BlockSpec((ncp, LANES), lambda i, h, qb: (0, 0)),
            pl.BlockSpec((t // tk, LANES, tk), lambda i, h, qb: (0, 0, 0)),
        ],
        out_specs=pl.BlockSpec((1, bq, GROUP_COLS), lambda i, h, qb: (i, qb, h)),
        out_shape=jax.ShapeDtypeStruct((n, t, NSA_Q_COLS), BF16),
        scratch_shapes=[pltpu.VMEM((rows, LANES), F32), pltpu.VMEM((rows, LANES), F32),
                        pltpu.VMEM((rows, HEAD_DIM), F32)],
        compiler_params=pltpu.CompilerParams(
            dimension_semantics=("parallel", "parallel", "arbitrary"),
            vmem_limit_bytes=V7X_VMEM_LIMIT_BYTES),
        name="nsa_prompt_attention",
    )(pj, kvc, kvc, pj, pj, pj, pj, pj, pj, bg, ovl, expand)


def nsa_layer(x, kv_past, win_past, w_in, b_gate, cmp_pe, cmp_w1, cmp_b1, cmp_w2, cmp_b2, w_out):
    n, t, _ = x.shape
    pj = proj(x, w_in)
    kv_new = pj[..., COL_KV * LANES:COL_WIN * LANES].reshape(n, t, 4, N_KV, HEAD_DIM)
    win_new = pj[..., COL_WIN * LANES:COL_Z * LANES].reshape(n, t, 2, N_KV, HEAD_DIM)
    if kv_past is None:
        kvc = nsa_compress_prompt(pj, cmp_pe, cmp_w1, cmp_b1, cmp_w2, cmp_b2)
        o = nsa_prompt_attention(pj, kvc, b_gate)
        y = matmul(o.reshape(n * t, NSA_Q_COLS), w_out).reshape(n, t, -1)
        return y, (kv_new, win_new[:, -min(WINDOW, t):])
    q = pj[..., :NSA_Q_COLS]
    z = pj[..., COL_Z * LANES:COL_GATE * LANES]
    gl = pj[..., COL_GATE * LANES:].reshape(n, t, N_KV, LANES)[..., :3 * Q_PER_KV]
    past = kv_past.shape[1]
    kv_all = jnp.concatenate([kv_past.astype(kv_new.dtype), kv_new], axis=1)
    q_pos = past + jnp.arange(t)
    qg = q.reshape(n, t, N_KV, Q_PER_KV, HEAD_DIM).astype(F32) * (HEAD_DIM ** -0.5)
    o_cmp, o_slc = cmp_slc_branches(qg, q_pos, kv_all, cmp_pe, cmp_w1, cmp_b1, cmp_w2, cmp_b2)
    keys = jnp.concatenate([win_past.astype(win_new.dtype), win_new], axis=1)
    lb = win_past.shape[1]
    k_pos = past - lb + jnp.arange(lb + t)
    o_win = window_attend(qg, q_pos, keys[:, :, 0], keys[:, :, 1], k_pos)
    win_state = keys[:, -min(WINDOW, lb + t):]
    g = jax.nn.sigmoid(gl.astype(F32).reshape(n, t, N_HEADS, 3) + b_gate.astype(F32))
    g = g.reshape(n, t, N_KV, Q_PER_KV, 3)
    o = g[..., 0:1] * o_cmp + g[..., 1:2] * o_slc + g[..., 2:3] * o_win
    o = o.reshape(n, t, NSA_Q_COLS)
    return proj(o * jax.nn.silu(z), w_out), (kv_new, win_state)


def _nsa_w_in_layout(w_in):
    nsa_gate = 3 * N_HEADS
    a = COL_Z * LANES
    d = w_in.shape[0]
    w_gate = w_in[:, a:a + nsa_gate].reshape(d, N_KV, 3 * Q_PER_KV)
    w_gate = jnp.pad(w_gate, ((0, 0), (0, 0), (0, LANES - 3 * Q_PER_KV))).reshape(d, N_KV * LANES)
    return jnp.concatenate([w_in[:, :a], w_in[:, a + nsa_gate:], w_gate], axis=1)


def kernel(x_prompt, x_sample, state_l0_ssm, state_l1_conv, state_l1_rnn, cache_l2_kv, cache_l2_win, state_l3_ssm, page_table, l0_norm_pre, l0_norm_post, l0_w_in, l0_a_re, l0_a_im, l0_log_dt, l0_b, l0_c, l0_d, l0_w_glu, l0_b_glu, l0_w_out, l1_norm_pre, l1_norm_post, l1_w_in, l1_conv_w, l1_conv_b, l1_w_rg, l1_b_rg, l1_w_ig, l1_b_ig, l1_lam, l1_w_out, l2_norm_pre, l2_norm_post, l2_w_in, l2_b_gate, l2_cmp_pe, l2_cmp_w1, l2_cmp_b1, l2_cmp_w2, l2_cmp_b2, l2_w_out, l3_norm_pre, l3_norm_post, l3_w_in, l3_a_re, l3_a_im, l3_log_dt, l3_b, l3_c, l3_d, l3_w_glu, l3_b_glu, l3_w_out):
    bp = x_prompt.shape[0]
    dt = x_prompt.dtype
    d_model = x_prompt.shape[-1]
    s5_groups = l0_a_re.shape[0]
    rg_width = l1_lam.shape[0]
    bf = lambda w: w.astype(BF16)
    l0_p = s5_params(l0_w_in, l0_a_re, l0_a_im, l0_log_dt, l0_b, l0_c, l0_d, l0_w_glu, l0_b_glu, l0_w_out)
    l3_p = s5_params(l3_w_in, l3_a_re, l3_a_im, l3_log_dt, l3_b, l3_c, l3_d, l3_w_glu, l3_b_glu, l3_w_out)
    l1_p = rg_params(l1_w_in, l1_conv_w, l1_conv_b, l1_w_rg, l1_b_rg, l1_w_ig, l1_b_ig, l1_lam, l1_w_out)
    l2_wi, l2_wo = bf(_nsa_w_in_layout(l2_w_in)), bf(l2_w_out)
    layers = (
        (l0_norm_pre, l0_norm_post, lambda h, st: s5_layer(h, st[0], l0_p)),
        (l1_norm_pre, l1_norm_post, lambda h, st: rglru_layer(h, st[0], st[1], l1_p)),
        (l2_norm_pre, l2_norm_post,
         lambda h, st: nsa_layer(h, st[0], st[1], l2_wi, l2_b_gate, l2_cmp_pe, l2_cmp_w1, l2_cmp_b1,
                                 l2_cmp_w2, l2_cmp_b2, l2_wo)),
        (l3_norm_pre, l3_norm_post, lambda h, st: s5_layer(h, st[0], l3_p)),
    )
    prompt_state = (
        (jnp.zeros((bp, 2, s5_groups, S5_STATE), dt),),
        (jnp.zeros((bp, RG_CONV - 1, rg_width), dt), jnp.zeros((bp, rg_width), dt)),
        (None, None),
        (jnp.zeros((bp, 2, s5_groups, S5_STATE), dt),),
    )
    sample_state = (
        (state_l0_ssm,),
        (state_l1_conv, state_l1_rnn),
        (gather_pages(cache_l2_kv, page_table), cache_l2_win),
        (state_l3_ssm,),
    )
    hp, hs = x_prompt, x_sample
    new_p, new_s = [], []
    for i in range(4):
        g_pre, g_post, mix = layers[i]
        outs = []
        for h, st in ((hp, prompt_state[i]), (hs, sample_state[i])):
            n, t, _ = h.shape
            h2 = h.reshape(n * t, d_model)
            xn = rmsnorm(h2, g_pre, BF16).reshape(n, t, d_model)
            y, s_new = mix(xn, st)
            h_new = post_norm_residual(h2, y.reshape(n * t, d_model), g_post).reshape(n, t, d_model)
            outs.append((h_new, s_new))
        (hp, sp), (hs, ss) = outs
        new_p.append(sp)
        new_s.append(ss)
    (l0_ssm_p,), (l1_conv_p, l1_rnn_p), (l2_kv_p, l2_win_p), (l3_ssm_p,) = new_p
    (l0_ssm_s,), (l1_conv_s, l1_rnn_s), (l2_kv_s, l2_win_s), (l3_ssm_s,) = new_s
    return (hp, hs, l0_ssm_p, l0_ssm_s, l1_conv_p, l1_rnn_p, l1_conv_s, l1_rnn_s,
            l2_kv_p, l2_win_p, l2_kv_s, l2_win_s, l3_ssm_p, l3_ssm_s)
```

```python
import functools
import math

import numpy as np
import jax
import jax.numpy as jnp
from jax import lax
from jax.experimental import pallas as pl
from jax.experimental.pallas import tpu as pltpu

F32 = jnp.float32
BF16 = jnp.bfloat16

EPS = 1e-6
NEG = -1e30
TINY = 1e-30

LANES = 128
SUBLANES = 8

S5_GROUP = 16
S5_STATE = 64
S5_TILE_GROUPS = SUBLANES * LANES // S5_STATE

RG_CONV = 4
RG_C = 8.0

N_HEADS = 32
HEAD_DIM = 128
N_KV = 4
Q_PER_KV = N_HEADS // N_KV
CMP_LEN = 32
CMP_STRIDE = 16
SLC_BLOCK = 64
SLC_TOP = 16
FORCE_SCORE = 1e3
WINDOW = 512

NSA_Q_COLS = N_HEADS * HEAD_DIM
COL_KV = NSA_Q_COLS // LANES
COL_WIN = COL_KV + 4 * N_KV
COL_Z = COL_WIN + 2 * N_KV
COL_GATE = COL_Z + N_HEADS
GROUP_COLS = Q_PER_KV * HEAD_DIM
KV_HALF_COLS = 2 * N_KV * HEAD_DIM
PAGES_PER_STEP = 8

V7X_VMEM_LIMIT_BYTES = 48 * 1024 * 1024


def _pick_tile(n, target, align):
    if n <= target:
        return n
    t = (target // align) * align
    while t >= align:
        if n % t == 0:
            return t
        t -= align
    return n


def _rmsnorm_kernel(x_ref, g_ref, o_ref):
    x = x_ref[...]
    ms = jnp.mean(x * x, axis=-1, keepdims=True)
    o_ref[...] = (x * lax.rsqrt(ms + EPS) * g_ref[...]).astype(o_ref.dtype)


def rmsnorm(x2d, g, out_dtype):
    m, d = x2d.shape
    tm = _pick_tile(m, 256, 8)
    return pl.pallas_call(
        _rmsnorm_kernel,
        grid=(m // tm,),
        in_specs=[pl.BlockSpec((tm, d), lambda i: (i, 0)),
                  pl.BlockSpec((1, d), lambda i: (0, 0))],
        out_specs=pl.BlockSpec((tm, d), lambda i: (i, 0)),
        out_shape=jax.ShapeDtypeStruct((m, d), out_dtype),
        compiler_params=pltpu.CompilerParams(dimension_semantics=("parallel",),
                                             vmem_limit_bytes=V7X_VMEM_LIMIT_BYTES),
        name="rmsnorm",
    )(x2d, g.reshape(1, d))


def _post_norm_residual_kernel(h_ref, y_ref, g_ref, o_ref):
    y = y_ref[...]
    ms = jnp.mean(y * y, axis=-1, keepdims=True)
    o_ref[...] = h_ref[...] + y * lax.rsqrt(ms + EPS) * g_ref[...]


def post_norm_residual(h2d, y2d, g):
    m, d = h2d.shape
    tm = _pick_tile(m, 256, 8)
    return pl.pallas_call(
        _post_norm_residual_kernel,
        grid=(m // tm,),
        in_specs=[pl.BlockSpec((tm, d), lambda i: (i, 0)),
                  pl.BlockSpec((tm, d), lambda i: (i, 0)),
                  pl.BlockSpec((1, d), lambda i: (0, 0))],
        out_specs=pl.BlockSpec((tm, d), lambda i: (i, 0)),
        out_shape=jax.ShapeDtypeStruct((m, d), F32),
        compiler_params=pltpu.CompilerParams(dimension_semantics=("parallel",),
                                             vmem_limit_bytes=V7X_VMEM_LIMIT_BYTES),
        name="post_norm_residual",
    )(h2d, y2d, g.reshape(1, d))


def _matmul_kernel(a_ref, b_ref, o_ref, acc_ref):
    k = pl.program_id(2)

    @pl.when(k == 0)
    def _():
        acc_ref[...] = jnp.zeros_like(acc_ref)

    acc_ref[...] += jnp.dot(a_ref[...], b_ref[...], preferred_element_type=F32)

    @pl.when(k == pl.num_programs(2) - 1)
    def _():
        o_ref[...] = acc_ref[...].astype(o_ref.dtype)


def matmul(a, b, out_dtype=F32):
    m, k = a.shape
    _, n = b.shape
    tm = _pick_tile(m, 1024, 8)
    tn = _pick_tile(n, 512, 128)
    tk = _pick_tile(k, 4096, 128)
    return pl.pallas_call(
        _matmul_kernel,
        grid=(m // tm, n // tn, k // tk),
        in_specs=[pl.BlockSpec((tm, tk), lambda i, j, kk: (i, kk)),
                  pl.BlockSpec((tk, tn), lambda i, j, kk: (kk, j))],
        out_specs=pl.BlockSpec((tm, tn), lambda i, j, kk: (i, j)),
        out_shape=jax.ShapeDtypeStruct((m, n), out_dtype),
        scratch_shapes=[pltpu.VMEM((tm, tn), F32)],
        compiler_params=pltpu.CompilerParams(
            dimension_semantics=("parallel", "parallel", "arbitrary"),
            vmem_limit_bytes=V7X_VMEM_LIMIT_BYTES),
        name="matmul",
    )(a, b)


def proj(x, w_bf16):
    n, t, k = x.shape
    return matmul(x.reshape(n * t, k).astype(BF16), w_bf16).reshape(n, t, -1)


def s5_discretize(a_re, a_im, log_dt, b, c):
    dt = jnp.exp(log_dt.astype(F32))[:, None]
    ar, ai = a_re.astype(F32), a_im.astype(F32)
    mag = jnp.exp(ar * dt)
    abar_r, abar_i = mag * jnp.cos(ai * dt), mag * jnp.sin(ai * dt)
    den = ar * ar + ai * ai
    coef_r = ((abar_r - 1.0) * ar + abar_i * ai) / den
    coef_i = (abar_i * ar - (abar_r - 1.0) * ai) / den
    b_r, b_i = b[0].astype(F32), b[1].astype(F32)
    bbar_r = coef_r[..., None] * b_r - coef_i[..., None] * b_i
    bbar_i = coef_r[..., None] * b_i + coef_i[..., None] * b_r
    return abar_r, abar_i, bbar_r, bbar_i, c[0].astype(F32), c[1].astype(F32)


def s5_tile_weights(bbar_r, bbar_i, c_r, c_i):
    g = bbar_r.shape[0]
    tg = S5_TILE_GROUPS
    nt = g // tg
    eye = jnp.eye(tg, dtype=bool)[None, :, None, :, None]

    def expand(x):
        xt = x.transpose(0, 1, 3, 2)[:, :, :, None, :]
        full = jnp.where(eye, xt, 0.0)
        return full.reshape(nt, tg * x.shape[3], tg * x.shape[2])

    br = expand(bbar_r.reshape(nt, tg, S5_STATE, S5_GROUP))
    bi = expand(bbar_i.reshape(nt, tg, S5_STATE, S5_GROUP))
    cr = expand(c_r.reshape(nt, tg, S5_GROUP, S5_STATE))
    ci = expand(c_i.reshape(nt, tg, S5_GROUP, S5_STATE))
    bcat = jnp.concatenate([br, bi], axis=-1).astype(BF16)
    ccat = jnp.concatenate([cr, -ci], axis=1).astype(BF16)
    return bcat, ccat


def _s5_prompt_kernel(u_ref, bcat_ref, ccat_ref, ar_ref, ai_ref, d_ref, h0_ref,
                      g_ref, hlast_ref, *scr, tc, tp, tg):
    bur_scrs, bui_scrs, h_scr = scr[:tg], scr[tg:2 * tg], scr[2 * tg]
    c = pl.program_id(2)
    sw = S5_TILE_GROUPS * S5_STATE
    cw = S5_TILE_GROUPS * S5_GROUP

    @pl.when(c == 0)
    def _():
        h_scr[...] = h0_ref[0]

    u = u_ref[0]
    ub = u.astype(BF16)
    for k in range(tg):
        bu = jnp.dot(ub[:, k * cw:(k + 1) * cw], bcat_ref[k], preferred_element_type=F32)
        for s in range(SUBLANES):
            bur_scrs[k][pl.ds(s * tp, tc), :] = bu[:, s * LANES:(s + 1) * LANES]
            bui_scrs[k][pl.ds(s * tp, tc), :] = bu[:, sw + s * LANES:sw + (s + 1) * LANES]

    ar = [ar_ref[k] for k in range(tg)]
    ai = [ai_ref[k] for k in range(tg)]

    def step(t, carry):
        new = []
        for k in range(tg):
            hr, hi = carry[2 * k], carry[2 * k + 1]
            br = bur_scrs[k][pl.ds(t, SUBLANES, stride=tp), :]
            bi = bui_scrs[k][pl.ds(t, SUBLANES, stride=tp), :]
            hr2 = ar[k] * hr - ai[k] * hi + br
            hi2 = ar[k] * hi + ai[k] * hr + bi
            bur_scrs[k][pl.ds(t, SUBLANES, stride=tp), :] = hr2
            bui_scrs[k][pl.ds(t, SUBLANES, stride=tp), :] = hi2
            new += [hr2, hi2]
        return tuple(new)

    init = tuple(h_scr[p, k] for k in range(tg) for p in (0, 1))
    fin = lax.fori_loop(0, tc, step, init, unroll=4)
    for k in range(tg):
        h_scr[0, k] = fin[2 * k]
        h_scr[1, k] = fin[2 * k + 1]

    for k in range(tg):
        pieces = [bur_scrs[k][pl.ds(s * tp, tc), :].astype(BF16) for s in range(SUBLANES)]
        pieces += [bui_scrs[k][pl.ds(s * tp, tc), :].astype(BF16) for s in range(SUBLANES)]
        hcat = jnp.concatenate(pieces, axis=-1)
        y = jnp.dot(hcat, ccat_ref[k], preferred_element_type=F32)
        yk = y + d_ref[:, k * cw:(k + 1) * cw] * u[:, k * cw:(k + 1) * cw]
        g_ref[0, :, k * cw:(k + 1) * cw] = jax.nn.gelu(yk).astype(g_ref.dtype)

    @pl.when(c == pl.num_programs(2) - 1)
    def _():
        hlast_ref[0] = h_scr[...]


def s5_prompt(uz, h0, abar_r, abar_i, bcat, ccat, d, *, tc=512, tg=4):
    n, t, w2 = uz.shape
    w = w2 // 2
    g = w // S5_GROUP
    nt = g // S5_TILE_GROUPS
    tg = min(tg, nt)
    tc = min(tc, t)
    tp = tc + SUBLANES
    cw = S5_TILE_GROUPS * S5_GROUP
    sw = S5_TILE_GROUPS * S5_STATE
    h0t = h0.reshape(n, 2, nt, SUBLANES, LANES)
    art = abar_r.reshape(nt, SUBLANES, LANES)
    ait = abar_i.reshape(nt, SUBLANES, LANES)
    kern = functools.partial(_s5_prompt_kernel, tc=tc, tp=tp, tg=tg)
    gout, hlast = pl.pallas_call(
        kern,
        grid=(nt // tg, n, t // tc),
        in_specs=[
            pl.BlockSpec((1, tc, tg * cw), lambda j, i, c: (i, c, j)),
            pl.BlockSpec((tg, cw, 2 * sw), lambda j, i, c: (j, 0, 0)),
            pl.BlockSpec((tg, 2 * sw, cw), lambda j, i, c: (j, 0, 0)),
            pl.BlockSpec((tg, SUBLANES, LANES), lambda j, i, c: (j, 0, 0)),
            pl.BlockSpec((tg, SUBLANES, LANES), lambda j, i, c: (j, 0, 0)),
            pl.BlockSpec((1, tg * cw), lambda j, i, c: (0, j)),
            pl.BlockSpec((1, 2, tg, SUBLANES, LANES), lambda j, i, c: (i, 0, j, 0, 0)),
        ],
        out_specs=[
            pl.BlockSpec((1, tc, tg * cw), lambda j, i, c: (i, c, j)),
            pl.BlockSpec((1, 2, tg, SUBLANES, LANES), lambda j, i, c: (i, 0, j, 0, 0)),
        ],
        out_shape=[jax.ShapeDtypeStruct((n, t, w), BF16),
                   jax.ShapeDtypeStruct((n, 2, nt, SUBLANES, LANES), F32)],
        scratch_shapes=[pltpu.VMEM((SUBLANES * tp, LANES), F32)] * (2 * tg)
        + [pltpu.VMEM((2, tg, SUBLANES, LANES), F32)],
        compiler_params=pltpu.CompilerParams(
            dimension_semantics=("parallel", "parallel", "arbitrary"),
            vmem_limit_bytes=V7X_VMEM_LIMIT_BYTES),
        name="s5_prompt",
    )(uz, bcat, ccat, art, ait, d.reshape(1, w), h0t)
    return gout, hlast.reshape(n, 2, g, S5_STATE)


def _s5_step_kernel(u_ref, bcat_ref, ccat_ref, ar_ref, ai_ref, d_ref, h0_ref, g_ref, h_ref):
    sw = S5_TILE_GROUPS * S5_STATE
    u = u_ref[...]
    bu = jnp.dot(u.astype(BF16), bcat_ref[0], preferred_element_type=F32)
    hr, hi = h0_ref[:, 0, :], h0_ref[:, 1, :]
    ar, ai = ar_ref[0], ai_ref[0]
    hr2 = ar * hr - ai * hi + bu[:, :sw]
    hi2 = ar * hi + ai * hr + bu[:, sw:]
    h_ref[:, 0, :] = hr2
    h_ref[:, 1, :] = hi2
    hcat = jnp.concatenate([hr2.astype(BF16), hi2.astype(BF16)], axis=-1)
    y = jnp.dot(hcat, ccat_ref[0], preferred_element_type=F32)
    g_ref[...] = jax.nn.gelu(y + d_ref[...] * u).astype(g_ref.dtype)


def s5_step(uz, h0, abar_r, abar_i, bcat, ccat, d):
    n, w2 = uz.shape
    w = w2 // 2
    g = w // S5_GROUP
    nt = g // S5_TILE_GROUPS
    cw = S5_TILE_GROUPS * S5_GROUP
    sw = S5_TILE_GROUPS * S5_STATE
    gout, hnew = pl.pallas_call(
        _s5_step_kernel,
        grid=(nt,),
        in_specs=[
            pl.BlockSpec((n, cw), lambda j: (0, j)),
            pl.BlockSpec((1, cw, 2 * sw), lambda j: (j, 0, 0)),
            pl.BlockSpec((1, 2 * sw, cw), lambda j: (j, 0, 0)),
            pl.BlockSpec((1, 1, sw), lambda j: (j, 0, 0)),
            pl.BlockSpec((1, 1, sw), lambda j: (j, 0, 0)),
            pl.BlockSpec((1, cw), lambda j: (0, j)),
            pl.BlockSpec((n, 2, sw), lambda j: (0, 0, j)),
        ],
        out_specs=[pl.BlockSpec((n, cw), lambda j: (0, j)),
                   pl.BlockSpec((n, 2, sw), lambda j: (0, 0, j))],
        out_shape=[jax.ShapeDtypeStruct((n, w), BF16),
                   jax.ShapeDtypeStruct((n, 2, g * S5_STATE), F32)],
        compiler_params=pltpu.CompilerParams(dimension_semantics=("parallel",),
                                             vmem_limit_bytes=V7X_VMEM_LIMIT_BYTES),
        name="s5_step",
    )(uz, bcat, ccat, abar_r.reshape(nt, 1, sw), abar_i.reshape(nt, 1, sw), d.reshape(1, w),
      h0.reshape(n, 2, g * S5_STATE))
    return gout, hnew.reshape(n, 2, g, S5_STATE)


def s5_params(w_in, a_re, a_im, log_dt, b, c, d, w_glu, b_glu, w_out):
    abar_r, abar_i, bbar_r, bbar_i, c_r, c_i = s5_discretize(a_re, a_im, log_dt, b, c)
    bcat, ccat = s5_tile_weights(bbar_r, bbar_i, c_r, c_i)
    return dict(w_in=w_in.astype(BF16), abar_r=abar_r, abar_i=abar_i, bcat=bcat, ccat=ccat,
                d=d.astype(F32), w_glu=w_glu.astype(BF16), b_glu=b_glu, w_out=w_out.astype(BF16))


def _matmul_glu_kernel(a_ref, b_ref, bias_ref, g_ref, z_ref, o_ref, acc_ref):
    k = pl.program_id(2)

    @pl.when(k == 0)
    def _():
        acc_ref[...] = jnp.zeros_like(acc_ref)

    acc_ref[...] += jnp.dot(a_ref[...], b_ref[...], preferred_element_type=F32)

    @pl.when(k == pl.num_programs(2) - 1)
    def _():
        gate = jax.nn.sigmoid(acc_ref[...] + bias_ref[...])
        o_ref[...] = (g_ref[...].astype(F32) * gate * jax.nn.silu(z_ref[...])).astype(o_ref.dtype)


def matmul_glu(g, w, bias, uz):
    m, k = g.shape
    _, n = w.shape
    tm = _pick_tile(m, 1024, 8)
    tn = _pick_tile(n, 512, LANES)
    tk = _pick_tile(k, 4096, LANES)
    z_off = n // tn
    return pl.pallas_call(
        _matmul_glu_kernel,
        grid=(m // tm, n // tn, k // tk),
        in_specs=[pl.BlockSpec((tm, tk), lambda i, j, kk: (i, kk)),
                  pl.BlockSpec((tk, tn), lambda i, j, kk: (kk, j)),
                  pl.BlockSpec((1, tn), lambda i, j, kk: (0, j)),
                  pl.BlockSpec((tm, tn), lambda i, j, kk: (i, j)),
                  pl.BlockSpec((tm, tn), lambda i, j, kk: (i, z_off + j))],
        out_specs=pl.BlockSpec((tm, tn), lambda i, j, kk: (i, j)),
        out_shape=jax.ShapeDtypeStruct((m, n), BF16),
        scratch_shapes=[pltpu.VMEM((tm, tn), F32)],
        compiler_params=pltpu.CompilerParams(
            dimension_semantics=("parallel", "parallel", "arbitrary"),
            vmem_limit_bytes=V7X_VMEM_LIMIT_BYTES),
        name="matmul_glu",
    )(g, w, bias.astype(F32).reshape(1, n), g, uz)


def s5_layer(x, h0, p):
    n, t, _ = x.shape
    uz = proj(x, p["w_in"])
    w = uz.shape[-1] // 2
    if t == 1:
        g, h = s5_step(uz[:, 0], h0, p["abar_r"], p["abar_i"], p["bcat"], p["ccat"], p["d"])
    else:
        g, h = s5_prompt(uz, h0, p["abar_r"], p["abar_i"], p["bcat"], p["ccat"], p["d"])
    gz = matmul_glu(g.reshape(n * t, w), p["w_glu"], p["b_glu"], uz.reshape(n * t, 2 * w))
    return matmul(gz, p["w_out"]).reshape(n, t, -1), (h,)


def _round_up(x, m):
    return -(-x // m) * m


def rg_pad_cols(x, nb, bp):
    lead = x.shape[:-1]
    blk = x.shape[-1] // nb
    x = x.reshape(lead + (nb, blk))
    x = jnp.pad(x, [(0, 0)] * len(lead) + [(0, 0), (0, bp - blk)])
    return x.reshape(lead + (nb * bp,))


def rg_unpad_cols(x, nb, blk):
    lead = x.shape[:-1]
    bp = x.shape[-1] // nb
    return x.reshape(lead + (nb, bp))[..., :blk].reshape(lead + (nb * blk,))


def rg_params(w_in, conv_w, conv_b, w_rg, b_rg, w_ig, b_ig, lam, w_out):
    nb, blk, _ = w_rg.shape
    bp = _round_up(blk, LANES)
    width = nb * blk
    w_in_p = jnp.concatenate([rg_pad_cols(w_in[:, :width], nb, bp), rg_pad_cols(w_in[:, width:], nb, bp)],
                             axis=1).astype(BF16)
    pad_sq = lambda w: jnp.pad(w, ((0, 0), (0, bp - blk), (0, bp - blk)))
    w_gate = jnp.concatenate([pad_sq(w_rg), pad_sq(w_ig)], axis=-1).astype(BF16)
    b_gate = jnp.stack([rg_pad_cols(b_rg.astype(F32), nb, bp).reshape(nb, bp),
                        rg_pad_cols(b_ig.astype(F32), nb, bp).reshape(nb, bp)], axis=1)
    b_gate = b_gate.reshape(nb, 1, 2 * bp)
    sp = rg_pad_cols(jax.nn.softplus(-lam.astype(F32)), nb, bp).reshape(1, nb * bp)
    w_out_p = jnp.pad(w_out.reshape(nb, blk, -1), ((0, 0), (0, bp - blk), (0, 0))).reshape(nb * bp, -1)
    return dict(nb=nb, blk=blk, bp=bp, w_in=w_in_p, conv_w=rg_pad_cols(conv_w.astype(F32), nb, bp),
                conv_b=rg_pad_cols(conv_b.astype(F32), nb, bp).reshape(1, nb * bp),
                w_gate=w_gate, b_gate=b_gate, sp=sp, w_out=w_out_p.astype(BF16))


def _rg_gates(uc, wg, bg, sp, bp):
    gates = jnp.dot(uc.astype(BF16), wg, preferred_element_type=F32) + bg
    r = jax.nn.sigmoid(gates[:, :bp])
    ig = jax.nn.sigmoid(gates[:, bp:])
    log_a = -RG_C * r * sp
    a = jnp.exp(log_a)
    xin = jnp.sqrt(1.0 - a * a) * (ig * uc)
    return a, xin


def _rg_prompt_kernel(u_ref, z_ref, cw_ref, cb_ref, wg_ref, bg_ref, sp_ref, cbuf_ref, h0_ref,
                      o_ref, tail_ref, hlast_ref, ext_scr, h_scr, *slabs, tc, pitch):
    c = pl.program_id(2)
    bp = u_ref.shape[-1]
    nl = bp // LANES
    a_scrs, x_scrs = slabs[:nl], slabs[nl:]
    sub = tc // SUBLANES

    @pl.when(c == 0)
    def _():
        ext_scr[0:SUBLANES, :] = cbuf_ref[0]
        h_scr[...] = h0_ref[0]

    ext_scr[pl.ds(SUBLANES, tc), :] = u_ref[0]
    w = cw_ref[...]
    uc = (w[3:4] * ext_scr[pl.ds(SUBLANES, tc), :] + w[2:3] * ext_scr[pl.ds(SUBLANES - 1, tc), :]
          + w[1:2] * ext_scr[pl.ds(SUBLANES - 2, tc), :] + w[0:1] * ext_scr[pl.ds(SUBLANES - 3, tc), :]
          + cb_ref[...])
    ext_scr[0:SUBLANES, :] = ext_scr[pl.ds(tc, SUBLANES), :]

    a, xin = _rg_gates(uc, wg_ref[0], bg_ref[0], sp_ref[...], bp)
    for l in range(nl):
        for s in range(SUBLANES):
            a_scrs[l][pl.ds(s * pitch, sub), :] = a[s * sub:(s + 1) * sub, l * LANES:(l + 1) * LANES]
            x_scrs[l][pl.ds(s * pitch, sub), :] = xin[s * sub:(s + 1) * sub, l * LANES:(l + 1) * LANES]

    def step(j, carry):
        new = []
        for l in range(nl):
            p, s_ = carry[2 * l], carry[2 * l + 1]
            aj = a_scrs[l][pl.ds(j, SUBLANES, stride=pitch), :]
            xj = x_scrs[l][pl.ds(j, SUBLANES, stride=pitch), :]
            p = aj * p
            s_ = aj * s_ + xj
            a_scrs[l][pl.ds(j, SUBLANES, stride=pitch), :] = p
            x_scrs[l][pl.ds(j, SUBLANES, stride=pitch), :] = s_
            new += [p, s_]
        return tuple(new)

    init = tuple(jnp.ones((SUBLANES, LANES), F32) if i % 2 == 0 else jnp.zeros((SUBLANES, LANES), F32)
                 for i in range(2 * nl))
    fin = lax.fori_loop(0, sub, step, init)

    for l in range(nl):
        p_end, s_end = fin[2 * l], fin[2 * l + 1]
        h = h_scr[:, l * LANES:(l + 1) * LANES]
        for s in range(SUBLANES):
            rows = pl.ds(s * pitch, sub)
            hs = x_scrs[l][rows, :] + a_scrs[l][rows, :] * h
            zs = z_ref[0, s * sub:(s + 1) * sub, l * LANES:(l + 1) * LANES]
            o_ref[0, s * sub:(s + 1) * sub, l * LANES:(l + 1) * LANES] = (hs * jax.nn.silu(zs)).astype(o_ref.dtype)
            h = s_end[s:s + 1, :] + p_end[s:s + 1, :] * h
        h_scr[:, l * LANES:(l + 1) * LANES] = h

    @pl.when(c == pl.num_programs(2) - 1)
    def _():
        tail_ref[0] = ext_scr[0:SUBLANES, :]
        hlast_ref[0] = h_scr[...]


def rg_prompt(uz, conv_buf, h0, p, *, tc=512):
    n, t, _ = uz.shape
    nb, blk, bp = p["nb"], p["blk"], p["bp"]
    tc = min(tc, t)
    sub = tc // SUBLANES
    pitch = sub + SUBLANES
    nl = bp // LANES
    cbuf = jnp.pad(rg_pad_cols(conv_buf.astype(F32), nb, bp), ((0, 0), (SUBLANES - (RG_CONV - 1), 0), (0, 0)))
    h0p = rg_pad_cols(h0.astype(F32), nb, bp).reshape(n, 1, nb * bp)
    kern = functools.partial(_rg_prompt_kernel, tc=tc, pitch=pitch)
    o, tail, hlast = pl.pallas_call(
        kern,
        grid=(n, nb, t // tc),
        in_specs=[
            pl.BlockSpec((1, tc, bp), lambda i, b, c: (i, c, b)),
            pl.BlockSpec((1, tc, bp), lambda i, b, c: (i, c, nb + b)),
            pl.BlockSpec((RG_CONV, bp), lambda i, b, c: (0, b)),
            pl.BlockSpec((1, bp), lambda i, b, c: (0, b)),
            pl.BlockSpec((1, bp, 2 * bp), lambda i, b, c: (b, 0, 0)),
            pl.BlockSpec((1, 1, 2 * bp), lambda i, b, c: (b, 0, 0)),
            pl.BlockSpec((1, bp), lambda i, b, c: (0, b)),
            pl.BlockSpec((1, SUBLANES, bp), lambda i, b, c: (i, 0, b)),
            pl.BlockSpec((1, 1, bp), lambda i, b, c: (i, 0, b)),
        ],
        out_specs=[
            pl.BlockSpec((1, tc, bp), lambda i, b, c: (i, c, b)),
            pl.BlockSpec((1, SUBLANES, bp), lambda i, b, c: (i, 0, b)),
            pl.BlockSpec((1, 1, bp), lambda i, b, c: (i, 0, b)),
        ],
        out_shape=[jax.ShapeDtypeStruct((n, t, nb * bp), BF16),
                   jax.ShapeDtypeStruct((n, SUBLANES, nb * bp), F32),
                   jax.ShapeDtypeStruct((n, 1, nb * bp), F32)],
        scratch_shapes=[pltpu.VMEM((tc + SUBLANES, bp), F32), pltpu.VMEM((1, bp), F32)]
        + [pltpu.VMEM((SUBLANES * pitch, LANES), F32)] * (2 * nl),
        compiler_params=pltpu.CompilerParams(
            dimension_semantics=("parallel", "parallel", "arbitrary"),
            vmem_limit_bytes=V7X_VMEM_LIMIT_BYTES),
        name="rg_prompt",
    )(uz, uz, p["conv_w"], p["conv_b"], p["w_gate"], p["b_gate"], p["sp"], cbuf, h0p)
    conv_state = rg_unpad_cols(tail[:, SUBLANES - (RG_CONV - 1):], nb, blk)
    return o, conv_state, rg_unpad_cols(hlast[:, 0], nb, blk)


def _rg_step_kernel(u_ref, z_ref, cw_ref, cb_ref, wg_ref, bg_ref, sp_ref, cbuf_ref, h0_ref, o_ref, h_ref):
    bp = u_ref.shape[-1]
    w = cw_ref[...]
    uc = (w[3:4] * u_ref[...] + w[2:3] * cbuf_ref[2] + w[1:2] * cbuf_ref[1] + w[0:1] * cbuf_ref[0]
          + cb_ref[...])
    a, xin = _rg_gates(uc, wg_ref[0], bg_ref[0], sp_ref[...], bp)
    h = a * h0_ref[...] + xin
    h_ref[...] = h
    o_ref[...] = (h * jax.nn.silu(z_ref[...])).astype(o_ref.dtype)


def rg_step(uz, conv_buf, h0, p):
    n, _ = uz.shape
    nb, blk, bp = p["nb"], p["blk"], p["bp"]
    cbuf = rg_pad_cols(conv_buf.astype(F32), nb, bp).transpose(1, 0, 2)
    h0p = rg_pad_cols(h0.astype(F32), nb, bp)
    o, hnew = pl.pallas_call(
        _rg_step_kernel,
        grid=(nb,),
        in_specs=[
            pl.BlockSpec((n, bp), lambda b: (0, b)),
            pl.BlockSpec((n, bp), lambda b: (0, nb + b)),
            pl.BlockSpec((RG_CONV, bp), lambda b: (0, b)),
            pl.BlockSpec((1, bp), lambda b: (0, b)),
            pl.BlockSpec((1, bp, 2 * bp), lambda b: (b, 0, 0)),
            pl.BlockSpec((1, 1, 2 * bp), lambda b: (b, 0, 0)),
            pl.BlockSpec((1, bp), lambda b: (0, b)),
            pl.BlockSpec((RG_CONV - 1, n, bp), lambda b: (0, 0, b)),
            pl.BlockSpec((n, bp), lambda b: (0, b)),
        ],
        out_specs=[pl.BlockSpec((n, bp), lambda b: (0, b)), pl.BlockSpec((n, bp), lambda b: (0, b))],
        out_shape=[jax.ShapeDtypeStruct((n, nb * bp), BF16), jax.ShapeDtypeStruct((n, nb * bp), F32)],
        compiler_params=pltpu.CompilerParams(dimension_semantics=("parallel",),
                                             vmem_limit_bytes=V7X_VMEM_LIMIT_BYTES),
        name="rg_step",
    )(uz, uz, p["conv_w"], p["conv_b"], p["w_gate"], p["b_gate"], p["sp"], cbuf, h0p)
    u_new = rg_unpad_cols(uz[:, :nb * bp], nb, blk)
    conv_state = jnp.concatenate([conv_buf[:, 1:].astype(F32), u_new[:, None]], axis=1)
    return o, conv_state, rg_unpad_cols(hnew, nb, blk)


def rglru_layer(x, conv_buf, h0, p):
    n, t, _ = x.shape
    uz = proj(x, p["w_in"])
    if t == 1:
        o, conv_state, h = rg_step(uz[:, 0], conv_buf, h0, p)
    else:
        o, conv_state, h = rg_prompt(uz, conv_buf, h0, p)
    y = matmul(o.reshape(n * t, -1), p["w_out"]).reshape(n, t, -1)
    return y, (conv_state, h)


def _masked_softmax_rows(s, ok):
    s = jnp.where(ok, s, NEG)
    mx = jnp.max(s, axis=-1, keepdims=True)
    p = jnp.where(ok, jnp.exp(s - mx), 0.0)
    den = jnp.maximum(jnp.sum(p, axis=-1, keepdims=True), TINY)
    return p / den


def _compress_kernel(x_ref, pe_ref, w1_ref, b1_ref, w2_ref, b2_ref, o_ref, *, nh):
    half = CMP_LEN // 2
    acc_lo = jnp.zeros((nh, w1_ref.shape[-1]), F32)
    acc_hi = jnp.zeros((nh, w1_ref.shape[-1]), F32)
    for l in range(half):
        x = x_ref[pl.ds(l, nh, stride=CMP_STRIDE), :]
        acc_lo += jnp.dot((x + pe_ref[0, l:l + 1, :]).astype(BF16), w1_ref[0, l],
                          preferred_element_type=F32)
        acc_hi += jnp.dot((x + pe_ref[0, half + l:half + l + 1, :]).astype(BF16), w1_ref[0, half + l],
                          preferred_element_type=F32)
    hid = jax.nn.gelu(acc_lo + pltpu.roll(acc_hi, nh - 1, axis=0) + b1_ref[0])
    out = jnp.dot(hid.astype(BF16), w2_ref[0], preferred_element_type=F32) + b2_ref[0]
    row = lax.broadcasted_iota(jnp.int32, out.shape, 0)
    o_ref[0, 0, 0] = jnp.where(row < nh - 1, out, 0.0)


def nsa_compress_prompt(pj, cmp_pe, cmp_w1, cmp_b1, cmp_w2, cmp_b2):
    n, t, _ = pj.shape
    nh = t // CMP_STRIDE
    hidden = cmp_w1.shape[-1]
    kern = functools.partial(_compress_kernel, nh=nh)
    return pl.pallas_call(
        kern,
        grid=(n, 2, N_KV),
        in_specs=[
            pl.BlockSpec((None, t, LANES), lambda i, ty, h: (i, 0, COL_KV + ty * N_KV + h)),
            pl.BlockSpec((1, CMP_LEN, HEAD_DIM), lambda i, ty, h: (ty, 0, 0)),
            pl.BlockSpec((1, CMP_LEN, HEAD_DIM, hidden), lambda i, ty, h: (ty, 0, 0, 0)),
            pl.BlockSpec((1, 1, hidden), lambda i, ty, h: (ty, 0, 0)),
            pl.BlockSpec((1, hidden, HEAD_DIM), lambda i, ty, h: (ty, 0, 0)),
            pl.BlockSpec((1, 1, HEAD_DIM), lambda i, ty, h: (ty, 0, 0)),
        ],
        out_specs=pl.BlockSpec((1, 1, 1, nh, HEAD_DIM), lambda i, ty, h: (i, ty, h, 0, 0)),
        out_shape=jax.ShapeDtypeStruct((n, 2, N_KV, nh, HEAD_DIM), F32),
        compiler_params=pltpu.CompilerParams(
            dimension_semantics=("parallel", "parallel", "parallel"),
            vmem_limit_bytes=V7X_VMEM_LIMIT_BYTES),
        name="nsa_compress_prompt",
    )(pj, cmp_pe, cmp_w1.astype(BF16), cmp_b1.reshape(2, 1, hidden), cmp_w2.astype(BF16),
      cmp_b2.reshape(2, 1, HEAD_DIM))


def _nsa_prompt_kernel(q_ref, kc_ref, vc_ref, ks_ref, vs_ref, kw_ref, vw_ref, z_ref, gl_ref, bg_ref,
                       ovl_ref, exp_ref, o_ref, m_scr, l_scr, acc_scr, *, bq, tk, t, n_sel, n_top, wlen):
    qb = pl.program_id(2)
    q = q_ref[0] * (HEAD_DIM ** -0.5)
    qs = jnp.concatenate([q[:, g * HEAD_DIM:(g + 1) * HEAD_DIM] for g in range(Q_PER_KV)],
                         axis=0).astype(BF16)
    nt_dims = (((1,), (1,)), ((), ()))

    def per_head(x):
        return jnp.concatenate([x] * Q_PER_KV, axis=0)

    qpos = qb * bq + lax.broadcasted_iota(jnp.int32, (bq, LANES), 0)
    lane = lax.broadcasted_iota(jnp.int32, (bq, LANES), 1)

    ncp = kc_ref.shape[3]
    s = lax.dot_general(qs, kc_ref[0, 0, 0].astype(BF16), nt_dims, preferred_element_type=F32)
    cpos = lax.broadcasted_iota(jnp.int32, (bq, ncp), 1) * CMP_STRIDE + (CMP_LEN - 1)
    qpos_c = qb * bq + lax.broadcasted_iota(jnp.int32, (bq, ncp), 0)
    p_cmp = _masked_softmax_rows(s, per_head(cpos <= qpos_c))
    o_cmp = jnp.dot(p_cmp.astype(BF16), vc_ref[0, 0, 0].astype(BF16), preferred_element_type=F32)

    psum = p_cmp[0:bq]
    for g in range(1, Q_PER_KV):
        psum = psum + p_cmp[g * bq:(g + 1) * bq]
    p_hi = psum.astype(BF16)
    p_lo = (psum - p_hi.astype(F32)).astype(BF16)
    imp = (jnp.dot(p_hi, ovl_ref[...], preferred_element_type=F32)
           + jnp.dot(p_lo, ovl_ref[...], preferred_element_type=F32))
    cur = qpos // SLC_BLOCK
    forced = (lane == 0) | (lane == cur) | (lane == cur - 1)
    imp = jnp.where(forced, FORCE_SCORE, jnp.where(lane > cur, -1.0, imp))
    imp = jnp.where(lane < n_sel, imp, -2.0)
    rank = jnp.zeros((bq, LANES), F32)
    for b in range(n_sel):
        col = jnp.broadcast_to(imp[:, b:b + 1], (bq, LANES))
        tie = jnp.where(lane > b, 1.0, 0.0)
        rank = rank + jnp.where(col > imp, 1.0, jnp.where(col == imp, tie, 0.0))
    sel = jnp.where(rank < n_top, 1.0, 0.0).astype(BF16)

    m_scr[...] = jnp.full(m_scr.shape, NEG, F32)
    l_scr[...] = jnp.zeros(l_scr.shape, F32)
    acc_scr[...] = jnp.zeros(acc_scr.shape, F32)
    reps = tk // LANES
    qpos_k = qb * bq + lax.broadcasted_iota(jnp.int32, (bq, tk), 0)
    koff = lax.broadcasted_iota(jnp.int32, (bq, tk), 1)

    def body(kt, carry):
        start = pl.multiple_of(kt * tk, tk)
        k = ks_ref[0, pl.ds(start, tk), :].astype(BF16)
        v = vs_ref[0, pl.ds(start, tk), :].astype(BF16)
        s = lax.dot_general(qs, k, nt_dims, preferred_element_type=F32)
        selk = jnp.dot(sel, exp_ref[kt], preferred_element_type=F32)
        ok = per_head((selk > 0.5) & (kt * tk + koff <= qpos_k))
        s = jnp.where(ok, s, NEG)
        m_old = m_scr[...]
        m_new = jnp.maximum(m_old, jnp.max(s, axis=-1, keepdims=True))
        alpha = jnp.exp(m_old - m_new)
        p = jnp.where(ok, jnp.exp(s - jnp.concatenate([m_new] * reps, axis=1)), 0.0)
        l_scr[...] = alpha * l_scr[...] + jnp.sum(p, axis=-1, keepdims=True)
        acc_scr[...] = alpha * acc_scr[...] + jnp.dot(p.astype(BF16), v, preferred_element_type=F32)
        m_scr[...] = m_new
        return carry

    n_tiles = ((qb + 1) * bq + tk - 1) // tk
    lax.fori_loop(0, n_tiles, body, 0)
    o_slc = acc_scr[...] / jnp.maximum(l_scr[...], TINY)

    wstart = pl.multiple_of(jnp.clip(qb * bq + bq - wlen, 0, t - wlen), bq)
    kw = kw_ref[0, pl.ds(wstart, wlen), :].astype(BF16)
    vw = vw_ref[0, pl.ds(wstart, wlen), :].astype(BF16)
    s = lax.dot_general(qs, kw, nt_dims, preferred_element_type=F32)
    diff = (qb * bq + lax.broadcasted_iota(jnp.int32, (bq, wlen), 0)
            - wstart - lax.broadcasted_iota(jnp.int32, (bq, wlen), 1))
    p_win = _masked_softmax_rows(s, per_head((diff >= 0) & (diff <= WINDOW)))
    o_win = jnp.dot(p_win.astype(BF16), vw, preferred_element_type=F32)

    gs = jax.nn.sigmoid(gl_ref[0] + bg_ref[0])
    outs = []
    for g in range(Q_PER_KV):
        r0, r1 = g * bq, (g + 1) * bq
        g_cmp = jnp.broadcast_to(gs[:, 3 * g:3 * g + 1], (bq, HEAD_DIM))
        g_slc = jnp.broadcast_to(gs[:, 3 * g + 1:3 * g + 2], (bq, HEAD_DIM))
        g_win = jnp.broadcast_to(gs[:, 3 * g + 2:3 * g + 3], (bq, HEAD_DIM))
        outs.append(g_cmp * o_cmp[r0:r1] + g_slc * o_slc[r0:r1] + g_win * o_win[r0:r1])
    o = jnp.concatenate(outs, axis=1)
    o_ref[0] = (o * jax.nn.silu(z_ref[0])).astype(o_ref.dtype)


def nsa_prompt_attention(pj, kvc, b_gate, *, bq=128, tk=512):
    n, t, _ = pj.shape
    tk = min(tk, t)
    ncp = kvc.shape[3]
    n_cmp = ncp - 1
    n_sel = -(-t // SLC_BLOCK)
    n_top = min(SLC_TOP, n_sel)
    wlen = min(t, WINDOW + bq)
    c = np.arange(ncp)[:, None]
    sblk = np.arange(LANES)[None, :]
    ovl = ((c * CMP_STRIDE < sblk * SLC_BLOCK + SLC_BLOCK) & (c * CMP_STRIDE + CMP_LEN > sblk * SLC_BLOCK)
           & (sblk < n_sel) & (c < n_cmp))
    ovl = jnp.asarray(ovl, BF16)
    kk = np.arange(t).reshape(t // tk, 1, tk)
    expand = jnp.asarray(kk // SLC_BLOCK == np.arange(LANES)[None, :, None], BF16)
    bg = jnp.zeros((N_KV, 1, LANES), F32).at[:, 0, :3 * Q_PER_KV].set(
        b_gate.astype(F32).reshape(N_KV, 3 * Q_PER_KV))
    kern = functools.partial(_nsa_prompt_kernel, bq=bq, tk=tk, t=t, n_sel=n_sel, n_top=n_top, wlen=wlen)
    rows = Q_PER_KV * bq
    gcb = GROUP_COLS // LANES
    return pl.pallas_call(
        kern,
        grid=(n, N_KV, t // bq),
        in_specs=[
            pl.BlockSpec((1, bq, GROUP_COLS), lambda i, h, qb: (i, qb, h)),
            pl.BlockSpec((1, 1, 1, ncp, HEAD_DIM), lambda i, h, qb: (i, 0, h, 0, 0)),
            pl.BlockSpec((1, 1, 1, ncp, HEAD_DIM), lambda i, h, qb: (i, 1, h, 0, 0)),
            pl.BlockSpec((1, t, LANES), lambda i, h, qb: (i, 0, COL_KV + 2 * N_KV + h)),
            pl.BlockSpec((1, t, LANES), lambda i, h, qb: (i, 0, COL_KV + 3 * N_KV + h)),
            pl.BlockSpec((1, t, LANES), lambda i, h, qb: (i, 0, COL_WIN + h)),
            pl.BlockSpec((1, t, LANES), lambda i, h, qb: (i, 0, COL_WIN + N_KV + h)),
            pl.BlockSpec((1, bq, GROUP_COLS), lambda i, h, qb: (i, qb, COL_Z // gcb + h)),
            pl.BlockSpec((1, bq, LANES), lambda i, h, qb: (i, qb, COL_GATE + h)),
            pl.BlockSpec((1, 1, LANES), lambda i, h, qb: (h, 0, 0)),
            pl.BlockSpec((ncp, LANES), lambda i, h, qb: (0, 0)),
            pl.BlockSpec((t // tk, LANES, tk), lambda i, h, qb: (0, 0, 0)),
        ],
        out_specs=pl.BlockSpec((1, bq, GROUP_COLS), lambda i, h, qb: (i, qb, h)),
        out_shape=jax.ShapeDtypeStruct((n, t, NSA_Q_COLS), BF16),
        scratch_shapes=[pltpu.VMEM((rows, LANES), F32), pltpu.VMEM((rows, LANES), F32),
                        pltpu.VMEM((rows, HEAD_DIM), F32)],
        compiler_params=pltpu.CompilerParams(
            dimension_semantics=("parallel", "parallel", "arbitrary"),
            vmem_limit_bytes=V7X_VMEM_LIMIT_BYTES),
        name="nsa_prompt_attention",
    )(pj, kvc, kvc, pj, pj, pj, pj, pj, pj, bg, ovl, expand)


def _compress_paged_kernel(pt_ref, *refs, nh, page):
    g_pages = PAGES_PER_STEP
    page_refs = refs[:g_pages]
    pe_ref, w1_ref, b1_ref, w2_ref, b2_ref, o_ref, h_scr = refs[g_pages:]
    g = pl.program_id(1)
    half = CMP_LEN // 2
    hpp = page // CMP_STRIDE
    hidden = w1_ref.shape[-1]
    r = lax.broadcasted_iota(jnp.int32, (page, page), 0)
    c = lax.broadcasted_iota(jnp.int32, (page, page), 1)
    perm = jnp.where(c == (r % hpp) * CMP_STRIDE + r // hpp, 1.0, 0.0).astype(BF16)
    xs = []
    for i in range(g_pages):
        x = page_refs[i][...]
        x_hi = x.astype(BF16)
        x_lo = (x - x_hi.astype(F32)).astype(BF16)
        xs.append(jnp.dot(perm, x_hi, preferred_element_type=F32) + jnp.dot(perm, x_lo, preferred_element_type=F32))
    rows = hpp * g_pages
    for ty in range(2):
        acc_lo = jnp.zeros((N_KV * rows, hidden), F32)
        acc_hi = jnp.zeros((N_KV * rows, hidden), F32)
        for l in range(half):
            pieces = [xs[i][l * hpp:(l + 1) * hpp, (ty * N_KV + h) * HEAD_DIM:(ty * N_KV + h + 1) * HEAD_DIM]
                      for h in range(N_KV) for i in range(g_pages)]
            a = jnp.concatenate(pieces, axis=0)
            acc_lo += jnp.dot((a + pe_ref[ty, l:l + 1, :]).astype(BF16), w1_ref[ty, l],
                              preferred_element_type=F32)
            acc_hi += jnp.dot((a + pe_ref[ty, half + l:half + l + 1, :]).astype(BF16), w1_ref[ty, half + l],
                              preferred_element_type=F32)
        for h in range(N_KV):
            dst = pl.ds(pl.multiple_of(g * rows, rows), rows)
            h_scr[ty * N_KV + h, 0, dst, :] = acc_lo[h * rows:(h + 1) * rows]
            h_scr[ty * N_KV + h, 1, dst, :] = acc_hi[h * rows:(h + 1) * rows]

    @pl.when(g == pl.num_programs(1) - 1)
    def _():
        for ty in range(2):
            for h in range(N_KV):
                lo = h_scr[ty * N_KV + h, 0]
                hi = h_scr[ty * N_KV + h, 1]
                hid = jax.nn.gelu(lo + pltpu.roll(hi, nh - 1, axis=0) + b1_ref[ty])
                out = jnp.dot(hid.astype(BF16), w2_ref[ty], preferred_element_type=F32) + b2_ref[ty]
                row = lax.broadcasted_iota(jnp.int32, out.shape, 0)
                o_ref[0, ty, h] = jnp.where(row < nh - 1, out, 0.0)


def nsa_compress_paged(cache, page_table, cmp_pe, cmp_w1, cmp_b1, cmp_w2, cmp_b2):
    n, npages = page_table.shape
    _, page, _ = cache.shape
    nh = npages * page // CMP_STRIDE
    hidden = cmp_w1.shape[-1]
    gp = PAGES_PER_STEP
    kern = functools.partial(_compress_paged_kernel, nh=nh, page=page)
    page_specs = [pl.BlockSpec((None, page, KV_HALF_COLS),
                               lambda i, g, pt, j=j: (pt[i, g * gp + j], 0, 0)) for j in range(gp)]
    grid_spec = pltpu.PrefetchScalarGridSpec(
        num_scalar_prefetch=1,
        grid=(n, npages // gp),
        in_specs=page_specs + [
            pl.BlockSpec((2, CMP_LEN, HEAD_DIM), lambda i, g, pt: (0, 0, 0)),
            pl.BlockSpec((2, CMP_LEN, HEAD_DIM, hidden), lambda i, g, pt: (0, 0, 0, 0)),
            pl.BlockSpec((2, 1, hidden), lambda i, g, pt: (0, 0, 0)),
            pl.BlockSpec((2, hidden, HEAD_DIM), lambda i, g, pt: (0, 0, 0)),
            pl.BlockSpec((2, 1, HEAD_DIM), lambda i, g, pt: (0, 0, 0)),
        ],
        out_specs=pl.BlockSpec((1, 2, N_KV, nh, HEAD_DIM), lambda i, g, pt: (i, 0, 0, 0, 0)),
        scratch_shapes=[pltpu.VMEM((2 * N_KV, 2, nh, hidden), F32)],
    )
    return pl.pallas_call(
        kern,
        grid_spec=grid_spec,
        out_shape=jax.ShapeDtypeStruct((n, 2, N_KV, nh, HEAD_DIM), F32),
        compiler_params=pltpu.CompilerParams(dimension_semantics=("parallel", "arbitrary"),
                                             vmem_limit_bytes=V7X_VMEM_LIMIT_BYTES),
        name="nsa_compress_paged",
    )(page_table, *([cache] * gp), cmp_pe, cmp_w1.astype(BF16), cmp_b1.reshape(2, 1, hidden),
      cmp_w2.astype(BF16), cmp_b2.reshape(2, 1, HEAD_DIM))


def _decode_select_kernel(q_ref, kc_ref, vc_ref, ovl_ref, ocmp_ref, sel_ref, *, past, n_sel, n_top):
    nt_dims = (((1,), (1,)), ((), ()))
    qs = (q_ref[0] * (HEAD_DIM ** -0.5)).astype(BF16)
    nh = kc_ref.shape[3]
    s = lax.dot_general(qs, kc_ref[0, 0, 0].astype(BF16), nt_dims, preferred_element_type=F32)
    cpos = lax.broadcasted_iota(jnp.int32, (Q_PER_KV, nh), 1) * CMP_STRIDE + (CMP_LEN - 1)
    p_cmp = _masked_softmax_rows(s, cpos <= past)
    ocmp_ref[0, 0] = jnp.dot(p_cmp.astype(BF16), vc_ref[0, 0, 0].astype(BF16), preferred_element_type=F32)
    psum = jnp.sum(p_cmp, axis=0, keepdims=True)
    p_hi = psum.astype(BF16)
    p_lo = (psum - p_hi.astype(F32)).astype(BF16)
    imp = (jnp.dot(p_hi, ovl_ref[...], preferred_element_type=F32)
           + jnp.dot(p_lo, ovl_ref[...], preferred_element_type=F32))
    nsp = imp.shape[1]
    lane = lax.broadcasted_iota(jnp.int32, (1, nsp), 1)
    cur = past // SLC_BLOCK
    forced = (lane == 0) | (lane == cur) | (lane == cur - 1)
    imp = jnp.where(forced, FORCE_SCORE, jnp.where(lane > cur, -1.0, imp))
    imp = jnp.where(lane < n_sel, imp, -2.0)
    rank = jnp.zeros((1, nsp), F32)
    for b in range(n_sel):
        col = jnp.broadcast_to(imp[:, b:b + 1], (1, nsp))
        tie = jnp.where(lane > b, 1.0, 0.0)
        rank = rank + jnp.where(col > imp, 1.0, jnp.where(col == imp, tie, 0.0))
    sel_ref[0, 0] = jnp.where(rank < n_top, 1.0, 0.0)


def nsa_decode_select(q, kvc, past):
    n = q.shape[0]
    nh = kvc.shape[3]
    n_cmp = nh - 1
    n_sel = past // SLC_BLOCK + 1
    n_top = min(SLC_TOP, n_sel)
    nsp = _round_up(n_sel, LANES)
    c = np.arange(nh)[:, None]
    sblk = np.arange(nsp)[None, :]
    ovl = ((c * CMP_STRIDE < sblk * SLC_BLOCK + SLC_BLOCK) & (c * CMP_STRIDE + CMP_LEN > sblk * SLC_BLOCK)
           & (sblk < n_sel) & (c < n_cmp))
    ovl = jnp.asarray(ovl, BF16)
    kern = functools.partial(_decode_select_kernel, past=past, n_sel=n_sel, n_top=n_top)
    return pl.pallas_call(
        kern,
        grid=(n, N_KV),
        in_specs=[
            pl.BlockSpec((1, Q_PER_KV, HEAD_DIM), lambda i, h: (i, h, 0)),
            pl.BlockSpec((1, 1, 1, nh, HEAD_DIM), lambda i, h: (i, 0, h, 0, 0)),
            pl.BlockSpec((1, 1, 1, nh, HEAD_DIM), lambda i, h: (i, 1, h, 0, 0)),
            pl.BlockSpec((nh, nsp), lambda i, h: (0, 0)),
        ],
        out_specs=[pl.BlockSpec((1, 1, Q_PER_KV, HEAD_DIM), lambda i, h: (i, h, 0, 0)),
                   pl.BlockSpec((1, 1, 1, nsp), lambda i, h: (i, h, 0, 0))],
        out_shape=[jax.ShapeDtypeStruct((n, N_KV, Q_PER_KV, HEAD_DIM), F32),
                   jax.ShapeDtypeStruct((n, N_KV, 1, nsp), F32)],
        compiler_params=pltpu.CompilerParams(dimension_semantics=("parallel", "parallel"),
                                             vmem_limit_bytes=V7X_VMEM_LIMIT_BYTES),
        name="nsa_decode_select",
    )(q, kvc, kvc, ovl)


def _decode_slc_kernel(pt_ref, q_ref, sel_ref, page_ref, knew_ref, vnew_ref, o_ref, m_scr, l_scr, acc_scr, *, page):
    p = pl.program_id(1)
    nt_dims = (((1,), (1,)), ((), ()))
    nsp = sel_ref.shape[3]

    @pl.when(p == 0)
    def _():
        m_scr[...] = jnp.full(m_scr.shape, NEG, F32)
        l_scr[...] = jnp.zeros(l_scr.shape, F32)
        acc_scr[...] = jnp.zeros(acc_scr.shape, F32)

    blk = lax.broadcasted_iota(jnp.int32, (nsp, page), 0)
    key = lax.broadcasted_iota(jnp.int32, (nsp, page), 1)
    expand = jnp.where(blk == (p * page + key) // SLC_BLOCK, 1.0, 0.0).astype(BF16)
    x = page_ref[...]
    for h in range(N_KV):
        qs = (q_ref[0, h * Q_PER_KV:(h + 1) * Q_PER_KV, :] * (HEAD_DIM ** -0.5)).astype(BF16)
        k = x[:, h * HEAD_DIM:(h + 1) * HEAD_DIM].astype(BF16)
        v = x[:, (N_KV + h) * HEAD_DIM:(N_KV + h + 1) * HEAD_DIM].astype(BF16)
        s = lax.dot_general(qs, k, nt_dims, preferred_element_type=F32)
        sel = jnp.broadcast_to(sel_ref[0, h], (Q_PER_KV, nsp)).astype(BF16)
        ok = jnp.dot(sel, expand, preferred_element_type=F32) > 0.5
        s = jnp.where(ok, s, NEG)
        m_old = m_scr[h]
        m_new = jnp.maximum(m_old, jnp.max(s, axis=-1, keepdims=True))
        alpha = jnp.exp(m_old - m_new)
        pr = jnp.where(ok, jnp.exp(s - m_new), 0.0)
        l_scr[h] = alpha * l_scr[h] + jnp.sum(pr, axis=-1, keepdims=True)
        acc_scr[h] = alpha * acc_scr[h] + jnp.dot(pr.astype(BF16), v, preferred_element_type=F32)
        m_scr[h] = m_new

    @pl.when(p == pl.num_programs(1) - 1)
    def _():
        for h in range(N_KV):
            qf = q_ref[0, h * Q_PER_KV:(h + 1) * Q_PER_KV, :] * (HEAD_DIM ** -0.5)
            s_new = jnp.sum(qf * knew_ref[0, h:h + 1, :], axis=-1, keepdims=True)
            m_old = m_scr[h]
            m_new = jnp.maximum(m_old, s_new)
            alpha = jnp.exp(m_old - m_new)
            p_new = jnp.exp(s_new - m_new)
            den = alpha * l_scr[h] + p_new
            acc = alpha * acc_scr[h] + p_new * vnew_ref[0, h:h + 1, :]
            o_ref[0, h] = acc / jnp.maximum(den, TINY)


def nsa_decode_selected(q, sel, cache, page_table, k_new, v_new):
    n, npages = page_table.shape
    _, page, _ = cache.shape
    nsp = sel.shape[3]
    kern = functools.partial(_decode_slc_kernel, page=page)
    grid_spec = pltpu.PrefetchScalarGridSpec(
        num_scalar_prefetch=1,
        grid=(n, npages),
        in_specs=[
            pl.BlockSpec((1, N_HEADS, HEAD_DIM), lambda i, p, pt: (i, 0, 0)),
            pl.BlockSpec((1, N_KV, 1, nsp), lambda i, p, pt: (i, 0, 0, 0)),
            pl.BlockSpec((None, page, KV_HALF_COLS), lambda i, p, pt: (pt[i, p], 0, 1)),
            pl.BlockSpec((1, N_KV, HEAD_DIM), lambda i, p, pt: (i, 0, 0)),
            pl.BlockSpec((1, N_KV, HEAD_DIM), lambda i, p, pt: (i, 0, 0)),
        ],
        out_specs=pl.BlockSpec((1, N_KV, Q_PER_KV, HEAD_DIM), lambda i, p, pt: (i, 0, 0, 0)),
        scratch_shapes=[pltpu.VMEM((N_KV, Q_PER_KV, LANES), F32), pltpu.VMEM((N_KV, Q_PER_KV, LANES), F32),
                        pltpu.VMEM((N_KV, Q_PER_KV, HEAD_DIM), F32)],
    )
    return pl.pallas_call(
        kern,
        grid_spec=grid_spec,
        out_shape=jax.ShapeDtypeStruct((n, N_KV, Q_PER_KV, HEAD_DIM), F32),
        compiler_params=pltpu.CompilerParams(dimension_semantics=("parallel", "arbitrary"),
                                             vmem_limit_bytes=V7X_VMEM_LIMIT_BYTES),
        name="nsa_decode_selected",
    )(page_table, q, sel, cache, k_new, v_new)


def _decode_combine_kernel(q_ref, win_ref, kwn_ref, vwn_ref, ocmp_ref, oslc_ref, gl_ref, bg_ref, z_ref, o_ref):
    nt_dims = (((1,), (1,)), ((), ()))
    for h in range(N_KV):
        qf = q_ref[0, h * Q_PER_KV:(h + 1) * Q_PER_KV, :] * (HEAD_DIM ** -0.5)
        k = win_ref[0, :, h * HEAD_DIM:(h + 1) * HEAD_DIM].astype(BF16)
        v = win_ref[0, :, (N_KV + h) * HEAD_DIM:(N_KV + h + 1) * HEAD_DIM].astype(BF16)
        s = lax.dot_general(qf.astype(BF16), k, nt_dims, preferred_element_type=F32)
        s_new = jnp.sum(qf * kwn_ref[0, h:h + 1, :], axis=-1, keepdims=True)
        m = jnp.maximum(jnp.max(s, axis=-1, keepdims=True), s_new)
        pr = jnp.exp(s - m)
        p_new = jnp.exp(s_new - m)
        den = jnp.maximum(jnp.sum(pr, axis=-1, keepdims=True) + p_new, TINY)
        o_win = (jnp.dot(pr.astype(BF16), v, preferred_element_type=F32) + p_new * vwn_ref[0, h:h + 1, :]) / den
        g_cmp = jax.nn.sigmoid(gl_ref[0, h, 0] + bg_ref[h, 0])
        g_slc = jax.nn.sigmoid(gl_ref[0, h, 1] + bg_ref[h, 1])
        g_win = jax.nn.sigmoid(gl_ref[0, h, 2] + bg_ref[h, 2])
        o = g_cmp * ocmp_ref[0, h] + g_slc * oslc_ref[0, h] + g_win * o_win
        rows = slice(h * Q_PER_KV, (h + 1) * Q_PER_KV)
        o_ref[0, rows, :] = o * jax.nn.silu(z_ref[0, rows, :])


def nsa_decode_combine(q, win_cache, kw_new, vw_new, o_cmp, o_slc, gl, b_gate, z):
    n, lb, _ = win_cache.shape
    glt = gl.reshape(n, N_KV, Q_PER_KV, 3).transpose(0, 1, 3, 2)[..., None]
    bgt = b_gate.astype(F32).reshape(N_KV, Q_PER_KV, 3).transpose(0, 2, 1)[..., None]
    return pl.pallas_call(
        _decode_combine_kernel,
        grid=(n,),
        in_specs=[
            pl.BlockSpec((1, N_HEADS, HEAD_DIM), lambda i: (i, 0, 0)),
            pl.BlockSpec((1, lb, KV_HALF_COLS), lambda i: (i, 0, 0)),
            pl.BlockSpec((1, N_KV, HEAD_DIM), lambda i: (i, 0, 0)),
            pl.BlockSpec((1, N_KV, HEAD_DIM), lambda i: (i, 0, 0)),
            pl.BlockSpec((1, N_KV, Q_PER_KV, HEAD_DIM), lambda i: (i, 0, 0, 0)),
            pl.BlockSpec((1, N_KV, Q_PER_KV, HEAD_DIM), lambda i: (i, 0, 0, 0)),
            pl.BlockSpec((1, N_KV, 3, Q_PER_KV, 1), lambda i: (i, 0, 0, 0, 0)),
            pl.BlockSpec((N_KV, 3, Q_PER_KV, 1), lambda i: (0, 0, 0, 0)),
            pl.BlockSpec((1, N_HEADS, HEAD_DIM), lambda i: (i, 0, 0)),
        ],
        out_specs=pl.BlockSpec((1, N_HEADS, HEAD_DIM), lambda i: (i, 0, 0)),
        out_shape=jax.ShapeDtypeStruct((n, N_HEADS, HEAD_DIM), F32),
        compiler_params=pltpu.CompilerParams(dimension_semantics=("parallel",),
                                             vmem_limit_bytes=V7X_VMEM_LIMIT_BYTES),
        name="nsa_decode_combine",
    )(q, win_cache, kw_new, vw_new, o_cmp, o_slc, glt, bgt, z)


def nsa_decode(pj, cache_kv, page_table, cache_win, b_gate, cmp_pe, cmp_w1, cmp_b1, cmp_w2, cmp_b2):
    n = pj.shape[0]
    pool, page = cache_kv.shape[:2]
    past = page_table.shape[1] * page
    lb = cache_win.shape[1]
    assert lb <= WINDOW and lb <= past and page % SLC_BLOCK == 0
    cache = cache_kv.reshape(pool, page, 2 * KV_HALF_COLS)
    win = cache_win.reshape(n, lb, KV_HALF_COLS)
    q = pj[:, :NSA_Q_COLS].reshape(n, N_HEADS, HEAD_DIM)
    kv_new = pj[:, COL_KV * LANES:COL_WIN * LANES].reshape(n, 4, N_KV, HEAD_DIM)
    win_new = pj[:, COL_WIN * LANES:COL_Z * LANES].reshape(n, 2, N_KV, HEAD_DIM)
    z = pj[:, COL_Z * LANES:COL_GATE * LANES].reshape(n, N_HEADS, HEAD_DIM)
    gl = pj[:, COL_GATE * LANES:].reshape(n, N_KV, LANES)[..., :3 * Q_PER_KV]
    kvc = nsa_compress_paged(cache, page_table, cmp_pe, cmp_w1, cmp_b1, cmp_w2, cmp_b2)
    o_cmp, sel = nsa_decode_select(q, kvc, past)
    o_slc = nsa_decode_selected(q, sel, cache, page_table, kv_new[:, 2], kv_new[:, 3])
    o = nsa_decode_combine(q, win, win_new[:, 0], win_new[:, 1], o_cmp, o_slc, gl, b_gate, z)
    return o.reshape(n, NSA_Q_COLS)


def nsa_layer(x, kv_cache, win_past, page_table, w_in, b_gate, cmp_pe, cmp_w1, cmp_b1, cmp_w2, cmp_b2, w_out):
    n, t, _ = x.shape
    pj = proj(x, w_in)
    kv_new = pj[..., COL_KV * LANES:COL_WIN * LANES].reshape(n, t, 4, N_KV, HEAD_DIM)
    win_new = pj[..., COL_WIN * LANES:COL_Z * LANES].reshape(n, t, 2, N_KV, HEAD_DIM)
    if kv_cache is None:
        kvc = nsa_compress_prompt(pj, cmp_pe, cmp_w1, cmp_b1, cmp_w2, cmp_b2)
        o = nsa_prompt_attention(pj, kvc, b_gate)
        win_state = win_new[:, -min(WINDOW, t):]
    else:
        assert t == 1
        o = nsa_decode(pj[:, 0], kv_cache, page_table, win_past, b_gate, cmp_pe, cmp_w1, cmp_b1, cmp_w2, cmp_b2)
        o = o.astype(BF16)
        keys = jnp.concatenate([win_past.astype(win_new.dtype), win_new], axis=1)
        win_state = keys[:, -min(WINDOW, win_past.shape[1] + t):]
    y = matmul(o.reshape(n * t, NSA_Q_COLS), w_out).reshape(n, t, -1)
    return y, (kv_new, win_state)


def _nsa_w_in_layout(w_in):
    nsa_gate = 3 * N_HEADS
    a = COL_Z * LANES
    d = w_in.shape[0]
    w_gate = w_in[:, a:a + nsa_gate].reshape(d, N_KV, 3 * Q_PER_KV)
    w_gate = jnp.pad(w_gate, ((0, 0), (0, 0), (0, LANES - 3 * Q_PER_KV))).reshape(d, N_KV * LANES)
    return jnp.concatenate([w_in[:, :a], w_in[:, a + nsa_gate:], w_gate], axis=1)


def kernel(x_prompt, x_sample, state_l0_ssm, state_l1_conv, state_l1_rnn, cache_l2_kv, cache_l2_win, state_l3_ssm, page_table, l0_norm_pre, l0_norm_post, l0_w_in, l0_a_re, l0_a_im, l0_log_dt, l0_b, l0_c, l0_d, l0_w_glu, l0_b_glu, l0_w_out, l1_norm_pre, l1_norm_post, l1_w_in, l1_conv_w, l1_conv_b, l1_w_rg, l1_b_rg, l1_w_ig, l1_b_ig, l1_lam, l1_w_out, l2_norm_pre, l2_norm_post, l2_w_in, l2_b_gate, l2_cmp_pe, l2_cmp_w1, l2_cmp_b1, l2_cmp_w2, l2_cmp_b2, l2_w_out, l3_norm_pre, l3_norm_post, l3_w_in, l3_a_re, l3_a_im, l3_log_dt, l3_b, l3_c, l3_d, l3_w_glu, l3_b_glu, l3_w_out):
    bp = x_prompt.shape[0]
    dt = x_prompt.dtype
    d_model = x_prompt.shape[-1]
    s5_groups = l0_a_re.shape[0]
    rg_width = l1_lam.shape[0]
    bf = lambda w: w.astype(BF16)
    l0_p = s5_params(l0_w_in, l0_a_re, l0_a_im, l0_log_dt, l0_b, l0_c, l0_d, l0_w_glu, l0_b_glu, l0_w_out)
    l3_p = s5_params(l3_w_in, l3_a_re, l3_a_im, l3_log_dt, l3_b, l3_c, l3_d, l3_w_glu, l3_b_glu, l3_w_out)
    l1_p = rg_params(l1_w_in, l1_conv_w, l1_conv_b, l1_w_rg, l1_b_rg, l1_w_ig, l1_b_ig, l1_lam, l1_w_out)
    l2_wi, l2_wo = bf(_nsa_w_in_layout(l2_w_in)), bf(l2_w_out)
    layers = (
        (l0_norm_pre, l0_norm_post, lambda h, st: s5_layer(h, st[0], l0_p)),
        (l1_norm_pre, l1_norm_post, lambda h, st: rglru_layer(h, st[0], st[1], l1_p)),
        (l2_norm_pre, l2_norm_post,
         lambda h, st: nsa_layer(h, st[0], st[1], page_table, l2_wi, l2_b_gate, l2_cmp_pe, l2_cmp_w1, l2_cmp_b1,
                                 l2_cmp_w2, l2_cmp_b2, l2_wo)),
        (l3_norm_pre, l3_norm_post, lambda h, st: s5_layer(h, st[0], l3_p)),
    )
    prompt_state = (
        (jnp.zeros((bp, 2, s5_groups, S5_STATE), dt),),
        (jnp.zeros((bp, RG_CONV - 1, rg_width), dt), jnp.zeros((bp, rg_width), dt)),
        (None, None),
        (jnp.zeros((bp, 2, s5_groups, S5_STATE), dt),),
    )
    sample_state = (
        (state_l0_ssm,),
        (state_l1_conv, state_l1_rnn),
        (cache_l2_kv, cache_l2_win),
        (state_l3_ssm,),
    )
    hp, hs = x_prompt, x_sample
    new_p, new_s = [], []
    for i in range(4):
        g_pre, g_post, mix = layers[i]
        outs = []
        for h, st in ((hp, prompt_state[i]), (hs, sample_state[i])):
            n, t, _ = h.shape
            h2 = h.reshape(n * t, d_model)
            xn = rmsnorm(h2, g_pre, BF16).reshape(n, t, d_model)
            y, s_new = mix(xn, st)
            h_new = post_norm_residual(h2, y.reshape(n * t, d_model), g_post).reshape(n, t, d_model)
            outs.append((h_new, s_new))
        (hp, sp), (hs, ss) = outs
        new_p.append(sp)
        new_s.append(ss)
    (l0_ssm_p,), (l1_conv_p, l1_rnn_p), (l2_kv_p, l2_win_p), (l3_ssm_p,) = new_p
    (l0_ssm_s,), (l1_conv_s, l1_rnn_s), (l2_kv_s, l2_win_s), (l3_ssm_s,) = new_s
    return (hp, hs, l0_ssm_p, l0_ssm_s, l1_conv_p, l1_rnn_p, l1_conv_s, l1_rnn_s,
            l2_kv_p, l2_win_p, l2_kv_s, l2_win_s, l3_ssm_p, l3_ssm_s)
```

```python
import functools
import math

import numpy as np
import jax
import jax.numpy as jnp
from jax import lax
from jax.experimental import pallas as pl
from jax.experimental.pallas import tpu as pltpu

F32 = jnp.float32
BF16 = jnp.bfloat16

EPS = 1e-6
NEG = -1e30
TINY = 1e-30

LANES = 128
SUBLANES = 8

S5_GROUP = 16
S5_STATE = 64
S5_TILE_GROUPS = SUBLANES * LANES // S5_STATE

RG_CONV = 4
RG_C = 8.0

N_HEADS = 32
HEAD_DIM = 128
N_KV = 4
Q_PER_KV = N_HEADS // N_KV
CMP_LEN = 32
CMP_STRIDE = 16
SLC_BLOCK = 64
SLC_TOP = 16
FORCE_SCORE = 1e3
WINDOW = 512

NSA_Q_COLS = N_HEADS * HEAD_DIM
COL_KV = NSA_Q_COLS // LANES
COL_WIN = COL_KV + 4 * N_KV
COL_Z = COL_WIN + 2 * N_KV
COL_GATE = COL_Z + N_HEADS
GROUP_COLS = Q_PER_KV * HEAD_DIM
KV_HALF_COLS = 2 * N_KV * HEAD_DIM
PAGES_PER_STEP = 8

V7X_VMEM_LIMIT_BYTES = 48 * 1024 * 1024


def _pick_tile(n, target, align):
    if n <= target:
        return n
    t = (target // align) * align
    while t >= align:
        if n % t == 0:
            return t
        t -= align
    return n


def _rmsnorm_kernel(x_ref, g_ref, o_ref):
    x = x_ref[...]
    ms = jnp.mean(x * x, axis=-1, keepdims=True)
    o_ref[...] = (x * lax.rsqrt(ms + EPS) * g_ref[...]).astype(o_ref.dtype)


def rmsnorm(x2d, g, out_dtype):
    m, d = x2d.shape
    tm = _pick_tile(m, 256, 8)
    return pl.pallas_call(
        _rmsnorm_kernel,
        grid=(m // tm,),
        in_specs=[pl.BlockSpec((tm, d), lambda i: (i, 0)),
                  pl.BlockSpec((1, d), lambda i: (0, 0))],
        out_specs=pl.BlockSpec((tm, d), lambda i: (i, 0)),
        out_shape=jax.ShapeDtypeStruct((m, d), out_dtype),
        compiler_params=pltpu.CompilerParams(dimension_semantics=("parallel",),
                                             vmem_limit_bytes=V7X_VMEM_LIMIT_BYTES),
        name="rmsnorm",
    )(x2d, g.reshape(1, d))


def _post_norm_residual_kernel(h_ref, y_ref, g_ref, o_ref):
    y = y_ref[...]
    ms = jnp.mean(y * y, axis=-1, keepdims=True)
    o_ref[...] = h_ref[...] + y * lax.rsqrt(ms + EPS) * g_ref[...]


def _post_norm_residual_pre_kernel(h_ref, y_ref, g_ref, gn_ref, o_ref, xn_ref):
    y = y_ref[...]
    ms = jnp.mean(y * y, axis=-1, keepdims=True)
    h = h_ref[...] + y * lax.rsqrt(ms + EPS) * g_ref[...]
    o_ref[...] = h
    ms_h = jnp.mean(h * h, axis=-1, keepdims=True)
    xn_ref[...] = (h * lax.rsqrt(ms_h + EPS) * gn_ref[...]).astype(xn_ref.dtype)


def post_norm_residual(h2d, y2d, g, g_next=None):
    m, d = h2d.shape
    tm = _pick_tile(m, 256, 8)
    row = pl.BlockSpec((tm, d), lambda i: (i, 0))
    gain = pl.BlockSpec((1, d), lambda i: (0, 0))
    params = pltpu.CompilerParams(dimension_semantics=("parallel",), vmem_limit_bytes=V7X_VMEM_LIMIT_BYTES)
    if g_next is None:
        h_new = pl.pallas_call(
            _post_norm_residual_kernel,
            grid=(m // tm,),
            in_specs=[row, row, gain],
            out_specs=row,
            out_shape=jax.ShapeDtypeStruct((m, d), F32),
            compiler_params=params,
            name="post_norm_residual",
        )(h2d, y2d, g.reshape(1, d))
        return h_new, None
    return pl.pallas_call(
        _post_norm_residual_pre_kernel,
        grid=(m // tm,),
        in_specs=[row, row, gain, gain],
        out_specs=[row, row],
        out_shape=[jax.ShapeDtypeStruct((m, d), F32), jax.ShapeDtypeStruct((m, d), BF16)],
        compiler_params=params,
        name="post_norm_residual_pre",
    )(h2d, y2d, g.reshape(1, d), g_next.reshape(1, d))


def _matmul_single_k_kernel(a_ref, b_ref, o_ref):
    o_ref[...] = jnp.dot(a_ref[...], b_ref[...], preferred_element_type=F32).astype(o_ref.dtype)


def _matmul_kernel(a_ref, b_ref, o_ref, acc_ref):
    k = pl.program_id(2)

    @pl.when(k == 0)
    def _():
        acc_ref[...] = jnp.zeros_like(acc_ref)

    acc_ref[...] += jnp.dot(a_ref[...], b_ref[...], preferred_element_type=F32)

    @pl.when(k == pl.num_programs(2) - 1)
    def _():
        o_ref[...] = acc_ref[...].astype(o_ref.dtype)


def matmul(a, b, out_dtype=F32):
    m, k = a.shape
    _, n = b.shape
    tm = _pick_tile(m, 1024, 8)
    tn = _pick_tile(n, 512, 128)
    tk = _pick_tile(k, 4096, 128)
    if tk == k:
        return pl.pallas_call(
            _matmul_single_k_kernel,
            grid=(m // tm, n // tn),
            in_specs=[pl.BlockSpec((tm, k), lambda i, j: (i, 0)),
                      pl.BlockSpec((k, tn), lambda i, j: (0, j))],
            out_specs=pl.BlockSpec((tm, tn), lambda i, j: (i, j)),
            out_shape=jax.ShapeDtypeStruct((m, n), out_dtype),
            compiler_params=pltpu.CompilerParams(
                dimension_semantics=("parallel", "parallel"),
                vmem_limit_bytes=V7X_VMEM_LIMIT_BYTES),
            name="matmul",
        )(a, b)
    return pl.pallas_call(
        _matmul_kernel,
        grid=(m // tm, n // tn, k // tk),
        in_specs=[pl.BlockSpec((tm, tk), lambda i, j, kk: (i, kk)),
                  pl.BlockSpec((tk, tn), lambda i, j, kk: (kk, j))],
        out_specs=pl.BlockSpec((tm, tn), lambda i, j, kk: (i, j)),
        out_shape=jax.ShapeDtypeStruct((m, n), out_dtype),
        scratch_shapes=[pltpu.VMEM((tm, tn), F32)],
        compiler_params=pltpu.CompilerParams(
            dimension_semantics=("parallel", "parallel", "arbitrary"),
            vmem_limit_bytes=V7X_VMEM_LIMIT_BYTES),
        name="matmul",
    )(a, b)


def proj(x, w_bf16):
    n, t, k = x.shape
    return matmul(x.reshape(n * t, k).astype(BF16), w_bf16).reshape(n, t, -1)


def s5_discretize(a_re, a_im, log_dt, b, c):
    dt = jnp.exp(log_dt.astype(F32))[:, None]
    ar, ai = a_re.astype(F32), a_im.astype(F32)
    mag = jnp.exp(ar * dt)
    abar_r, abar_i = mag * jnp.cos(ai * dt), mag * jnp.sin(ai * dt)
    den = ar * ar + ai * ai
    coef_r = ((abar_r - 1.0) * ar + abar_i * ai) / den
    coef_i = (abar_i * ar - (abar_r - 1.0) * ai) / den
    b_r, b_i = b[0].astype(F32), b[1].astype(F32)
    bbar_r = coef_r[..., None] * b_r - coef_i[..., None] * b_i
    bbar_i = coef_r[..., None] * b_i + coef_i[..., None] * b_r
    return abar_r, abar_i, bbar_r, bbar_i, c[0].astype(F32), c[1].astype(F32)


def s5_tile_weights(bbar_r, bbar_i, c_r, c_i):
    g = bbar_r.shape[0]
    tg = S5_TILE_GROUPS
    nt = g // tg
    eye = jnp.eye(tg, dtype=bool)[None, :, None, :, None]

    def expand(x):
        xt = x.transpose(0, 1, 3, 2)[:, :, :, None, :]
        full = jnp.where(eye, xt, 0.0)
        return full.reshape(nt, tg * x.shape[3], tg * x.shape[2])

    br = expand(bbar_r.reshape(nt, tg, S5_STATE, S5_GROUP))
    bi = expand(bbar_i.reshape(nt, tg, S5_STATE, S5_GROUP))
    cr = expand(c_r.reshape(nt, tg, S5_GROUP, S5_STATE))
    ci = expand(c_i.reshape(nt, tg, S5_GROUP, S5_STATE))
    bcat = jnp.concatenate([br, bi], axis=-1).astype(BF16)
    ccat = jnp.concatenate([cr, -ci], axis=1).astype(BF16)
    return bcat, ccat


def _s5_prompt_kernel(u_ref, bcat_ref, ccat_ref, ar_ref, ai_ref, d_ref, h0_ref,
                      g_ref, hlast_ref, *scr, tc, tp, tg):
    bur_scrs, bui_scrs, h_scr = scr[:tg], scr[tg:2 * tg], scr[2 * tg]
    c = pl.program_id(2)
    sw = S5_TILE_GROUPS * S5_STATE
    cw = S5_TILE_GROUPS * S5_GROUP

    @pl.when(c == 0)
    def _():
        h_scr[...] = h0_ref[0]

    u = u_ref[0]
    ub = u.astype(BF16)
    for k in range(tg):
        bu = jnp.dot(ub[:, k * cw:(k + 1) * cw], bcat_ref[k], preferred_element_type=F32)
        for s in range(SUBLANES):
            bur_scrs[k][pl.ds(s * tp, tc), :] = bu[:, s * LANES:(s + 1) * LANES]
            bui_scrs[k][pl.ds(s * tp, tc), :] = bu[:, sw + s * LANES:sw + (s + 1) * LANES]

    ar = [ar_ref[k] for k in range(tg)]
    ai = [ai_ref[k] for k in range(tg)]

    def step(t, carry):
        new = []
        for k in range(tg):
            hr, hi = carry[2 * k], carry[2 * k + 1]
            br = bur_scrs[k][pl.ds(t, SUBLANES, stride=tp), :]
            bi = bui_scrs[k][pl.ds(t, SUBLANES, stride=tp), :]
            hr2 = ar[k] * hr - ai[k] * hi + br
            hi2 = ar[k] * hi + ai[k] * hr + bi
            bur_scrs[k][pl.ds(t, SUBLANES, stride=tp), :] = hr2
            bui_scrs[k][pl.ds(t, SUBLANES, stride=tp), :] = hi2
            new += [hr2, hi2]
        return tuple(new)

    init = tuple(h_scr[p, k] for k in range(tg) for p in (0, 1))
    fin = lax.fori_loop(0, tc, step, init, unroll=4)
    for k in range(tg):
        h_scr[0, k] = fin[2 * k]
        h_scr[1, k] = fin[2 * k + 1]

    for k in range(tg):
        pieces = [bur_scrs[k][pl.ds(s * tp, tc), :].astype(BF16) for s in range(SUBLANES)]
        pieces += [bui_scrs[k][pl.ds(s * tp, tc), :].astype(BF16) for s in range(SUBLANES)]
        hcat = jnp.concatenate(pieces, axis=-1)
        y = jnp.dot(hcat, ccat_ref[k], preferred_element_type=F32)
        yk = y + d_ref[:, k * cw:(k + 1) * cw] * u[:, k * cw:(k + 1) * cw]
        g_ref[0, :, k * cw:(k + 1) * cw] = jax.nn.gelu(yk).astype(g_ref.dtype)

    @pl.when(c == pl.num_programs(2) - 1)
    def _():
        hlast_ref[0] = h_scr[...]


def s5_prompt(uz, h0, abar_r, abar_i, bcat, ccat, d, *, tc=512, tg=4):
    n, t, w2 = uz.shape
    w = w2 // 2
    g = w // S5_GROUP
    nt = g // S5_TILE_GROUPS
    tg = min(tg, nt)
    tc = min(tc, t)
    tp = tc + SUBLANES
    cw = S5_TILE_GROUPS * S5_GROUP
    sw = S5_TILE_GROUPS * S5_STATE
    h0t = h0.reshape(n, 2, nt, SUBLANES, LANES)
    art = abar_r.reshape(nt, SUBLANES, LANES)
    ait = abar_i.reshape(nt, SUBLANES, LANES)
    kern = functools.partial(_s5_prompt_kernel, tc=tc, tp=tp, tg=tg)
    gout, hlast = pl.pallas_call(
        kern,
        grid=(nt // tg, n, t // tc),
        in_specs=[
            pl.BlockSpec((1, tc, tg * cw), lambda j, i, c: (i, c, j)),
            pl.BlockSpec((tg, cw, 2 * sw), lambda j, i, c: (j, 0, 0)),
            pl.BlockSpec((tg, 2 * sw, cw), lambda j, i, c: (j, 0, 0)),
            pl.BlockSpec((tg, SUBLANES, LANES), lambda j, i, c: (j, 0, 0)),
            pl.BlockSpec((tg, SUBLANES, LANES), lambda j, i, c: (j, 0, 0)),
            pl.BlockSpec((1, tg * cw), lambda j, i, c: (0, j)),
            pl.BlockSpec((1, 2, tg, SUBLANES, LANES), lambda j, i, c: (i, 0, j, 0, 0)),
        ],
        out_specs=[
            pl.BlockSpec((1, tc, tg * cw), lambda j, i, c: (i, c, j)),
            pl.BlockSpec((1, 2, tg, SUBLANES, LANES), lambda j, i, c: (i, 0, j, 0, 0)),
        ],
        out_shape=[jax.ShapeDtypeStruct((n, t, w), BF16),
                   jax.ShapeDtypeStruct((n, 2, nt, SUBLANES, LANES), F32)],
        scratch_shapes=[pltpu.VMEM((SUBLANES * tp, LANES), F32)] * (2 * tg)
        + [pltpu.VMEM((2, tg, SUBLANES, LANES), F32)],
        compiler_params=pltpu.CompilerParams(
            dimension_semantics=("parallel", "parallel", "arbitrary"),
            vmem_limit_bytes=V7X_VMEM_LIMIT_BYTES),
        name="s5_prompt",
    )(uz, bcat, ccat, art, ait, d.reshape(1, w), h0t)
    return gout, hlast.reshape(n, 2, g, S5_STATE)


def _s5_step_kernel(u_ref, bcat_ref, ccat_ref, ar_ref, ai_ref, d_ref, h0_ref, g_ref, h_ref):
    sw = S5_TILE_GROUPS * S5_STATE
    u = u_ref[...]
    bu = jnp.dot(u.astype(BF16), bcat_ref[0], preferred_element_type=F32)
    hr, hi = h0_ref[:, 0, :], h0_ref[:, 1, :]
    ar, ai = ar_ref[0], ai_ref[0]
    hr2 = ar * hr - ai * hi + bu[:, :sw]
    hi2 = ar * hi + ai * hr + bu[:, sw:]
    h_ref[:, 0, :] = hr2
    h_ref[:, 1, :] = hi2
    hcat = jnp.concatenate([hr2.astype(BF16), hi2.astype(BF16)], axis=-1)
    y = jnp.dot(hcat, ccat_ref[0], preferred_element_type=F32)
    g_ref[...] = jax.nn.gelu(y + d_ref[...] * u).astype(g_ref.dtype)


def s5_step(uz, h0, abar_r, abar_i, bcat, ccat, d):
    n, w2 = uz.shape
    w = w2 // 2
    g = w // S5_GROUP
    nt = g // S5_TILE_GROUPS
    cw = S5_TILE_GROUPS * S5_GROUP
    sw = S5_TILE_GROUPS * S5_STATE
    gout, hnew = pl.pallas_call(
        _s5_step_kernel,
        grid=(nt,),
        in_specs=[
            pl.BlockSpec((n, cw), lambda j: (0, j)),
            pl.BlockSpec((1, cw, 2 * sw), lambda j: (j, 0, 0)),
            pl.BlockSpec((1, 2 * sw, cw), lambda j: (j, 0, 0)),
            pl.BlockSpec((1, 1, sw), lambda j: (j, 0, 0)),
            pl.BlockSpec((1, 1, sw), lambda j: (j, 0, 0)),
            pl.BlockSpec((1, cw), lambda j: (0, j)),
            pl.BlockSpec((n, 2, sw), lambda j: (0, 0, j)),
        ],
        out_specs=[pl.BlockSpec((n, cw), lambda j: (0, j)),
                   pl.BlockSpec((n, 2, sw), lambda j: (0, 0, j))],
        out_shape=[jax.ShapeDtypeStruct((n, w), BF16),
                   jax.ShapeDtypeStruct((n, 2, g * S5_STATE), F32)],
        compiler_params=pltpu.CompilerParams(dimension_semantics=("parallel",),
                                             vmem_limit_bytes=V7X_VMEM_LIMIT_BYTES),
        name="s5_step",
    )(uz, bcat, ccat, abar_r.reshape(nt, 1, sw), abar_i.reshape(nt, 1, sw), d.reshape(1, w),
      h0.reshape(n, 2, g * S5_STATE))
    return gout, hnew.reshape(n, 2, g, S5_STATE)


def s5_params(w_in, a_re, a_im, log_dt, b, c, d, w_glu, b_glu, w_out):
    abar_r, abar_i, bbar_r, bbar_i, c_r, c_i = s5_discretize(a_re, a_im, log_dt, b, c)
    bcat, ccat = s5_tile_weights(bbar_r, bbar_i, c_r, c_i)
    return dict(w_in=w_in.astype(BF16), abar_r=abar_r, abar_i=abar_i, bcat=bcat, ccat=ccat,
                d=d.astype(F32), w_glu=w_glu.astype(BF16), b_glu=b_glu, w_out=w_out.astype(BF16))


def _matmul_glu_kernel(a_ref, b_ref, bias_ref, g_ref, z_ref, o_ref, acc_ref):
    k = pl.program_id(2)

    @pl.when(k == 0)
    def _():
        acc_ref[...] = jnp.zeros_like(acc_ref)

    acc_ref[...] += jnp.dot(a_ref[...], b_ref[...], preferred_element_type=F32)

    @pl.when(k == pl.num_programs(2) - 1)
    def _():
        gate = jax.nn.sigmoid(acc_ref[...] + bias_ref[...])
        o_ref[...] = (g_ref[...].astype(F32) * gate * jax.nn.silu(z_ref[...])).astype(o_ref.dtype)


def matmul_glu(g, w, bias, uz):
    m, k = g.shape
    _, n = w.shape
    tm = _pick_tile(m, 1024, 8)
    tn = _pick_tile(n, 512, LANES)
    tk = _pick_tile(k, 4096, LANES)
    z_off = n // tn
    return pl.pallas_call(
        _matmul_glu_kernel,
        grid=(m // tm, n // tn, k // tk),
        in_specs=[pl.BlockSpec((tm, tk), lambda i, j, kk: (i, kk)),
                  pl.BlockSpec((tk, tn), lambda i, j, kk: (kk, j)),
                  pl.BlockSpec((1, tn), lambda i, j, kk: (0, j)),
                  pl.BlockSpec((tm, tn), lambda i, j, kk: (i, j)),
                  pl.BlockSpec((tm, tn), lambda i, j, kk: (i, z_off + j))],
        out_specs=pl.BlockSpec((tm, tn), lambda i, j, kk: (i, j)),
        out_shape=jax.ShapeDtypeStruct((m, n), BF16),
        scratch_shapes=[pltpu.VMEM((tm, tn), F32)],
        compiler_params=pltpu.CompilerParams(
            dimension_semantics=("parallel", "parallel", "arbitrary"),
            vmem_limit_bytes=V7X_VMEM_LIMIT_BYTES),
        name="matmul_glu",
    )(g, w, bias.astype(F32).reshape(1, n), g, uz)


def s5_layer(x, h0, p):
    n, t, _ = x.shape
    uz = proj(x, p["w_in"])
    w = uz.shape[-1] // 2
    if t == 1:
        g, h = s5_step(uz[:, 0], h0, p["abar_r"], p["abar_i"], p["bcat"], p["ccat"], p["d"])
    else:
        g, h = s5_prompt(uz, h0, p["abar_r"], p["abar_i"], p["bcat"], p["ccat"], p["d"])
    gz = matmul_glu(g.reshape(n * t, w), p["w_glu"], p["b_glu"], uz.reshape(n * t, 2 * w))
    return matmul(gz, p["w_out"]).reshape(n, t, -1), (h,)


def _round_up(x, m):
    return -(-x // m) * m


def rg_pad_cols(x, nb, bp):
    lead = x.shape[:-1]
    blk = x.shape[-1] // nb
    x = x.reshape(lead + (nb, blk))
    x = jnp.pad(x, [(0, 0)] * len(lead) + [(0, 0), (0, bp - blk)])
    return x.reshape(lead + (nb * bp,))


def rg_unpad_cols(x, nb, blk):
    lead = x.shape[:-1]
    bp = x.shape[-1] // nb
    return x.reshape(lead + (nb, bp))[..., :blk].reshape(lead + (nb * blk,))


def rg_params(w_in, conv_w, conv_b, w_rg, b_rg, w_ig, b_ig, lam, w_out):
    nb, blk, _ = w_rg.shape
    bp = _round_up(blk, LANES)
    width = nb * blk
    w_in_p = jnp.concatenate([rg_pad_cols(w_in[:, :width], nb, bp), rg_pad_cols(w_in[:, width:], nb, bp)],
                             axis=1).astype(BF16)
    pad_sq = lambda w: jnp.pad(w, ((0, 0), (0, bp - blk), (0, bp - blk)))
    w_gate = jnp.concatenate([pad_sq(w_rg), pad_sq(w_ig)], axis=-1).astype(BF16)
    b_gate = jnp.stack([rg_pad_cols(b_rg.astype(F32), nb, bp).reshape(nb, bp),
                        rg_pad_cols(b_ig.astype(F32), nb, bp).reshape(nb, bp)], axis=1)
    b_gate = b_gate.reshape(nb, 1, 2 * bp)
    sp = rg_pad_cols(jax.nn.softplus(-lam.astype(F32)), nb, bp).reshape(1, nb * bp)
    w_out_p = jnp.pad(w_out.reshape(nb, blk, -1), ((0, 0), (0, bp - blk), (0, 0))).reshape(nb * bp, -1)
    return dict(nb=nb, blk=blk, bp=bp, w_in=w_in_p, conv_w=rg_pad_cols(conv_w.astype(F32), nb, bp),
                conv_b=rg_pad_cols(conv_b.astype(F32), nb, bp).reshape(1, nb * bp),
                w_gate=w_gate, b_gate=b_gate, sp=sp, w_out=w_out_p.astype(BF16))


def _rg_gates(uc, wg, bg, sp, bp):
    gates = jnp.dot(uc.astype(BF16), wg, preferred_element_type=F32) + bg
    r = jax.nn.sigmoid(gates[:, :bp])
    ig = jax.nn.sigmoid(gates[:, bp:])
    log_a = -RG_C * r * sp
    a = jnp.exp(log_a)
    xin = jnp.sqrt(1.0 - a * a) * (ig * uc)
    return a, xin


def _rg_prompt_kernel(u_ref, z_ref, cw_ref, cb_ref, wg_ref, bg_ref, sp_ref, cbuf_ref, h0_ref,
                      o_ref, tail_ref, hlast_ref, ext_scr, h_scr, *slabs, tc, pitch):
    c = pl.program_id(2)
    bp = u_ref.shape[-1]
    nl = bp // LANES
    a_scrs, x_scrs = slabs[:nl], slabs[nl:]
    sub = tc // SUBLANES

    @pl.when(c == 0)
    def _():
        ext_scr[0:SUBLANES, :] = cbuf_ref[0]
        h_scr[...] = h0_ref[0]

    ext_scr[pl.ds(SUBLANES, tc), :] = u_ref[0]
    w = cw_ref[...]
    uc = (w[3:4] * ext_scr[pl.ds(SUBLANES, tc), :] + w[2:3] * ext_scr[pl.ds(SUBLANES - 1, tc), :]
          + w[1:2] * ext_scr[pl.ds(SUBLANES - 2, tc), :] + w[0:1] * ext_scr[pl.ds(SUBLANES - 3, tc), :]
          + cb_ref[...])
    ext_scr[0:SUBLANES, :] = ext_scr[pl.ds(tc, SUBLANES), :]

    a, xin = _rg_gates(uc, wg_ref[0], bg_ref[0], sp_ref[...], bp)
    for l in range(nl):
        for s in range(SUBLANES):
            a_scrs[l][pl.ds(s * pitch, sub), :] = a[s * sub:(s + 1) * sub, l * LANES:(l + 1) * LANES]
            x_scrs[l][pl.ds(s * pitch, sub), :] = xin[s * sub:(s + 1) * sub, l * LANES:(l + 1) * LANES]

    def step(j, carry):
        new = []
        for l in range(nl):
            p, s_ = carry[2 * l], carry[2 * l + 1]
            aj = a_scrs[l][pl.ds(j, SUBLANES, stride=pitch), :]
            xj = x_scrs[l][pl.ds(j, SUBLANES, stride=pitch), :]
            p = aj * p
            s_ = aj * s_ + xj
            a_scrs[l][pl.ds(j, SUBLANES, stride=pitch), :] = p
            x_scrs[l][pl.ds(j, SUBLANES, stride=pitch), :] = s_
            new += [p, s_]
        return tuple(new)

    init = tuple(jnp.ones((SUBLANES, LANES), F32) if i % 2 == 0 else jnp.zeros((SUBLANES, LANES), F32)
                 for i in range(2 * nl))
    fin = lax.fori_loop(0, sub, step, init)

    for l in range(nl):
        p_end, s_end = fin[2 * l], fin[2 * l + 1]
        h = h_scr[:, l * LANES:(l + 1) * LANES]
        for s in range(SUBLANES):
            rows = pl.ds(s * pitch, sub)
            hs = x_scrs[l][rows, :] + a_scrs[l][rows, :] * h
            zs = z_ref[0, s * sub:(s + 1) * sub, l * LANES:(l + 1) * LANES]
            o_ref[0, s * sub:(s + 1) * sub, l * LANES:(l + 1) * LANES] = (hs * jax.nn.silu(zs)).astype(o_ref.dtype)
            h = s_end[s:s + 1, :] + p_end[s:s + 1, :] * h
        h_scr[:, l * LANES:(l + 1) * LANES] = h

    @pl.when(c == pl.num_programs(2) - 1)
    def _():
        tail_ref[0] = ext_scr[0:SUBLANES, :]
        hlast_ref[0] = h_scr[...]


def rg_prompt(uz, conv_buf, h0, p, *, tc=512):
    n, t, _ = uz.shape
    nb, blk, bp = p["nb"], p["blk"], p["bp"]
    tc = min(tc, t)
    sub = tc // SUBLANES
    pitch = sub + SUBLANES
    nl = bp // LANES
    cbuf = jnp.pad(rg_pad_cols(conv_buf.astype(F32), nb, bp), ((0, 0), (SUBLANES - (RG_CONV - 1), 0), (0, 0)))
    h0p = rg_pad_cols(h0.astype(F32), nb, bp).reshape(n, 1, nb * bp)
    kern = functools.partial(_rg_prompt_kernel, tc=tc, pitch=pitch)
    o, tail, hlast = pl.pallas_call(
        kern,
        grid=(n, nb, t // tc),
        in_specs=[
            pl.BlockSpec((1, tc, bp), lambda i, b, c: (i, c, b)),
            pl.BlockSpec((1, tc, bp), lambda i, b, c: (i, c, nb + b)),
            pl.BlockSpec((RG_CONV, bp), lambda i, b, c: (0, b)),
            pl.BlockSpec((1, bp), lambda i, b, c: (0, b)),
            pl.BlockSpec((1, bp, 2 * bp), lambda i, b, c: (b, 0, 0)),
            pl.BlockSpec((1, 1, 2 * bp), lambda i, b, c: (b, 0, 0)),
            pl.BlockSpec((1, bp), lambda i, b, c: (0, b)),
            pl.BlockSpec((1, SUBLANES, bp), lambda i, b, c: (i, 0, b)),
            pl.BlockSpec((1, 1, bp), lambda i, b, c: (i, 0, b)),
        ],
        out_specs=[
            pl.BlockSpec((1, tc, bp), lambda i, b, c: (i, c, b)),
            pl.BlockSpec((1, SUBLANES, bp), lambda i, b, c: (i, 0, b)),
            pl.BlockSpec((1, 1, bp), lambda i, b, c: (i, 0, b)),
        ],
        out_shape=[jax.ShapeDtypeStruct((n, t, nb * bp), BF16),
                   jax.ShapeDtypeStruct((n, SUBLANES, nb * bp), F32),
                   jax.ShapeDtypeStruct((n, 1, nb * bp), F32)],
        scratch_shapes=[pltpu.VMEM((tc + SUBLANES, bp), F32), pltpu.VMEM((1, bp), F32)]
        + [pltpu.VMEM((SUBLANES * pitch, LANES), F32)] * (2 * nl),
        compiler_params=pltpu.CompilerParams(
            dimension_semantics=("parallel", "parallel", "arbitrary"),
            vmem_limit_bytes=V7X_VMEM_LIMIT_BYTES),
        name="rg_prompt",
    )(uz, uz, p["conv_w"], p["conv_b"], p["w_gate"], p["b_gate"], p["sp"], cbuf, h0p)
    conv_state = rg_unpad_cols(tail[:, SUBLANES - (RG_CONV - 1):], nb, blk)
    return o, conv_state, rg_unpad_cols(hlast[:, 0], nb, blk)


def _rg_step_kernel(u_ref, z_ref, cw_ref, cb_ref, wg_ref, bg_ref, sp_ref, cbuf_ref, h0_ref, o_ref, h_ref):
    bp = u_ref.shape[-1]
    w = cw_ref[...]
    uc = (w[3:4] * u_ref[...] + w[2:3] * cbuf_ref[2] + w[1:2] * cbuf_ref[1] + w[0:1] * cbuf_ref[0]
          + cb_ref[...])
    a, xin = _rg_gates(uc, wg_ref[0], bg_ref[0], sp_ref[...], bp)
    h = a * h0_ref[...] + xin
    h_ref[...] = h
    o_ref[...] = (h * jax.nn.silu(z_ref[...])).astype(o_ref.dtype)


def rg_step(uz, conv_buf, h0, p):
    n, _ = uz.shape
    nb, blk, bp = p["nb"], p["blk"], p["bp"]
    cbuf = rg_pad_cols(conv_buf.astype(F32), nb, bp).transpose(1, 0, 2)
    h0p = rg_pad_cols(h0.astype(F32), nb, bp)
    o, hnew = pl.pallas_call(
        _rg_step_kernel,
        grid=(nb,),
        in_specs=[
            pl.BlockSpec((n, bp), lambda b: (0, b)),
            pl.BlockSpec((n, bp), lambda b: (0, nb + b)),
            pl.BlockSpec((RG_CONV, bp), lambda b: (0, b)),
            pl.BlockSpec((1, bp), lambda b: (0, b)),
            pl.BlockSpec((1, bp, 2 * bp), lambda b: (b, 0, 0)),
            pl.BlockSpec((1, 1, 2 * bp), lambda b: (b, 0, 0)),
            pl.BlockSpec((1, bp), lambda b: (0, b)),
            pl.BlockSpec((RG_CONV - 1, n, bp), lambda b: (0, 0, b)),
            pl.BlockSpec((n, bp), lambda b: (0, b)),
        ],
        out_specs=[pl.BlockSpec((n, bp), lambda b: (0, b)), pl.BlockSpec((n, bp), lambda b: (0, b))],
        out_shape=[jax.ShapeDtypeStruct((n, nb * bp), BF16), jax.ShapeDtypeStruct((n, nb * bp), F32)],
        compiler_params=pltpu.CompilerParams(dimension_semantics=("parallel",),
                                             vmem_limit_bytes=V7X_VMEM_LIMIT_BYTES),
        name="rg_step",
    )(uz, uz, p["conv_w"], p["conv_b"], p["w_gate"], p["b_gate"], p["sp"], cbuf, h0p)
    u_new = rg_unpad_cols(uz[:, :nb * bp], nb, blk)
    conv_state = jnp.concatenate([conv_buf[:, 1:].astype(F32), u_new[:, None]], axis=1)
    return o, conv_state, rg_unpad_cols(hnew, nb, blk)


def rglru_layer(x, conv_buf, h0, p):
    n, t, _ = x.shape
    uz = proj(x, p["w_in"])
    if t == 1:
        o, conv_state, h = rg_step(uz[:, 0], conv_buf, h0, p)
    else:
        o, conv_state, h = rg_prompt(uz, conv_buf, h0, p)
    y = matmul(o.reshape(n * t, -1), p["w_out"]).reshape(n, t, -1)
    return y, (conv_state, h)


def _masked_softmax_rows(s, ok):
    s = jnp.where(ok, s, NEG)
    mx = jnp.max(s, axis=-1, keepdims=True)
    p = jnp.where(ok, jnp.exp(s - mx), 0.0)
    den = jnp.maximum(jnp.sum(p, axis=-1, keepdims=True), TINY)
    return p / den


def _compress_kernel(x_ref, pe_ref, w1_ref, b1_ref, w2_ref, b2_ref, o_ref, *, nh):
    half = CMP_LEN // 2
    acc_lo = jnp.zeros((nh, w1_ref.shape[-1]), F32)
    acc_hi = jnp.zeros((nh, w1_ref.shape[-1]), F32)
    for l in range(half):
        x = x_ref[pl.ds(l, nh, stride=CMP_STRIDE), :]
        acc_lo += jnp.dot((x + pe_ref[0, l:l + 1, :]).astype(BF16), w1_ref[0, l],
                          preferred_element_type=F32)
        acc_hi += jnp.dot((x + pe_ref[0, half + l:half + l + 1, :]).astype(BF16), w1_ref[0, half + l],
                          preferred_element_type=F32)
    hid = jax.nn.gelu(acc_lo + pltpu.roll(acc_hi, nh - 1, axis=0) + b1_ref[0])
    out = jnp.dot(hid.astype(BF16), w2_ref[0], preferred_element_type=F32) + b2_ref[0]
    row = lax.broadcasted_iota(jnp.int32, out.shape, 0)
    o_ref[0, 0, 0] = jnp.where(row < nh - 1, out, 0.0)


def nsa_compress_prompt(pj, cmp_pe, cmp_w1, cmp_b1, cmp_w2, cmp_b2):
    n, t, _ = pj.shape
    nh = t // CMP_STRIDE
    hidden = cmp_w1.shape[-1]
    kern = functools.partial(_compress_kernel, nh=nh)
    return pl.pallas_call(
        kern,
        grid=(n, 2, N_KV),
        in_specs=[
            pl.BlockSpec((None, t, LANES), lambda i, ty, h: (i, 0, COL_KV + ty * N_KV + h)),
            pl.BlockSpec((1, CMP_LEN, HEAD_DIM), lambda i, ty, h: (ty, 0, 0)),
            pl.BlockSpec((1, CMP_LEN, HEAD_DIM, hidden), lambda i, ty, h: (ty, 0, 0, 0)),
            pl.BlockSpec((1, 1, hidden), lambda i, ty, h: (ty, 0, 0)),
            pl.BlockSpec((1, hidden, HEAD_DIM), lambda i, ty, h: (ty, 0, 0)),
            pl.BlockSpec((1, 1, HEAD_DIM), lambda i, ty, h: (ty, 0, 0)),
        ],
        out_specs=pl.BlockSpec((1, 1, 1, nh, HEAD_DIM), lambda i, ty, h: (i, ty, h, 0, 0)),
        out_shape=jax.ShapeDtypeStruct((n, 2, N_KV, nh, HEAD_DIM), F32),
        compiler_params=pltpu.CompilerParams(
            dimension_semantics=("parallel", "parallel", "parallel"),
            vmem_limit_bytes=V7X_VMEM_LIMIT_BYTES),
        name="nsa_compress_prompt",
    )(pj, cmp_pe, cmp_w1.astype(BF16), cmp_b1.reshape(2, 1, hidden), cmp_w2.astype(BF16),
      cmp_b2.reshape(2, 1, HEAD_DIM))


def _nsa_prompt_kernel(q_ref, kc_ref, vc_ref, ks_ref, vs_ref, kw_ref, vw_ref, z_ref, gl_ref, bg_ref,
                       ovl_ref, exp_ref, o_ref, m_scr, l_scr, acc_scr, *, bq, tk, t, n_sel, n_top, wlen):
    qb = pl.program_id(2)
    q = q_ref[0] * (HEAD_DIM ** -0.5)
    qs = jnp.concatenate([q[:, g * HEAD_DIM:(g + 1) * HEAD_DIM] for g in range(Q_PER_KV)],
                         axis=0).astype(BF16)
    nt_dims = (((1,), (1,)), ((), ()))

    def per_head(x):
        return jnp.concatenate([x] * Q_PER_KV, axis=0)

    qpos = qb * bq + lax.broadcasted_iota(jnp.int32, (bq, LANES), 0)
    lane = lax.broadcasted_iota(jnp.int32, (bq, LANES), 1)

    ncp = kc_ref.shape[3]
    s = lax.dot_general(qs, kc_ref[0, 0, 0].astype(BF16), nt_dims, preferred_element_type=F32)
    cpos = lax.broadcasted_iota(jnp.int32, (bq, ncp), 1) * CMP_STRIDE + (CMP_LEN - 1)
    qpos_c = qb * bq + lax.broadcasted_iota(jnp.int32, (bq, ncp), 0)
    p_cmp = _masked_softmax_rows(s, per_head(cpos <= qpos_c))
    o_cmp = jnp.dot(p_cmp.astype(BF16), vc_ref[0, 0, 0].astype(BF16), preferred_element_type=F32)

    psum = p_cmp[0:bq]
    for g in range(1, Q_PER_KV):
        psum = psum + p_cmp[g * bq:(g + 1) * bq]
    p_hi = psum.astype(BF16)
    p_lo = (psum - p_hi.astype(F32)).astype(BF16)
    imp = (jnp.dot(p_hi, ovl_ref[...], preferred_element_type=F32)
           + jnp.dot(p_lo, ovl_ref[...], preferred_element_type=F32))
    cur = qpos // SLC_BLOCK
    forced = (lane == 0) | (lane == cur) | (lane == cur - 1)
    imp = jnp.where(forced, FORCE_SCORE, jnp.where(lane > cur, -1.0, imp))
    imp = jnp.where(lane < n_sel, imp, -2.0)
    rank = jnp.zeros((bq, LANES), F32)
    for b in range(n_sel):
        col = jnp.broadcast_to(imp[:, b:b + 1], (bq, LANES))
        tie = jnp.where(lane > b, 1.0, 0.0)
        rank = rank + jnp.where(col > imp, 1.0, jnp.where(col == imp, tie, 0.0))
    sel = jnp.where(rank < n_top, 1.0, 0.0).astype(BF16)

    m_scr[...] = jnp.full(m_scr.shape, NEG, F32)
    l_scr[...] = jnp.zeros(l_scr.shape, F32)
    acc_scr[...] = jnp.zeros(acc_scr.shape, F32)
    reps = tk // LANES
    qpos_k = qb * bq + lax.broadcasted_iota(jnp.int32, (bq, tk), 0)
    koff = lax.broadcasted_iota(jnp.int32, (bq, tk), 1)

    def body(kt, carry):
        start = pl.multiple_of(kt * tk, tk)
        k = ks_ref[0, pl.ds(start, tk), :].astype(BF16)
        v = vs_ref[0, pl.ds(start, tk), :].astype(BF16)
        s = lax.dot_general(qs, k, nt_dims, preferred_element_type=F32)
        selk = jnp.dot(sel, exp_ref[kt], preferred_element_type=F32)
        ok = per_head((selk > 0.5) & (kt * tk + koff <= qpos_k))
        s = jnp.where(ok, s, NEG)
        m_old = m_scr[...]
        m_new = jnp.maximum(m_old, jnp.max(s, axis=-1, keepdims=True))
        alpha = jnp.exp(m_old - m_new)
        p = jnp.where(ok, jnp.exp(s - jnp.concatenate([m_new] * reps, axis=1)), 0.0)
        l_scr[...] = alpha * l_scr[...] + jnp.sum(p, axis=-1, keepdims=True)
        acc_scr[...] = alpha * acc_scr[...] + jnp.dot(p.astype(BF16), v, preferred_element_type=F32)
        m_scr[...] = m_new
        return carry

    n_tiles = ((qb + 1) * bq + tk - 1) // tk
    lax.fori_loop(0, n_tiles, body, 0)
    o_slc = acc_scr[...] / jnp.maximum(l_scr[...], TINY)

    wstart = pl.multiple_of(jnp.clip(qb * bq + bq - wlen, 0, t - wlen), bq)
    kw = kw_ref[0, pl.ds(wstart, wlen), :].astype(BF16)
    vw = vw_ref[0, pl.ds(wstart, wlen), :].astype(BF16)
    s = lax.dot_general(qs, kw, nt_dims, preferred_element_type=F32)
    diff = (qb * bq + lax.broadcasted_iota(jnp.int32, (bq, wlen), 0)
            - wstart - lax.broadcasted_iota(jnp.int32, (bq, wlen), 1))
    p_win = _masked_softmax_rows(s, per_head((diff >= 0) & (diff <= WINDOW)))
    o_win = jnp.dot(p_win.astype(BF16), vw, preferred_element_type=F32)

    gs = jax.nn.sigmoid(gl_ref[0] + bg_ref[0])
    outs = []
    for g in range(Q_PER_KV):
        r0, r1 = g * bq, (g + 1) * bq
        g_cmp = jnp.broadcast_to(gs[:, 3 * g:3 * g + 1], (bq, HEAD_DIM))
        g_slc = jnp.broadcast_to(gs[:, 3 * g + 1:3 * g + 2], (bq, HEAD_DIM))
        g_win = jnp.broadcast_to(gs[:, 3 * g + 2:3 * g + 3], (bq, HEAD_DIM))
        outs.append(g_cmp * o_cmp[r0:r1] + g_slc * o_slc[r0:r1] + g_win * o_win[r0:r1])
    o = jnp.concatenate(outs, axis=1)
    o_ref[0] = (o * jax.nn.silu(z_ref[0])).astype(o_ref.dtype)


def nsa_prompt_attention(pj, kvc, b_gate, *, bq=128, tk=512):
    n, t, _ = pj.shape
    tk = min(tk, t)
    ncp = kvc.shape[3]
    n_cmp = ncp - 1
    n_sel = -(-t // SLC_BLOCK)
    n_top = min(SLC_TOP, n_sel)
    wlen = min(t, WINDOW + bq)
    c = np.arange(ncp)[:, None]
    sblk = np.arange(LANES)[None, :]
    ovl = ((c * CMP_STRIDE < sblk * SLC_BLOCK + SLC_BLOCK) & (c * CMP_STRIDE + CMP_LEN > sblk * SLC_BLOCK)
           & (sblk < n_sel) & (c < n_cmp))
    ovl = jnp.asarray(ovl, BF16)
    kk = np.arange(t).reshape(t // tk, 1, tk)
    expand = jnp.asarray(kk // SLC_BLOCK == np.arange(LANES)[None, :, None], BF16)
    bg = jnp.zeros((N_KV, 1, LANES), F32).at[:, 0, :3 * Q_PER_KV].set(
        b_gate.astype(F32).reshape(N_KV, 3 * Q_PER_KV))
    kern = functools.partial(_nsa_prompt_kernel, bq=bq, tk=tk, t=t, n_sel=n_sel, n_top=n_top, wlen=wlen)
    rows = Q_PER_KV * bq
    gcb = GROUP_COLS // LANES
    return pl.pallas_call(
        kern,
        grid=(n, N_KV, t // bq),
        in_specs=[
            pl.BlockSpec((1, bq, GROUP_COLS), lambda i, h, qb: (i, qb, h)),
            pl.BlockSpec((1, 1, 1, ncp, HEAD_DIM), lambda i, h, qb: (i, 0, h, 0, 0)),
            pl.BlockSpec((1, 1, 1, ncp, HEAD_DIM), lambda i, h, qb: (i, 1, h, 0, 0)),
            pl.BlockSpec((1, t, LANES), lambda i, h, qb: (i, 0, COL_KV + 2 * N_KV + h)),
            pl.BlockSpec((1, t, LANES), lambda i, h, qb: (i, 0, COL_KV + 3 * N_KV + h)),
            pl.BlockSpec((1, t, LANES), lambda i, h, qb: (i, 0, COL_WIN + h)),
            pl.BlockSpec((1, t, LANES), lambda i, h, qb: (i, 0, COL_WIN + N_KV + h)),
            pl.BlockSpec((1, bq, GROUP_COLS), lambda i, h, qb: (i, qb, COL_Z // gcb + h)),
            pl.BlockSpec((1, bq, LANES), lambda i, h, qb: (i, qb, COL_GATE + h)),
            pl.BlockSpec((1, 1, LANES), lambda i, h, qb: (h, 0, 0)),
            pl.BlockSpec((ncp, LANES), lambda i, h, qb: (0, 0)),
            pl.BlockSpec((t // tk, LANES, tk), lambda i, h, qb: (0, 0, 0)),
        ],
        out_specs=pl.BlockSpec((1, bq, GROUP_COLS), lambda i, h, qb: (i, qb, h)),
        out_shape=jax.ShapeDtypeStruct((n, t, NSA_Q_COLS), BF16),
        scratch_shapes=[pltpu.VMEM((rows, LANES), F32), pltpu.VMEM((rows, LANES), F32),
                        pltpu.VMEM((rows, HEAD_DIM), F32)],
        compiler_params=pltpu.CompilerParams(
            dimension_semantics=("parallel", "parallel", "arbitrary"),
            vmem_limit_bytes=V7X_VMEM_LIMIT_BYTES),
        name="nsa_prompt_attention",
    )(pj, kvc, kvc, pj, pj, pj, pj, pj, pj, bg, ovl, expand)


def _compress_paged_kernel(pt_ref, *refs, nh, page):
    g_pages = PAGES_PER_STEP
    page_refs = refs[:g_pages]
    pe_ref, w1_ref, b1_ref, w2_ref, b2_ref, o_ref, h_scr = refs[g_pages:]
    g = pl.program_id(1)
    half = CMP_LEN // 2
    hpp = page // CMP_STRIDE
    hidden = w1_ref.shape[-1]
    r = lax.broadcasted_iota(jnp.int32, (page, page), 0)
    c = lax.broadcasted_iota(jnp.int32, (page, page), 1)
    perm = jnp.where(c == (r % hpp) * CMP_STRIDE + r // hpp, 1.0, 0.0).astype(BF16)
    xs = []
    for i in range(g_pages):
        x = jnp.concatenate([page_refs[i][:, ty, h, :] for ty in range(2) for h in range(N_KV)], axis=1)
        x_hi = x.astype(BF16)
        x_lo = (x - x_hi.astype(F32)).astype(BF16)
        xs.append(jnp.dot(perm, x_hi, preferred_element_type=F32) + jnp.dot(perm, x_lo, preferred_element_type=F32))
    rows = hpp * g_pages
    for ty in range(2):
        acc_lo = jnp.zeros((N_KV * rows, hidden), F32)
        acc_hi = jnp.zeros((N_KV * rows, hidden), F32)
        for l in range(half):
            pieces = [xs[i][l * hpp:(l + 1) * hpp, (ty * N_KV + h) * HEAD_DIM:(ty * N_KV + h + 1) * HEAD_DIM]
                      for h in range(N_KV) for i in range(g_pages)]
            a = jnp.concatenate(pieces, axis=0)
            acc_lo += jnp.dot((a + pe_ref[ty, l:l + 1, :]).astype(BF16), w1_ref[ty, l],
                              preferred_element_type=F32)
            acc_hi += jnp.dot((a + pe_ref[ty, half + l:half + l + 1, :]).astype(BF16), w1_ref[ty, half + l],
                              preferred_element_type=F32)
        for h in range(N_KV):
            dst = pl.ds(pl.multiple_of(g * rows, rows), rows)
            h_scr[ty * N_KV + h, 0, dst, :] = acc_lo[h * rows:(h + 1) * rows]
            h_scr[ty * N_KV + h, 1, dst, :] = acc_hi[h * rows:(h + 1) * rows]

    @pl.when(g == pl.num_programs(1) - 1)
    def _():
        for ty in range(2):
            for h in range(N_KV):
                lo = h_scr[ty * N_KV + h, 0]
                hi = h_scr[ty * N_KV + h, 1]
                hid = jax.nn.gelu(lo + pltpu.roll(hi, nh - 1, axis=0) + b1_ref[ty])
                out = jnp.dot(hid.astype(BF16), w2_ref[ty], preferred_element_type=F32) + b2_ref[ty]
                row = lax.broadcasted_iota(jnp.int32, out.shape, 0)
                o_ref[0, ty, h] = jnp.where(row < nh - 1, out, 0.0)


def nsa_compress_paged(cache, page_table, cmp_pe, cmp_w1, cmp_b1, cmp_w2, cmp_b2):
    n, npages = page_table.shape
    page = cache.shape[1]
    nh = npages * page // CMP_STRIDE
    hidden = cmp_w1.shape[-1]
    gp = PAGES_PER_STEP
    kern = functools.partial(_compress_paged_kernel, nh=nh, page=page)
    page_specs = [pl.BlockSpec((None, page, 2, N_KV, HEAD_DIM),
                               lambda i, g, pt, j=j: (pt[i, g * gp + j], 0, 0, 0, 0)) for j in range(gp)]
    grid_spec = pltpu.PrefetchScalarGridSpec(
        num_scalar_prefetch=1,
        grid=(n, npages // gp),
        in_specs=page_specs + [
            pl.BlockSpec((2, CMP_LEN, HEAD_DIM), lambda i, g, pt: (0, 0, 0)),
            pl.BlockSpec((2, CMP_LEN, HEAD_DIM, hidden), lambda i, g, pt: (0, 0, 0, 0)),
            pl.BlockSpec((2, 1, hidden), lambda i, g, pt: (0, 0, 0)),
            pl.BlockSpec((2, hidden, HEAD_DIM), lambda i, g, pt: (0, 0, 0)),
            pl.BlockSpec((2, 1, HEAD_DIM), lambda i, g, pt: (0, 0, 0)),
        ],
        out_specs=pl.BlockSpec((1, 2, N_KV, nh, HEAD_DIM), lambda i, g, pt: (i, 0, 0, 0, 0)),
        scratch_shapes=[pltpu.VMEM((2 * N_KV, 2, nh, hidden), F32)],
    )
    return pl.pallas_call(
        kern,
        grid_spec=grid_spec,
        out_shape=jax.ShapeDtypeStruct((n, 2, N_KV, nh, HEAD_DIM), F32),
        compiler_params=pltpu.CompilerParams(dimension_semantics=("parallel", "arbitrary"),
                                             vmem_limit_bytes=V7X_VMEM_LIMIT_BYTES),
        name="nsa_compress_paged",
    )(page_table, *([cache] * gp), cmp_pe, cmp_w1.astype(BF16), cmp_b1.reshape(2, 1, hidden),
      cmp_w2.astype(BF16), cmp_b2.reshape(2, 1, HEAD_DIM))


def _decode_select_kernel(q_ref, kc_ref, vc_ref, ovl_ref, ocmp_ref, sel_ref, *, past, n_sel, n_top):
    nt_dims = (((1,), (1,)), ((), ()))
    qs = (q_ref[0] * (HEAD_DIM ** -0.5)).astype(BF16)
    nh = kc_ref.shape[3]
    s = lax.dot_general(qs, kc_ref[0, 0, 0].astype(BF16), nt_dims, preferred_element_type=F32)
    cpos = lax.broadcasted_iota(jnp.int32, (Q_PER_KV, nh), 1) * CMP_STRIDE + (CMP_LEN - 1)
    p_cmp = _masked_softmax_rows(s, cpos <= past)
    ocmp_ref[0, 0] = jnp.dot(p_cmp.astype(BF16), vc_ref[0, 0, 0].astype(BF16), preferred_element_type=F32)
    psum = jnp.sum(p_cmp, axis=0, keepdims=True)
    p_hi = psum.astype(BF16)
    p_lo = (psum - p_hi.astype(F32)).astype(BF16)
    imp = (jnp.dot(p_hi, ovl_ref[...], preferred_element_type=F32)
           + jnp.dot(p_lo, ovl_ref[...], preferred_element_type=F32))
    nsp = imp.shape[1]
    lane = lax.broadcasted_iota(jnp.int32, (1, nsp), 1)
    cur = past // SLC_BLOCK
    forced = (lane == 0) | (lane == cur) | (lane == cur - 1)
    imp = jnp.where(forced, FORCE_SCORE, jnp.where(lane > cur, -1.0, imp))
    imp = jnp.where(lane < n_sel, imp, -2.0)
    rank = jnp.zeros((1, nsp), F32)
    for b in range(n_sel):
        col = jnp.broadcast_to(imp[:, b:b + 1], (1, nsp))
        tie = jnp.where(lane > b, 1.0, 0.0)
        rank = rank + jnp.where(col > imp, 1.0, jnp.where(col == imp, tie, 0.0))
    sel_ref[0, 0] = jnp.where(rank < n_top, 1.0, 0.0)


def nsa_decode_select(q, kvc, past):
    n = q.shape[0]
    nh = kvc.shape[3]
    n_cmp = nh - 1
    n_sel = past // SLC_BLOCK + 1
    n_top = min(SLC_TOP, n_sel)
    nsp = _round_up(n_sel, LANES)
    c = np.arange(nh)[:, None]
    sblk = np.arange(nsp)[None, :]
    ovl = ((c * CMP_STRIDE < sblk * SLC_BLOCK + SLC_BLOCK) & (c * CMP_STRIDE + CMP_LEN > sblk * SLC_BLOCK)
           & (sblk < n_sel) & (c < n_cmp))
    ovl = jnp.asarray(ovl, BF16)
    kern = functools.partial(_decode_select_kernel, past=past, n_sel=n_sel, n_top=n_top)
    return pl.pallas_call(
        kern,
        grid=(n, N_KV),
        in_specs=[
            pl.BlockSpec((1, Q_PER_KV, HEAD_DIM), lambda i, h: (i, h, 0)),
            pl.BlockSpec((1, 1, 1, nh, HEAD_DIM), lambda i, h: (i, 0, h, 0, 0)),
            pl.BlockSpec((1, 1, 1, nh, HEAD_DIM), lambda i, h: (i, 1, h, 0, 0)),
            pl.BlockSpec((nh, nsp), lambda i, h: (0, 0)),
        ],
        out_specs=[pl.BlockSpec((1, 1, Q_PER_KV, HEAD_DIM), lambda i, h: (i, h, 0, 0)),
                   pl.BlockSpec((1, 1, 1, nsp), lambda i, h: (i, h, 0, 0))],
        out_shape=[jax.ShapeDtypeStruct((n, N_KV, Q_PER_KV, HEAD_DIM), F32),
                   jax.ShapeDtypeStruct((n, N_KV, 1, nsp), F32)],
        compiler_params=pltpu.CompilerParams(dimension_semantics=("parallel", "parallel"),
                                             vmem_limit_bytes=V7X_VMEM_LIMIT_BYTES),
        name="nsa_decode_select",
    )(q, kvc, kvc, ovl)


def _decode_slc_kernel(pt_ref, q_ref, sel_ref, *refs, page):
    page_refs = refs[:PAGES_PER_STEP]
    knew_ref, vnew_ref, o_ref, m_scr, l_scr, acc_scr = refs[PAGES_PER_STEP:]
    p = pl.program_id(1)
    nt_dims = (((1,), (1,)), ((), ()))
    nsp = sel_ref.shape[3]
    keys = PAGES_PER_STEP * page
    reps = keys // LANES

    @pl.when(p == 0)
    def _():
        m_scr[...] = jnp.full(m_scr.shape, NEG, F32)
        l_scr[...] = jnp.zeros(l_scr.shape, F32)
        acc_scr[...] = jnp.zeros(acc_scr.shape, F32)

    blk = lax.broadcasted_iota(jnp.int32, (nsp, keys), 0)
    key = lax.broadcasted_iota(jnp.int32, (nsp, keys), 1)
    expand = jnp.where(blk == (p * keys + key) // SLC_BLOCK, 1.0, 0.0).astype(BF16)
    for h in range(N_KV):
        qs = (q_ref[0, h * Q_PER_KV:(h + 1) * Q_PER_KV, :] * (HEAD_DIM ** -0.5)).astype(BF16)
        k = jnp.concatenate([r[:, 0, h, :] for r in page_refs], axis=0).astype(BF16)
        v = jnp.concatenate([r[:, 1, h, :] for r in page_refs], axis=0).astype(BF16)
        s = lax.dot_general(qs, k, nt_dims, preferred_element_type=F32)
        sel = jnp.broadcast_to(sel_ref[0, h], (Q_PER_KV, nsp)).astype(BF16)
        ok = jnp.dot(sel, expand, preferred_element_type=F32) > 0.5
        s = jnp.where(ok, s, NEG)
        m_old = m_scr[h]
        m_new = jnp.maximum(m_old, jnp.max(s, axis=-1, keepdims=True))
        alpha = jnp.exp(m_old - m_new)
        pr = jnp.where(ok, jnp.exp(s - jnp.concatenate([m_new] * reps, axis=1)), 0.0)
        l_scr[h] = alpha * l_scr[h] + jnp.sum(pr, axis=-1, keepdims=True)
        acc_scr[h] = alpha * acc_scr[h] + jnp.dot(pr.astype(BF16), v, preferred_element_type=F32)
        m_scr[h] = m_new

    @pl.when(p == pl.num_programs(1) - 1)
    def _():
        for h in range(N_KV):
            qf = q_ref[0, h * Q_PER_KV:(h + 1) * Q_PER_KV, :] * (HEAD_DIM ** -0.5)
            s_new = jnp.sum(qf * knew_ref[0, h:h + 1, :], axis=-1, keepdims=True)
            m_old = m_scr[h]
            m_new = jnp.maximum(m_old, s_new)
            alpha = jnp.exp(m_old - m_new)
            p_new = jnp.exp(s_new - m_new)
            den = alpha * l_scr[h] + p_new
            acc = alpha * acc_scr[h] + p_new * vnew_ref[0, h:h + 1, :]
            o_ref[0, h] = acc / jnp.maximum(den, TINY)


def nsa_decode_selected(q, sel, cache, page_table, k_new, v_new):
    n, npages = page_table.shape
    page = cache.shape[1]
    nsp = sel.shape[3]
    gp = PAGES_PER_STEP
    kern = functools.partial(_decode_slc_kernel, page=page)
    page_specs = [pl.BlockSpec((None, page, 2, N_KV, HEAD_DIM),
                               lambda i, p, pt, j=j: (pt[i, p * gp + j], 0, 1, 0, 0)) for j in range(gp)]
    grid_spec = pltpu.PrefetchScalarGridSpec(
        num_scalar_prefetch=1,
        grid=(n, npages // gp),
        in_specs=[
            pl.BlockSpec((1, N_HEADS, HEAD_DIM), lambda i, p, pt: (i, 0, 0)),
            pl.BlockSpec((1, N_KV, 1, nsp), lambda i, p, pt: (i, 0, 0, 0)),
        ] + page_specs + [
            pl.BlockSpec((1, N_KV, HEAD_DIM), lambda i, p, pt: (i, 0, 0)),
            pl.BlockSpec((1, N_KV, HEAD_DIM), lambda i, p, pt: (i, 0, 0)),
        ],
        out_specs=pl.BlockSpec((1, N_KV, Q_PER_KV, HEAD_DIM), lambda i, p, pt: (i, 0, 0, 0)),
        scratch_shapes=[pltpu.VMEM((N_KV, Q_PER_KV, LANES), F32), pltpu.VMEM((N_KV, Q_PER_KV, LANES), F32),
                        pltpu.VMEM((N_KV, Q_PER_KV, HEAD_DIM), F32)],
    )
    return pl.pallas_call(
        kern,
        grid_spec=grid_spec,
        out_shape=jax.ShapeDtypeStruct((n, N_KV, Q_PER_KV, HEAD_DIM), F32),
        compiler_params=pltpu.CompilerParams(dimension_semantics=("parallel", "arbitrary"),
                                             vmem_limit_bytes=V7X_VMEM_LIMIT_BYTES),
        name="nsa_decode_selected",
    )(page_table, q, sel, *([cache] * gp), k_new, v_new)


def _decode_combine_kernel(q_ref, win_ref, kwn_ref, vwn_ref, ocmp_ref, oslc_ref, gl_ref, bg_ref, z_ref, o_ref):
    nt_dims = (((1,), (1,)), ((), ()))
    for h in range(N_KV):
        qf = q_ref[0, h * Q_PER_KV:(h + 1) * Q_PER_KV, :] * (HEAD_DIM ** -0.5)
        k = win_ref[0, :, 0, h, :].astype(BF16)
        v = win_ref[0, :, 1, h, :].astype(BF16)
        s = lax.dot_general(qf.astype(BF16), k, nt_dims, preferred_element_type=F32)
        s_new = jnp.sum(qf * kwn_ref[0, h:h + 1, :], axis=-1, keepdims=True)
        m = jnp.maximum(jnp.max(s, axis=-1, keepdims=True), s_new)
        pr = jnp.exp(s - m)
        p_new = jnp.exp(s_new - m)
        den = jnp.maximum(jnp.sum(pr, axis=-1, keepdims=True) + p_new, TINY)
        o_win = (jnp.dot(pr.astype(BF16), v, preferred_element_type=F32) + p_new * vwn_ref[0, h:h + 1, :]) / den
        g_cmp = jax.nn.sigmoid(gl_ref[0, h, 0] + bg_ref[h, 0])
        g_slc = jax.nn.sigmoid(gl_ref[0, h, 1] + bg_ref[h, 1])
        g_win = jax.nn.sigmoid(gl_ref[0, h, 2] + bg_ref[h, 2])
        o = g_cmp * ocmp_ref[0, h] + g_slc * oslc_ref[0, h] + g_win * o_win
        rows = slice(h * Q_PER_KV, (h + 1) * Q_PER_KV)
        o_ref[0, rows, :] = o * jax.nn.silu(z_ref[0, rows, :])


def nsa_decode_combine(q, win_cache, kw_new, vw_new, o_cmp, o_slc, gl, b_gate, z):
    n, lb = win_cache.shape[:2]
    glt = gl.reshape(n, N_KV, Q_PER_KV, 3).transpose(0, 1, 3, 2)[..., None]
    bgt = b_gate.astype(F32).reshape(N_KV, Q_PER_KV, 3).transpose(0, 2, 1)[..., None]
    return pl.pallas_call(
        _decode_combine_kernel,
        grid=(n,),
        in_specs=[
            pl.BlockSpec((1, N_HEADS, HEAD_DIM), lambda i: (i, 0, 0)),
            pl.BlockSpec((1, lb, 2, N_KV, HEAD_DIM), lambda i: (i, 0, 0, 0, 0)),
            pl.BlockSpec((1, N_KV, HEAD_DIM), lambda i: (i, 0, 0)),
            pl.BlockSpec((1, N_KV, HEAD_DIM), lambda i: (i, 0, 0)),
            pl.BlockSpec((1, N_KV, Q_PER_KV, HEAD_DIM), lambda i: (i, 0, 0, 0)),
            pl.BlockSpec((1, N_KV, Q_PER_KV, HEAD_DIM), lambda i: (i, 0, 0, 0)),
            pl.BlockSpec((1, N_KV, 3, Q_PER_KV, 1), lambda i: (i, 0, 0, 0, 0)),
            pl.BlockSpec((N_KV, 3, Q_PER_KV, 1), lambda i: (0, 0, 0, 0)),
            pl.BlockSpec((1, N_HEADS, HEAD_DIM), lambda i: (i, 0, 0)),
        ],
        out_specs=pl.BlockSpec((1, N_HEADS, HEAD_DIM), lambda i: (i, 0, 0)),
        out_shape=jax.ShapeDtypeStruct((n, N_HEADS, HEAD_DIM), F32),
        compiler_params=pltpu.CompilerParams(dimension_semantics=("parallel",),
                                             vmem_limit_bytes=V7X_VMEM_LIMIT_BYTES),
        name="nsa_decode_combine",
    )(q, win_cache, kw_new, vw_new, o_cmp, o_slc, glt, bgt, z)


def nsa_decode(pj, cache_kv, page_table, cache_win, b_gate, cmp_pe, cmp_w1, cmp_b1, cmp_w2, cmp_b2):
    n = pj.shape[0]
    pool, page = cache_kv.shape[:2]
    past = page_table.shape[1] * page
    lb = cache_win.shape[1]
    assert lb <= WINDOW and lb <= past and page % SLC_BLOCK == 0
    cache = cache_kv
    win = cache_win
    q = pj[:, :NSA_Q_COLS].reshape(n, N_HEADS, HEAD_DIM)
    kv_new = pj[:, COL_KV * LANES:COL_WIN * LANES].reshape(n, 4, N_KV, HEAD_DIM)
    win_new = pj[:, COL_WIN * LANES:COL_Z * LANES].reshape(n, 2, N_KV, HEAD_DIM)
    z = pj[:, COL_Z * LANES:COL_GATE * LANES].reshape(n, N_HEADS, HEAD_DIM)
    gl = pj[:, COL_GATE * LANES:].reshape(n, N_KV, LANES)[..., :3 * Q_PER_KV]
    kvc = nsa_compress_paged(cache, page_table, cmp_pe, cmp_w1, cmp_b1, cmp_w2, cmp_b2)
    o_cmp, sel = nsa_decode_select(q, kvc, past)
    o_slc = nsa_decode_selected(q, sel, cache, page_table, kv_new[:, 2], kv_new[:, 3])
    o = nsa_decode_combine(q, win, win_new[:, 0], win_new[:, 1], o_cmp, o_slc, gl, b_gate, z)
    return o.reshape(n, NSA_Q_COLS)


def nsa_layer(x, kv_cache, win_past, page_table, w_in, b_gate, cmp_pe, cmp_w1, cmp_b1, cmp_w2, cmp_b2, w_out):
    n, t, _ = x.shape
    pj = proj(x, w_in)
    kv_new = pj[..., COL_KV * LANES:COL_WIN * LANES].reshape(n, t, 4, N_KV, HEAD_DIM)
    win_new = pj[..., COL_WIN * LANES:COL_Z * LANES].reshape(n, t, 2, N_KV, HEAD_DIM)
    if kv_cache is None:
        kvc = nsa_compress_prompt(pj, cmp_pe, cmp_w1, cmp_b1, cmp_w2, cmp_b2)
        o = nsa_prompt_attention(pj, kvc, b_gate)
        win_state = win_new[:, -min(WINDOW, t):]
    else:
        assert t == 1
        o = nsa_decode(pj[:, 0], kv_cache, page_table, win_past, b_gate, cmp_pe, cmp_w1, cmp_b1, cmp_w2, cmp_b2)
        o = o.astype(BF16)
        keys = jnp.concatenate([win_past.astype(win_new.dtype), win_new], axis=1)
        win_state = keys[:, -min(WINDOW, win_past.shape[1] + t):]
    y = matmul(o.reshape(n * t, NSA_Q_COLS), w_out).reshape(n, t, -1)
    return y, (kv_new, win_state)


def _nsa_w_in_layout(w_in):
    nsa_gate = 3 * N_HEADS
    a = COL_Z * LANES
    d = w_in.shape[0]
    w_gate = w_in[:, a:a + nsa_gate].reshape(d, N_KV, 3 * Q_PER_KV)
    w_gate = jnp.pad(w_gate, ((0, 0), (0, 0), (0, LANES - 3 * Q_PER_KV))).reshape(d, N_KV * LANES)
    return jnp.concatenate([w_in[:, :a], w_in[:, a + nsa_gate:], w_gate], axis=1)


def kernel(x_prompt, x_sample, state_l0_ssm, state_l1_conv, state_l1_rnn, cache_l2_kv, cache_l2_win, state_l3_ssm, page_table, l0_norm_pre, l0_norm_post, l0_w_in, l0_a_re, l0_a_im, l0_log_dt, l0_b, l0_c, l0_d, l0_w_glu, l0_b_glu, l0_w_out, l1_norm_pre, l1_norm_post, l1_w_in, l1_conv_w, l1_conv_b, l1_w_rg, l1_b_rg, l1_w_ig, l1_b_ig, l1_lam, l1_w_out, l2_norm_pre, l2_norm_post, l2_w_in, l2_b_gate, l2_cmp_pe, l2_cmp_w1, l2_cmp_b1, l2_cmp_w2, l2_cmp_b2, l2_w_out, l3_norm_pre, l3_norm_post, l3_w_in, l3_a_re, l3_a_im, l3_log_dt, l3_b, l3_c, l3_d, l3_w_glu, l3_b_glu, l3_w_out):
    bp = x_prompt.shape[0]
    dt = x_prompt.dtype
    d_model = x_prompt.shape[-1]
    s5_groups = l0_a_re.shape[0]
    rg_width = l1_lam.shape[0]
    bf = lambda w: w.astype(BF16)
    l0_p = s5_params(l0_w_in, l0_a_re, l0_a_im, l0_log_dt, l0_b, l0_c, l0_d, l0_w_glu, l0_b_glu, l0_w_out)
    l3_p = s5_params(l3_w_in, l3_a_re, l3_a_im, l3_log_dt, l3_b, l3_c, l3_d, l3_w_glu, l3_b_glu, l3_w_out)
    l1_p = rg_params(l1_w_in, l1_conv_w, l1_conv_b, l1_w_rg, l1_b_rg, l1_w_ig, l1_b_ig, l1_lam, l1_w_out)
    l2_wi, l2_wo = bf(_nsa_w_in_layout(l2_w_in)), bf(l2_w_out)
    layers = (
        (l0_norm_pre, l0_norm_post, lambda h, st: s5_layer(h, st[0], l0_p)),
        (l1_norm_pre, l1_norm_post, lambda h, st: rglru_layer(h, st[0], st[1], l1_p)),
        (l2_norm_pre, l2_norm_post,
         lambda h, st: nsa_layer(h, st[0], st[1], page_table, l2_wi, l2_b_gate, l2_cmp_pe, l2_cmp_w1, l2_cmp_b1,
                                 l2_cmp_w2, l2_cmp_b2, l2_wo)),
        (l3_norm_pre, l3_norm_post, lambda h, st: s5_layer(h, st[0], l3_p)),
    )
    prompt_state = (
        (jnp.zeros((bp, 2, s5_groups, S5_STATE), dt),),
        (jnp.zeros((bp, RG_CONV - 1, rg_width), dt), jnp.zeros((bp, rg_width), dt)),
        (None, None),
        (jnp.zeros((bp, 2, s5_groups, S5_STATE), dt),),
    )
    sample_state = (
        (state_l0_ssm,),
        (state_l1_conv, state_l1_rnn),
        (cache_l2_kv, cache_l2_win),
        (state_l3_ssm,),
    )
    n_layers = len(layers)
    streams = []
    for h, states in ((x_prompt, prompt_state), (x_sample, sample_state)):
        n, t, _ = h.shape
        h2 = h.reshape(n * t, d_model)
        xn = rmsnorm(h2, layers[0][0], BF16)
        new_states = []
        for i in range(n_layers):
            _, g_post, mix = layers[i]
            y, s_new = mix(xn.reshape(n, t, d_model), states[i])
            g_next = layers[i + 1][0] if i + 1 < n_layers else None
            h2, xn = post_norm_residual(h2, y.reshape(n * t, d_model), g_post, g_next)
            new_states.append(s_new)
        streams.append((h2.reshape(n, t, d_model), new_states))
    (hp, new_p), (hs, new_s) = streams
    (l0_ssm_p,), (l1_conv_p, l1_rnn_p), (l2_kv_p, l2_win_p), (l3_ssm_p,) = new_p
    (l0_ssm_s,), (l1_conv_s, l1_rnn_s), (l2_kv_s, l2_win_s), (l3_ssm_s,) = new_s
    return (hp, hs, l0_ssm_p, l0_ssm_s, l1_conv_p, l1_rnn_p, l1_conv_s, l1_rnn_s,
            l2_kv_p, l2_win_p, l2_kv_s, l2_win_s, l3_ssm_p, l3_ssm_s)
```

```python
import functools
import math

import numpy as np
import jax
import jax.numpy as jnp
from jax import lax
from jax.experimental import pallas as pl
from jax.experimental.pallas import tpu as pltpu

F32 = jnp.float32
BF16 = jnp.bfloat16

EPS = 1e-6
NEG = -1e30
TINY = 1e-30

LANES = 128
SUBLANES = 8

S5_GROUP = 16
S5_STATE = 64
S5_TILE_GROUPS = SUBLANES * LANES // S5_STATE

RG_CONV = 4
RG_C = 8.0

N_HEADS = 32
HEAD_DIM = 128
N_KV = 4
Q_PER_KV = N_HEADS // N_KV
CMP_LEN = 32
CMP_STRIDE = 16
SLC_BLOCK = 64
SLC_TOP = 16
FORCE_SCORE = 1e3
WINDOW = 512

NSA_Q_COLS = N_HEADS * HEAD_DIM
COL_KV = NSA_Q_COLS // LANES
COL_WIN = COL_KV + 4 * N_KV
COL_Z = COL_WIN + 2 * N_KV
COL_GATE = COL_Z + N_HEADS
GROUP_COLS = Q_PER_KV * HEAD_DIM
KV_HALF_COLS = 2 * N_KV * HEAD_DIM
PAGES_PER_STEP = 8

V7X_VMEM_LIMIT_BYTES = 48 * 1024 * 1024


def _pick_tile(n, target, align):
    if n <= target:
        return n
    t = (target // align) * align
    while t >= align:
        if n % t == 0:
            return t
        t -= align
    return n


def _rmsnorm_kernel(x_ref, g_ref, o_ref):
    x = x_ref[...]
    ms = jnp.mean(x * x, axis=-1, keepdims=True)
    o_ref[...] = (x * lax.rsqrt(ms + EPS) * g_ref[...]).astype(o_ref.dtype)


def rmsnorm(x2d, g, out_dtype):
    m, d = x2d.shape
    tm = _pick_tile(m, 256, 8)
    return pl.pallas_call(
        _rmsnorm_kernel,
        grid=(m // tm,),
        in_specs=[pl.BlockSpec((tm, d), lambda i: (i, 0)),
                  pl.BlockSpec((1, d), lambda i: (0, 0))],
        out_specs=pl.BlockSpec((tm, d), lambda i: (i, 0)),
        out_shape=jax.ShapeDtypeStruct((m, d), out_dtype),
        compiler_params=pltpu.CompilerParams(dimension_semantics=("parallel",),
                                             vmem_limit_bytes=V7X_VMEM_LIMIT_BYTES),
        name="rmsnorm",
    )(x2d, g.reshape(1, d))


def _post_norm_residual_kernel(h_ref, y_ref, g_ref, o_ref):
    y = y_ref[...]
    ms = jnp.mean(y * y, axis=-1, keepdims=True)
    o_ref[...] = h_ref[...] + y * lax.rsqrt(ms + EPS) * g_ref[...]


def _post_norm_residual_pre_kernel(h_ref, y_ref, g_ref, gn_ref, o_ref, xn_ref):
    y = y_ref[...]
    ms = jnp.mean(y * y, axis=-1, keepdims=True)
    h = h_ref[...] + y * lax.rsqrt(ms + EPS) * g_ref[...]
    o_ref[...] = h
    ms_h = jnp.mean(h * h, axis=-1, keepdims=True)
    xn_ref[...] = (h * lax.rsqrt(ms_h + EPS) * gn_ref[...]).astype(xn_ref.dtype)


def post_norm_residual(h2d, y2d, g, g_next=None):
    m, d = h2d.shape
    tm = _pick_tile(m, 256, 8)
    row = pl.BlockSpec((tm, d), lambda i: (i, 0))
    gain = pl.BlockSpec((1, d), lambda i: (0, 0))
    params = pltpu.CompilerParams(dimension_semantics=("parallel",), vmem_limit_bytes=V7X_VMEM_LIMIT_BYTES)
    if g_next is None:
        h_new = pl.pallas_call(
            _post_norm_residual_kernel,
            grid=(m // tm,),
            in_specs=[row, row, gain],
            out_specs=row,
            out_shape=jax.ShapeDtypeStruct((m, d), F32),
            compiler_params=params,
            name="post_norm_residual",
        )(h2d, y2d, g.reshape(1, d))
        return h_new, None
    return pl.pallas_call(
        _post_norm_residual_pre_kernel,
        grid=(m // tm,),
        in_specs=[row, row, gain, gain],
        out_specs=[row, row],
        out_shape=[jax.ShapeDtypeStruct((m, d), F32), jax.ShapeDtypeStruct((m, d), BF16)],
        compiler_params=params,
        name="post_norm_residual_pre",
    )(h2d, y2d, g.reshape(1, d), g_next.reshape(1, d))


def _matmul_single_k_kernel(a_ref, b_ref, o_ref):
    o_ref[...] = jnp.dot(a_ref[...], b_ref[...], preferred_element_type=F32).astype(o_ref.dtype)


def _matmul_kernel(a_ref, b_ref, o_ref, acc_ref):
    k = pl.program_id(2)

    @pl.when(k == 0)
    def _():
        acc_ref[...] = jnp.zeros_like(acc_ref)

    acc_ref[...] += jnp.dot(a_ref[...], b_ref[...], preferred_element_type=F32)

    @pl.when(k == pl.num_programs(2) - 1)
    def _():
        o_ref[...] = acc_ref[...].astype(o_ref.dtype)


def matmul(a, b, out_dtype=F32):
    m, k = a.shape
    _, n = b.shape
    tm = _pick_tile(m, 1024, 8)
    tn = _pick_tile(n, 512, 128)
    tk = _pick_tile(k, 4096, 128)
    if tk == k:
        wide = _pick_tile(n, 2 * tn, 128)
        if 2 * (tm * k * 2 + k * wide * 2 + tm * wide * 4) <= V7X_VMEM_LIMIT_BYTES - 4 * 1024 * 1024:
            tn = wide
        return pl.pallas_call(
            _matmul_single_k_kernel,
            grid=(m // tm, n // tn),
            in_specs=[pl.BlockSpec((tm, k), lambda i, j: (i, 0)),
                      pl.BlockSpec((k, tn), lambda i, j: (0, j))],
            out_specs=pl.BlockSpec((tm, tn), lambda i, j: (i, j)),
            out_shape=jax.ShapeDtypeStruct((m, n), out_dtype),
            compiler_params=pltpu.CompilerParams(
                dimension_semantics=("parallel", "parallel"),
                vmem_limit_bytes=V7X_VMEM_LIMIT_BYTES),
            name="matmul",
        )(a, b)
    return pl.pallas_call(
        _matmul_kernel,
        grid=(m // tm, n // tn, k // tk),
        in_specs=[pl.BlockSpec((tm, tk), lambda i, j, kk: (i, kk)),
                  pl.BlockSpec((tk, tn), lambda i, j, kk: (kk, j))],
        out_specs=pl.BlockSpec((tm, tn), lambda i, j, kk: (i, j)),
        out_shape=jax.ShapeDtypeStruct((m, n), out_dtype),
        scratch_shapes=[pltpu.VMEM((tm, tn), F32)],
        compiler_params=pltpu.CompilerParams(
            dimension_semantics=("parallel", "parallel", "arbitrary"),
            vmem_limit_bytes=V7X_VMEM_LIMIT_BYTES),
        name="matmul",
    )(a, b)


def proj(x, w_bf16):
    n, t, k = x.shape
    return matmul(x.reshape(n * t, k).astype(BF16), w_bf16).reshape(n, t, -1)


def s5_discretize(a_re, a_im, log_dt, b, c):
    dt = jnp.exp(log_dt.astype(F32))[:, None]
    ar, ai = a_re.astype(F32), a_im.astype(F32)
    mag = jnp.exp(ar * dt)
    abar_r, abar_i = mag * jnp.cos(ai * dt), mag * jnp.sin(ai * dt)
    den = ar * ar + ai * ai
    coef_r = ((abar_r - 1.0) * ar + abar_i * ai) / den
    coef_i = (abar_i * ar - (abar_r - 1.0) * ai) / den
    b_r, b_i = b[0].astype(F32), b[1].astype(F32)
    bbar_r = coef_r[..., None] * b_r - coef_i[..., None] * b_i
    bbar_i = coef_r[..., None] * b_i + coef_i[..., None] * b_r
    return abar_r, abar_i, bbar_r, bbar_i, c[0].astype(F32), c[1].astype(F32)


def s5_tile_weights(bbar_r, bbar_i, c_r, c_i):
    g = bbar_r.shape[0]
    tg = S5_TILE_GROUPS
    nt = g // tg
    eye = jnp.eye(tg, dtype=bool)[None, :, None, :, None]

    def expand(x):
        xt = x.transpose(0, 1, 3, 2)[:, :, :, None, :]
        full = jnp.where(eye, xt, 0.0)
        return full.reshape(nt, tg * x.shape[3], tg * x.shape[2])

    br = expand(bbar_r.reshape(nt, tg, S5_STATE, S5_GROUP))
    bi = expand(bbar_i.reshape(nt, tg, S5_STATE, S5_GROUP))
    cr = expand(c_r.reshape(nt, tg, S5_GROUP, S5_STATE))
    ci = expand(c_i.reshape(nt, tg, S5_GROUP, S5_STATE))
    bcat = jnp.concatenate([br, bi], axis=-1).astype(BF16)
    ccat = jnp.concatenate([cr, -ci], axis=1).astype(BF16)
    return bcat, ccat


def _s5_prompt_kernel(u_ref, bcat_ref, ccat_ref, ar_ref, ai_ref, d_ref, h0_ref,
                      g_ref, hlast_ref, *scr, tc, tp, tg):
    bur_scrs, bui_scrs, h_scr = scr[:tg], scr[tg:2 * tg], scr[2 * tg]
    c = pl.program_id(2)
    sw = S5_TILE_GROUPS * S5_STATE
    cw = S5_TILE_GROUPS * S5_GROUP

    @pl.when(c == 0)
    def _():
        h_scr[...] = h0_ref[0]

    u = u_ref[0]
    ub = u.astype(BF16)
    for k in range(tg):
        bu = jnp.dot(ub[:, k * cw:(k + 1) * cw], bcat_ref[k], preferred_element_type=F32)
        for s in range(SUBLANES):
            bur_scrs[k][pl.ds(s * tp, tc), :] = bu[:, s * LANES:(s + 1) * LANES]
            bui_scrs[k][pl.ds(s * tp, tc), :] = bu[:, sw + s * LANES:sw + (s + 1) * LANES]

    ar = [ar_ref[k] for k in range(tg)]
    ai = [ai_ref[k] for k in range(tg)]

    def step(t, carry):
        new = []
        for k in range(tg):
            hr, hi = carry[2 * k], carry[2 * k + 1]
            br = bur_scrs[k][pl.ds(t, SUBLANES, stride=tp), :]
            bi = bui_scrs[k][pl.ds(t, SUBLANES, stride=tp), :]
            hr2 = ar[k] * hr - ai[k] * hi + br
            hi2 = ar[k] * hi + ai[k] * hr + bi
            bur_scrs[k][pl.ds(t, SUBLANES, stride=tp), :] = hr2
            bui_scrs[k][pl.ds(t, SUBLANES, stride=tp), :] = hi2
            new += [hr2, hi2]
        return tuple(new)

    init = tuple(h_scr[p, k] for k in range(tg) for p in (0, 1))
    fin = lax.fori_loop(0, tc, step, init, unroll=4)
    for k in range(tg):
        h_scr[0, k] = fin[2 * k]
        h_scr[1, k] = fin[2 * k + 1]

    for k in range(tg):
        pieces = [bur_scrs[k][pl.ds(s * tp, tc), :].astype(BF16) for s in range(SUBLANES)]
        pieces += [bui_scrs[k][pl.ds(s * tp, tc), :].astype(BF16) for s in range(SUBLANES)]
        hcat = jnp.concatenate(pieces, axis=-1)
        y = jnp.dot(hcat, ccat_ref[k], preferred_element_type=F32)
        yk = y + d_ref[:, k * cw:(k + 1) * cw] * u[:, k * cw:(k + 1) * cw]
        g_ref[0, :, k * cw:(k + 1) * cw] = jax.nn.gelu(yk).astype(g_ref.dtype)

    @pl.when(c == pl.num_programs(2) - 1)
    def _():
        hlast_ref[0] = h_scr[...]


def s5_prompt(uz, h0, abar_r, abar_i, bcat, ccat, d, *, tc=512, tg=4):
    n, t, w2 = uz.shape
    w = w2 // 2
    g = w // S5_GROUP
    nt = g // S5_TILE_GROUPS
    tg = min(tg, nt)
    tc = min(tc, t)
    tp = tc + SUBLANES
    cw = S5_TILE_GROUPS * S5_GROUP
    sw = S5_TILE_GROUPS * S5_STATE
    h0t = h0.reshape(n, 2, nt, SUBLANES, LANES)
    art = abar_r.reshape(nt, SUBLANES, LANES)
    ait = abar_i.reshape(nt, SUBLANES, LANES)
    kern = functools.partial(_s5_prompt_kernel, tc=tc, tp=tp, tg=tg)
    gout, hlast = pl.pallas_call(
        kern,
        grid=(nt // tg, n, t // tc),
        in_specs=[
            pl.BlockSpec((1, tc, tg * cw), lambda j, i, c: (i, c, j)),
            pl.BlockSpec((tg, cw, 2 * sw), lambda j, i, c: (j, 0, 0)),
            pl.BlockSpec((tg, 2 * sw, cw), lambda j, i, c: (j, 0, 0)),
            pl.BlockSpec((tg, SUBLANES, LANES), lambda j, i, c: (j, 0, 0)),
            pl.BlockSpec((tg, SUBLANES, LANES), lambda j, i, c: (j, 0, 0)),
            pl.BlockSpec((1, tg * cw), lambda j, i, c: (0, j)),
            pl.BlockSpec((1, 2, tg, SUBLANES, LANES), lambda j, i, c: (i, 0, j, 0, 0)),
        ],
        out_specs=[
            pl.BlockSpec((1, tc, tg * cw), lambda j, i, c: (i, c, j)),
            pl.BlockSpec((1, 2, tg, SUBLANES, LANES), lambda j, i, c: (i, 0, j, 0, 0)),
        ],
        out_shape=[jax.ShapeDtypeStruct((n, t, w), BF16),
                   jax.ShapeDtypeStruct((n, 2, nt, SUBLANES, LANES), F32)],
        scratch_shapes=[pltpu.VMEM((SUBLANES * tp, LANES), F32)] * (2 * tg)
        + [pltpu.VMEM((2, tg, SUBLANES, LANES), F32)],
        compiler_params=pltpu.CompilerParams(
            dimension_semantics=("parallel", "parallel", "arbitrary"),
            vmem_limit_bytes=V7X_VMEM_LIMIT_BYTES),
        name="s5_prompt",
    )(uz, bcat, ccat, art, ait, d.reshape(1, w), h0t)
    return gout, hlast.reshape(n, 2, g, S5_STATE)


def _s5_step_kernel(u_ref, bcat_ref, ccat_ref, ar_ref, ai_ref, d_ref, h0_ref, g_ref, h_ref):
    sw = S5_TILE_GROUPS * S5_STATE
    u = u_ref[...]
    bu = jnp.dot(u.astype(BF16), bcat_ref[0], preferred_element_type=F32)
    hr, hi = h0_ref[:, 0, :], h0_ref[:, 1, :]
    ar, ai = ar_ref[0], ai_ref[0]
    hr2 = ar * hr - ai * hi + bu[:, :sw]
    hi2 = ar * hi + ai * hr + bu[:, sw:]
    h_ref[:, 0, :] = hr2
    h_ref[:, 1, :] = hi2
    hcat = jnp.concatenate([hr2.astype(BF16), hi2.astype(BF16)], axis=-1)
    y = jnp.dot(hcat, ccat_ref[0], preferred_element_type=F32)
    g_ref[...] = jax.nn.gelu(y + d_ref[...] * u).astype(g_ref.dtype)


def s5_step(uz, h0, abar_r, abar_i, bcat, ccat, d):
    n, w2 = uz.shape
    w = w2 // 2
    g = w // S5_GROUP
    nt = g // S5_TILE_GROUPS
    cw = S5_TILE_GROUPS * S5_GROUP
    sw = S5_TILE_GROUPS * S5_STATE
    gout, hnew = pl.pallas_call(
        _s5_step_kernel,
        grid=(nt,),
        in_specs=[
            pl.BlockSpec((n, cw), lambda j: (0, j)),
            pl.BlockSpec((1, cw, 2 * sw), lambda j: (j, 0, 0)),
            pl.BlockSpec((1, 2 * sw, cw), lambda j: (j, 0, 0)),
            pl.BlockSpec((1, 1, sw), lambda j: (j, 0, 0)),
            pl.BlockSpec((1, 1, sw), lambda j: (j, 0, 0)),
            pl.BlockSpec((1, cw), lambda j: (0, j)),
            pl.BlockSpec((n, 2, sw), lambda j: (0, 0, j)),
        ],
        out_specs=[pl.BlockSpec((n, cw), lambda j: (0, j)),
                   pl.BlockSpec((n, 2, sw), lambda j: (0, 0, j))],
        out_shape=[jax.ShapeDtypeStruct((n, w), BF16),
                   jax.ShapeDtypeStruct((n, 2, g * S5_STATE), F32)],
        compiler_params=pltpu.CompilerParams(dimension_semantics=("parallel",),
                                             vmem_limit_bytes=V7X_VMEM_LIMIT_BYTES),
        name="s5_step",
    )(uz, bcat, ccat, abar_r.reshape(nt, 1, sw), abar_i.reshape(nt, 1, sw), d.reshape(1, w),
      h0.reshape(n, 2, g * S5_STATE))
    return gout, hnew.reshape(n, 2, g, S5_STATE)


def s5_params(w_in, a_re, a_im, log_dt, b, c, d, w_glu, b_glu, w_out):
    abar_r, abar_i, bbar_r, bbar_i, c_r, c_i = s5_discretize(a_re, a_im, log_dt, b, c)
    bcat, ccat = s5_tile_weights(bbar_r, bbar_i, c_r, c_i)
    return dict(w_in=w_in.astype(BF16), abar_r=abar_r, abar_i=abar_i, bcat=bcat, ccat=ccat,
                d=d.astype(F32), w_glu=w_glu.astype(BF16), b_glu=b_glu, w_out=w_out.astype(BF16))


def _matmul_glu_kernel(a_ref, b_ref, bias_ref, g_ref, z_ref, o_ref, acc_ref):
    k = pl.program_id(2)

    @pl.when(k == 0)
    def _():
        acc_ref[...] = jnp.zeros_like(acc_ref)

    acc_ref[...] += jnp.dot(a_ref[...], b_ref[...], preferred_element_type=F32)

    @pl.when(k == pl.num_programs(2) - 1)
    def _():
        gate = jax.nn.sigmoid(acc_ref[...] + bias_ref[...])
        o_ref[...] = (g_ref[...].astype(F32) * gate * jax.nn.silu(z_ref[...])).astype(o_ref.dtype)


def matmul_glu(g, w, bias, uz):
    m, k = g.shape
    _, n = w.shape
    tm = _pick_tile(m, 1024, 8)
    tn = _pick_tile(n, 512, LANES)
    tk = _pick_tile(k, 4096, LANES)
    z_off = n // tn
    return pl.pallas_call(
        _matmul_glu_kernel,
        grid=(m // tm, n // tn, k // tk),
        in_specs=[pl.BlockSpec((tm, tk), lambda i, j, kk: (i, kk)),
                  pl.BlockSpec((tk, tn), lambda i, j, kk: (kk, j)),
                  pl.BlockSpec((1, tn), lambda i, j, kk: (0, j)),
                  pl.BlockSpec((tm, tn), lambda i, j, kk: (i, j)),
                  pl.BlockSpec((tm, tn), lambda i, j, kk: (i, z_off + j))],
        out_specs=pl.BlockSpec((tm, tn), lambda i, j, kk: (i, j)),
        out_shape=jax.ShapeDtypeStruct((m, n), BF16),
        scratch_shapes=[pltpu.VMEM((tm, tn), F32)],
        compiler_params=pltpu.CompilerParams(
            dimension_semantics=("parallel", "parallel", "arbitrary"),
            vmem_limit_bytes=V7X_VMEM_LIMIT_BYTES),
        name="matmul_glu",
    )(g, w, bias.astype(F32).reshape(1, n), g, uz)


def s5_layer(x, h0, p):
    n, t, _ = x.shape
    uz = proj(x, p["w_in"])
    w = uz.shape[-1] // 2
    if t == 1:
        g, h = s5_step(uz[:, 0], h0, p["abar_r"], p["abar_i"], p["bcat"], p["ccat"], p["d"])
    else:
        g, h = s5_prompt(uz, h0, p["abar_r"], p["abar_i"], p["bcat"], p["ccat"], p["d"])
    gz = matmul_glu(g.reshape(n * t, w), p["w_glu"], p["b_glu"], uz.reshape(n * t, 2 * w))
    return matmul(gz, p["w_out"]).reshape(n, t, -1), (h,)


def _round_up(x, m):
    return -(-x // m) * m


def rg_pad_cols(x, nb, bp):
    lead = x.shape[:-1]
    blk = x.shape[-1] // nb
    x = x.reshape(lead + (nb, blk))
    x = jnp.pad(x, [(0, 0)] * len(lead) + [(0, 0), (0, bp - blk)])
    return x.reshape(lead + (nb * bp,))


def rg_unpad_cols(x, nb, blk):
    lead = x.shape[:-1]
    bp = x.shape[-1] // nb
    return x.reshape(lead + (nb, bp))[..., :blk].reshape(lead + (nb * blk,))


def rg_params(w_in, conv_w, conv_b, w_rg, b_rg, w_ig, b_ig, lam, w_out):
    nb, blk, _ = w_rg.shape
    bp = _round_up(blk, LANES)
    width = nb * blk
    w_in_p = jnp.concatenate([rg_pad_cols(w_in[:, :width], nb, bp), rg_pad_cols(w_in[:, width:], nb, bp)],
                             axis=1).astype(BF16)
    pad_sq = lambda w: jnp.pad(w, ((0, 0), (0, bp - blk), (0, bp - blk)))
    w_gate = jnp.concatenate([pad_sq(w_rg), pad_sq(w_ig)], axis=-1).astype(BF16)
    b_gate = jnp.stack([rg_pad_cols(b_rg.astype(F32), nb, bp).reshape(nb, bp),
                        rg_pad_cols(b_ig.astype(F32), nb, bp).reshape(nb, bp)], axis=1)
    b_gate = b_gate.reshape(nb, 1, 2 * bp)
    sp = rg_pad_cols(jax.nn.softplus(-lam.astype(F32)), nb, bp).reshape(1, nb * bp)
    w_out_p = jnp.pad(w_out.reshape(nb, blk, -1), ((0, 0), (0, bp - blk), (0, 0))).reshape(nb * bp, -1)
    return dict(nb=nb, blk=blk, bp=bp, w_in=w_in_p, conv_w=rg_pad_cols(conv_w.astype(F32), nb, bp),
                conv_b=rg_pad_cols(conv_b.astype(F32), nb, bp).reshape(1, nb * bp),
                w_gate=w_gate, b_gate=b_gate, sp=sp, w_out=w_out_p.astype(BF16))


def _rg_gates(uc, wg, bg, sp, bp):
    gates = jnp.dot(uc.astype(BF16), wg, preferred_element_type=F32) + bg
    r = jax.nn.sigmoid(gates[:, :bp])
    ig = jax.nn.sigmoid(gates[:, bp:])
    log_a = -RG_C * r * sp
    a = jnp.exp(log_a)
    xin = jnp.sqrt(1.0 - a * a) * (ig * uc)
    return a, xin


def _rg_prompt_kernel(u_ref, z_ref, cw_ref, cb_ref, wg_ref, bg_ref, sp_ref, cbuf_ref, h0_ref,
                      o_ref, tail_ref, hlast_ref, ext_scr, h_scr, *slabs, tc, pitch):
    c = pl.program_id(2)
    bp = u_ref.shape[-1]
    nl = bp // LANES
    a_scrs, x_scrs = slabs[:nl], slabs[nl:]
    sub = tc // SUBLANES

    @pl.when(c == 0)
    def _():
        ext_scr[0:SUBLANES, :] = cbuf_ref[0]
        h_scr[...] = h0_ref[0]

    ext_scr[pl.ds(SUBLANES, tc), :] = u_ref[0]
    w = cw_ref[...]
    uc = (w[3:4] * ext_scr[pl.ds(SUBLANES, tc), :] + w[2:3] * ext_scr[pl.ds(SUBLANES - 1, tc), :]
          + w[1:2] * ext_scr[pl.ds(SUBLANES - 2, tc), :] + w[0:1] * ext_scr[pl.ds(SUBLANES - 3, tc), :]
          + cb_ref[...])
    ext_scr[0:SUBLANES, :] = ext_scr[pl.ds(tc, SUBLANES), :]

    a, xin = _rg_gates(uc, wg_ref[0], bg_ref[0], sp_ref[...], bp)
    for l in range(nl):
        for s in range(SUBLANES):
            a_scrs[l][pl.ds(s * pitch, sub), :] = a[s * sub:(s + 1) * sub, l * LANES:(l + 1) * LANES]
            x_scrs[l][pl.ds(s * pitch, sub), :] = xin[s * sub:(s + 1) * sub, l * LANES:(l + 1) * LANES]

    def step(j, carry):
        new = []
        for l in range(nl):
            p, s_ = carry[2 * l], carry[2 * l + 1]
            aj = a_scrs[l][pl.ds(j, SUBLANES, stride=pitch), :]
            xj = x_scrs[l][pl.ds(j, SUBLANES, stride=pitch), :]
            p = aj * p
            s_ = aj * s_ + xj
            a_scrs[l][pl.ds(j, SUBLANES, stride=pitch), :] = p
            x_scrs[l][pl.ds(j, SUBLANES, stride=pitch), :] = s_
            new += [p, s_]
        return tuple(new)

    init = tuple(jnp.ones((SUBLANES, LANES), F32) if i % 2 == 0 else jnp.zeros((SUBLANES, LANES), F32)
                 for i in range(2 * nl))
    fin = lax.fori_loop(0, sub, step, init)

    for l in range(nl):
        p_end, s_end = fin[2 * l], fin[2 * l + 1]
        h = h_scr[:, l * LANES:(l + 1) * LANES]
        for s in range(SUBLANES):
            rows = pl.ds(s * pitch, sub)
            hs = x_scrs[l][rows, :] + a_scrs[l][rows, :] * h
            zs = z_ref[0, s * sub:(s + 1) * sub, l * LANES:(l + 1) * LANES]
            o_ref[0, s * sub:(s + 1) * sub, l * LANES:(l + 1) * LANES] = (hs * jax.nn.silu(zs)).astype(o_ref.dtype)
            h = s_end[s:s + 1, :] + p_end[s:s + 1, :] * h
        h_scr[:, l * LANES:(l + 1) * LANES] = h

    @pl.when(c == pl.num_programs(2) - 1)
    def _():
        tail_ref[0] = ext_scr[0:SUBLANES, :]
        hlast_ref[0] = h_scr[...]


def rg_prompt(uz, conv_buf, h0, p, *, tc=512):
    n, t, _ = uz.shape
    nb, blk, bp = p["nb"], p["blk"], p["bp"]
    tc = min(tc, t)
    sub = tc // SUBLANES
    pitch = sub + SUBLANES
    nl = bp // LANES
    cbuf = jnp.pad(rg_pad_cols(conv_buf.astype(F32), nb, bp), ((0, 0), (SUBLANES - (RG_CONV - 1), 0), (0, 0)))
    h0p = rg_pad_cols(h0.astype(F32), nb, bp).reshape(n, 1, nb * bp)
    kern = functools.partial(_rg_prompt_kernel, tc=tc, pitch=pitch)
    o, tail, hlast = pl.pallas_call(
        kern,
        grid=(n, nb, t // tc),
        in_specs=[
            pl.BlockSpec((1, tc, bp), lambda i, b, c: (i, c, b)),
            pl.BlockSpec((1, tc, bp), lambda i, b, c: (i, c, nb + b)),
            pl.BlockSpec((RG_CONV, bp), lambda i, b, c: (0, b)),
            pl.BlockSpec((1, bp), lambda i, b, c: (0, b)),
            pl.BlockSpec((1, bp, 2 * bp), lambda i, b, c: (b, 0, 0)),
            pl.BlockSpec((1, 1, 2 * bp), lambda i, b, c: (b, 0, 0)),
            pl.BlockSpec((1, bp), lambda i, b, c: (0, b)),
            pl.BlockSpec((1, SUBLANES, bp), lambda i, b, c: (i, 0, b)),
            pl.BlockSpec((1, 1, bp), lambda i, b, c: (i, 0, b)),
        ],
        out_specs=[
            pl.BlockSpec((1, tc, bp), lambda i, b, c: (i, c, b)),
            pl.BlockSpec((1, SUBLANES, bp), lambda i, b, c: (i, 0, b)),
            pl.BlockSpec((1, 1, bp), lambda i, b, c: (i, 0, b)),
        ],
        out_shape=[jax.ShapeDtypeStruct((n, t, nb * bp), BF16),
                   jax.ShapeDtypeStruct((n, SUBLANES, nb * bp), F32),
                   jax.ShapeDtypeStruct((n, 1, nb * bp), F32)],
        scratch_shapes=[pltpu.VMEM((tc + SUBLANES, bp), F32), pltpu.VMEM((1, bp), F32)]
        + [pltpu.VMEM((SUBLANES * pitch, LANES), F32)] * (2 * nl),
        compiler_params=pltpu.CompilerParams(
            dimension_semantics=("parallel", "parallel", "arbitrary"),
            vmem_limit_bytes=V7X_VMEM_LIMIT_BYTES),
        name="rg_prompt",
    )(uz, uz, p["conv_w"], p["conv_b"], p["w_gate"], p["b_gate"], p["sp"], cbuf, h0p)
    conv_state = rg_unpad_cols(tail[:, SUBLANES - (RG_CONV - 1):], nb, blk)
    return o, conv_state, rg_unpad_cols(hlast[:, 0], nb, blk)


def _rg_step_kernel(u_ref, z_ref, cw_ref, cb_ref, wg_ref, bg_ref, sp_ref, cbuf_ref, h0_ref, o_ref, h_ref):
    bp = u_ref.shape[-1]
    w = cw_ref[...]
    uc = (w[3:4] * u_ref[...] + w[2:3] * cbuf_ref[2] + w[1:2] * cbuf_ref[1] + w[0:1] * cbuf_ref[0]
          + cb_ref[...])
    a, xin = _rg_gates(uc, wg_ref[0], bg_ref[0], sp_ref[...], bp)
    h = a * h0_ref[...] + xin
    h_ref[...] = h
    o_ref[...] = (h * jax.nn.silu(z_ref[...])).astype(o_ref.dtype)


def rg_step(uz, conv_buf, h0, p):
    n, _ = uz.shape
    nb, blk, bp = p["nb"], p["blk"], p["bp"]
    cbuf = rg_pad_cols(conv_buf.astype(F32), nb, bp).transpose(1, 0, 2)
    h0p = rg_pad_cols(h0.astype(F32), nb, bp)
    o, hnew = pl.pallas_call(
        _rg_step_kernel,
        grid=(nb,),
        in_specs=[
            pl.BlockSpec((n, bp), lambda b: (0, b)),
            pl.BlockSpec((n, bp), lambda b: (0, nb + b)),
            pl.BlockSpec((RG_CONV, bp), lambda b: (0, b)),
            pl.BlockSpec((1, bp), lambda b: (0, b)),
            pl.BlockSpec((1, bp, 2 * bp), lambda b: (b, 0, 0)),
            pl.BlockSpec((1, 1, 2 * bp), lambda b: (b, 0, 0)),
            pl.BlockSpec((1, bp), lambda b: (0, b)),
            pl.BlockSpec((RG_CONV - 1, n, bp), lambda b: (0, 0, b)),
            pl.BlockSpec((n, bp), lambda b: (0, b)),
        ],
        out_specs=[pl.BlockSpec((n, bp), lambda b: (0, b)), pl.BlockSpec((n, bp), lambda b: (0, b))],
        out_shape=[jax.ShapeDtypeStruct((n, nb * bp), BF16), jax.ShapeDtypeStruct((n, nb * bp), F32)],
        compiler_params=pltpu.CompilerParams(dimension_semantics=("parallel",),
                                             vmem_limit_bytes=V7X_VMEM_LIMIT_BYTES),
        name="rg_step",
    )(uz, uz, p["conv_w"], p["conv_b"], p["w_gate"], p["b_gate"], p["sp"], cbuf, h0p)
    u_new = rg_unpad_cols(uz[:, :nb * bp], nb, blk)
    conv_state = jnp.concatenate([conv_buf[:, 1:].astype(F32), u_new[:, None]], axis=1)
    return o, conv_state, rg_unpad_cols(hnew, nb, blk)


def rglru_layer(x, conv_buf, h0, p):
    n, t, _ = x.shape
    uz = proj(x, p["w_in"])
    if t == 1:
        o, conv_state, h = rg_step(uz[:, 0], conv_buf, h0, p)
    else:
        o, conv_state, h = rg_prompt(uz, conv_buf, h0, p)
    y = matmul(o.reshape(n * t, -1), p["w_out"]).reshape(n, t, -1)
    return y, (conv_state, h)


def _masked_softmax_rows(s, ok):
    s = jnp.where(ok, s, NEG)
    mx = jnp.max(s, axis=-1, keepdims=True)
    p = jnp.where(ok, jnp.exp(s - mx), 0.0)
    den = jnp.maximum(jnp.sum(p, axis=-1, keepdims=True), TINY)
    return p / den


def _compress_kernel(x_ref, pe_ref, w1_ref, b1_ref, w2_ref, b2_ref, o_ref, *, nh):
    half = CMP_LEN // 2
    acc_lo = jnp.zeros((nh, w1_ref.shape[-1]), F32)
    acc_hi = jnp.zeros((nh, w1_ref.shape[-1]), F32)
    for l in range(half):
        x = x_ref[pl.ds(l, nh, stride=CMP_STRIDE), :]
        acc_lo += jnp.dot((x + pe_ref[0, l:l + 1, :]).astype(BF16), w1_ref[0, l],
                          preferred_element_type=F32)
        acc_hi += jnp.dot((x + pe_ref[0, half + l:half + l + 1, :]).astype(BF16), w1_ref[0, half + l],
                          preferred_element_type=F32)
    hid = jax.nn.gelu(acc_lo + pltpu.roll(acc_hi, nh - 1, axis=0) + b1_ref[0])
    out = jnp.dot(hid.astype(BF16), w2_ref[0], preferred_element_type=F32) + b2_ref[0]
    row = lax.broadcasted_iota(jnp.int32, out.shape, 0)
    o_ref[0, 0, 0] = jnp.where(row < nh - 1, out, 0.0)


def nsa_compress_prompt(pj, cmp_pe, cmp_w1, cmp_b1, cmp_w2, cmp_b2):
    n, t, _ = pj.shape
    nh = t // CMP_STRIDE
    hidden = cmp_w1.shape[-1]
    kern = functools.partial(_compress_kernel, nh=nh)
    return pl.pallas_call(
        kern,
        grid=(n, 2, N_KV),
        in_specs=[
            pl.BlockSpec((None, t, LANES), lambda i, ty, h: (i, 0, COL_KV + ty * N_KV + h)),
            pl.BlockSpec((1, CMP_LEN, HEAD_DIM), lambda i, ty, h: (ty, 0, 0)),
            pl.BlockSpec((1, CMP_LEN, HEAD_DIM, hidden), lambda i, ty, h: (ty, 0, 0, 0)),
            pl.BlockSpec((1, 1, hidden), lambda i, ty, h: (ty, 0, 0)),
            pl.BlockSpec((1, hidden, HEAD_DIM), lambda i, ty, h: (ty, 0, 0)),
            pl.BlockSpec((1, 1, HEAD_DIM), lambda i, ty, h: (ty, 0, 0)),
        ],
        out_specs=pl.BlockSpec((1, 1, 1, nh, HEAD_DIM), lambda i, ty, h: (i, ty, h, 0, 0)),
        out_shape=jax.ShapeDtypeStruct((n, 2, N_KV, nh, HEAD_DIM), F32),
        compiler_params=pltpu.CompilerParams(
            dimension_semantics=("parallel", "parallel", "parallel"),
            vmem_limit_bytes=V7X_VMEM_LIMIT_BYTES),
        name="nsa_compress_prompt",
    )(pj, cmp_pe, cmp_w1.astype(BF16), cmp_b1.reshape(2, 1, hidden), cmp_w2.astype(BF16),
      cmp_b2.reshape(2, 1, HEAD_DIM))


def _nsa_prompt_kernel(q_ref, kc_ref, vc_ref, ks_ref, vs_ref, kw_ref, vw_ref, z_ref, gl_ref, bg_ref,
                       ovl_ref, exp_ref, o_ref, m_scr, l_scr, acc_scr, *, bq, tk, t, n_sel, n_top, wlen):
    qb = pl.program_id(2)
    q = q_ref[0] * (HEAD_DIM ** -0.5)
    qs = jnp.concatenate([q[:, g * HEAD_DIM:(g + 1) * HEAD_DIM] for g in range(Q_PER_KV)],
                         axis=0).astype(BF16)
    nt_dims = (((1,), (1,)), ((), ()))

    def per_head(x):
        return jnp.concatenate([x] * Q_PER_KV, axis=0)

    qpos = qb * bq + lax.broadcasted_iota(jnp.int32, (bq, LANES), 0)
    lane = lax.broadcasted_iota(jnp.int32, (bq, LANES), 1)

    ncp = kc_ref.shape[3]
    s = lax.dot_general(qs, kc_ref[0, 0, 0].astype(BF16), nt_dims, preferred_element_type=F32)
    cpos = lax.broadcasted_iota(jnp.int32, (bq, ncp), 1) * CMP_STRIDE + (CMP_LEN - 1)
    qpos_c = qb * bq + lax.broadcasted_iota(jnp.int32, (bq, ncp), 0)
    p_cmp = _masked_softmax_rows(s, per_head(cpos <= qpos_c))
    o_cmp = jnp.dot(p_cmp.astype(BF16), vc_ref[0, 0, 0].astype(BF16), preferred_element_type=F32)

    psum = p_cmp[0:bq]
    for g in range(1, Q_PER_KV):
        psum = psum + p_cmp[g * bq:(g + 1) * bq]
    p_hi = psum.astype(BF16)
    p_lo = (psum - p_hi.astype(F32)).astype(BF16)
    imp = (jnp.dot(p_hi, ovl_ref[...], preferred_element_type=F32)
           + jnp.dot(p_lo, ovl_ref[...], preferred_element_type=F32))
    cur = qpos // SLC_BLOCK
    forced = (lane == 0) | (lane == cur) | (lane == cur - 1)
    imp = jnp.where(forced, FORCE_SCORE, jnp.where(lane > cur, -1.0, imp))
    imp = jnp.where(lane < n_sel, imp, -2.0)
    rank = jnp.zeros((bq, LANES), F32)
    for b in range(n_sel):
        col = jnp.broadcast_to(imp[:, b:b + 1], (bq, LANES))
        tie = jnp.where(lane > b, 1.0, 0.0)
        rank = rank + jnp.where(col > imp, 1.0, jnp.where(col == imp, tie, 0.0))
    sel = jnp.where(rank < n_top, 1.0, 0.0).astype(BF16)

    m_scr[...] = jnp.full(m_scr.shape, NEG, F32)
    l_scr[...] = jnp.zeros(l_scr.shape, F32)
    acc_scr[...] = jnp.zeros(acc_scr.shape, F32)
    reps = tk // LANES
    qpos_k = qb * bq + lax.broadcasted_iota(jnp.int32, (bq, tk), 0)
    koff = lax.broadcasted_iota(jnp.int32, (bq, tk), 1)

    def body(kt, carry):
        start = pl.multiple_of(kt * tk, tk)
        k = ks_ref[0, pl.ds(start, tk), :].astype(BF16)
        v = vs_ref[0, pl.ds(start, tk), :].astype(BF16)
        s = lax.dot_general(qs, k, nt_dims, preferred_element_type=F32)
        selk = jnp.dot(sel, exp_ref[kt], preferred_element_type=F32)
        ok = per_head((selk > 0.5) & (kt * tk + koff <= qpos_k))
        s = jnp.where(ok, s, NEG)
        m_old = m_scr[...]
        m_new = jnp.maximum(m_old, jnp.max(s, axis=-1, keepdims=True))
        alpha = jnp.exp(m_old - m_new)
        p = jnp.exp(s - jnp.concatenate([m_new] * reps, axis=1))
        l_scr[...] = alpha * l_scr[...] + jnp.sum(p, axis=-1, keepdims=True)
        acc_scr[...] = alpha * acc_scr[...] + jnp.dot(p.astype(BF16), v, preferred_element_type=F32)
        m_scr[...] = m_new
        return carry

    n_tiles = ((qb + 1) * bq + tk - 1) // tk
    lax.fori_loop(0, n_tiles, body, 0)
    o_slc = acc_scr[...] / jnp.maximum(l_scr[...], TINY)

    wstart = pl.multiple_of(jnp.clip(qb * bq + bq - wlen, 0, t - wlen), bq)
    kw = kw_ref[0, pl.ds(wstart, wlen), :].astype(BF16)
    vw = vw_ref[0, pl.ds(wstart, wlen), :].astype(BF16)
    s = lax.dot_general(qs, kw, nt_dims, preferred_element_type=F32)
    diff = (qb * bq + lax.broadcasted_iota(jnp.int32, (bq, wlen), 0)
            - wstart - lax.broadcasted_iota(jnp.int32, (bq, wlen), 1))
    p_win = _masked_softmax_rows(s, per_head((diff >= 0) & (diff <= WINDOW)))
    o_win = jnp.dot(p_win.astype(BF16), vw, preferred_element_type=F32)

    gs = jax.nn.sigmoid(gl_ref[0] + bg_ref[0])
    outs = []
    for g in range(Q_PER_KV):
        r0, r1 = g * bq, (g + 1) * bq
        g_cmp = jnp.broadcast_to(gs[:, 3 * g:3 * g + 1], (bq, HEAD_DIM))
        g_slc = jnp.broadcast_to(gs[:, 3 * g + 1:3 * g + 2], (bq, HEAD_DIM))
        g_win = jnp.broadcast_to(gs[:, 3 * g + 2:3 * g + 3], (bq, HEAD_DIM))
        outs.append(g_cmp * o_cmp[r0:r1] + g_slc * o_slc[r0:r1] + g_win * o_win[r0:r1])
    o = jnp.concatenate(outs, axis=1)
    o_ref[0] = (o * jax.nn.silu(z_ref[0])).astype(o_ref.dtype)


def nsa_prompt_attention(pj, kvc, b_gate, *, bq=128, tk=512):
    n, t, _ = pj.shape
    tk = min(tk, t)
    ncp = kvc.shape[3]
    n_cmp = ncp - 1
    n_sel = -(-t // SLC_BLOCK)
    n_top = min(SLC_TOP, n_sel)
    wlen = min(t, WINDOW + bq)
    c = np.arange(ncp)[:, None]
    sblk = np.arange(LANES)[None, :]
    ovl = ((c * CMP_STRIDE < sblk * SLC_BLOCK + SLC_BLOCK) & (c * CMP_STRIDE + CMP_LEN > sblk * SLC_BLOCK)
           & (sblk < n_sel) & (c < n_cmp))
    ovl = jnp.asarray(ovl, BF16)
    kk = np.arange(t).reshape(t // tk, 1, tk)
    expand = jnp.asarray(kk // SLC_BLOCK == np.arange(LANES)[None, :, None], BF16)
    bg = jnp.zeros((N_KV, 1, LANES), F32).at[:, 0, :3 * Q_PER_KV].set(
        b_gate.astype(F32).reshape(N_KV, 3 * Q_PER_KV))
    kern = functools.partial(_nsa_prompt_kernel, bq=bq, tk=tk, t=t, n_sel=n_sel, n_top=n_top, wlen=wlen)
    rows = Q_PER_KV * bq
    gcb = GROUP_COLS // LANES
    return pl.pallas_call(
        kern,
        grid=(n, N_KV, t // bq),
        in_specs=[
            pl.BlockSpec((1, bq, GROUP_COLS), lambda i, h, qb: (i, qb, h)),
            pl.BlockSpec((1, 1, 1, ncp, HEAD_DIM), lambda i, h, qb: (i, 0, h, 0, 0)),
            pl.BlockSpec((1, 1, 1, ncp, HEAD_DIM), lambda i, h, qb: (i, 1, h, 0, 0)),
            pl.BlockSpec((1, t, LANES), lambda i, h, qb: (i, 0, COL_KV + 2 * N_KV + h)),
            pl.BlockSpec((1, t, LANES), lambda i, h, qb: (i, 0, COL_KV + 3 * N_KV + h)),
            pl.BlockSpec((1, t, LANES), lambda i, h, qb: (i, 0, COL_WIN + h)),
            pl.BlockSpec((1, t, LANES), lambda i, h, qb: (i, 0, COL_WIN + N_KV + h)),
            pl.BlockSpec((1, bq, GROUP_COLS), lambda i, h, qb: (i, qb, COL_Z // gcb + h)),
            pl.BlockSpec((1, bq, LANES), lambda i, h, qb: (i, qb, COL_GATE + h)),
            pl.BlockSpec((1, 1, LANES), lambda i, h, qb: (h, 0, 0)),
            pl.BlockSpec((ncp, LANES), lambda i, h, qb: (0, 0)),
            pl.BlockSpec((t // tk, LANES, tk), lambda i, h, qb: (0, 0, 0)),
        ],
        out_specs=pl.BlockSpec((1, bq, GROUP_COLS), lambda i, h, qb: (i, qb, h)),
        out_shape=jax.ShapeDtypeStruct((n, t, NSA_Q_COLS), BF16),
        scratch_shapes=[pltpu.VMEM((rows, LANES), F32), pltpu.VMEM((rows, LANES), F32),
                        pltpu.VMEM((rows, HEAD_DIM), F32)],
        compiler_params=pltpu.CompilerParams(
            dimension_semantics=("parallel", "parallel", "arbitrary"),
            vmem_limit_bytes=V7X_VMEM_LIMIT_BYTES),
        name="nsa_prompt_attention",
    )(pj, kvc, kvc, pj, pj, pj, pj, pj, pj, bg, ovl, expand)


def _compress_paged_kernel(pt_ref, *refs, nh, page):
    g_pages = PAGES_PER_STEP
    page_refs = refs[:g_pages]
    pe_ref, w1_ref, b1_ref, w2_ref, b2_ref, o_ref, h_scr = refs[g_pages:]
    g = pl.program_id(1)
    half = CMP_LEN // 2
    hpp = page // CMP_STRIDE
    hidden = w1_ref.shape[-1]
    r = lax.broadcasted_iota(jnp.int32, (page, page), 0)
    c = lax.broadcasted_iota(jnp.int32, (page, page), 1)
    perm = jnp.where(c == (r % hpp) * CMP_STRIDE + r // hpp, 1.0, 0.0).astype(BF16)
    xs = []
    for i in range(g_pages):
        x = jnp.concatenate([page_refs[i][:, ty, h, :] for ty in range(2) for h in range(N_KV)], axis=1)
        x_hi = x.astype(BF16)
        x_lo = (x - x_hi.astype(F32)).astype(BF16)
        xs.append(jnp.dot(perm, x_hi, preferred_element_type=F32) + jnp.dot(perm, x_lo, preferred_element_type=F32))
    rows = hpp * g_pages
    for ty in range(2):
        acc_lo = jnp.zeros((N_KV * rows, hidden), F32)
        acc_hi = jnp.zeros((N_KV * rows, hidden), F32)
        for l in range(half):
            pieces = [xs[i][l * hpp:(l + 1) * hpp, (ty * N_KV + h) * HEAD_DIM:(ty * N_KV + h + 1) * HEAD_DIM]
                      for h in range(N_KV) for i in range(g_pages)]
            a = jnp.concatenate(pieces, axis=0)
            acc_lo += jnp.dot((a + pe_ref[ty, l:l + 1, :]).astype(BF16), w1_ref[ty, l],
                              preferred_element_type=F32)
            acc_hi += jnp.dot((a + pe_ref[ty, half + l:half + l + 1, :]).astype(BF16), w1_ref[ty, half + l],
                              preferred_element_type=F32)
        for h in range(N_KV):
            dst = pl.ds(pl.multiple_of(g * rows, rows), rows)
            h_scr[ty * N_KV + h, 0, dst, :] = acc_lo[h * rows:(h + 1) * rows]
            h_scr[ty * N_KV + h, 1, dst, :] = acc_hi[h * rows:(h + 1) * rows]

    @pl.when(g == pl.num_programs(1) - 1)
    def _():
        for ty in range(2):
            for h in range(N_KV):
                lo = h_scr[ty * N_KV + h, 0]
                hi = h_scr[ty * N_KV + h, 1]
                hid = jax.nn.gelu(lo + pltpu.roll(hi, nh - 1, axis=0) + b1_ref[ty])
                out = jnp.dot(hid.astype(BF16), w2_ref[ty], preferred_element_type=F32) + b2_ref[ty]
                row = lax.broadcasted_iota(jnp.int32, out.shape, 0)
                o_ref[0, ty, h] = jnp.where(row < nh - 1, out, 0.0)


def nsa_compress_paged(cache, page_table, cmp_pe, cmp_w1, cmp_b1, cmp_w2, cmp_b2):
    n, npages = page_table.shape
    page = cache.shape[1]
    nh = npages * page // CMP_STRIDE
    hidden = cmp_w1.shape[-1]
    gp = PAGES_PER_STEP
    kern = functools.partial(_compress_paged_kernel, nh=nh, page=page)
    page_specs = [pl.BlockSpec((None, page, 2, N_KV, HEAD_DIM),
                               lambda i, g, pt, j=j: (pt[i, g * gp + j], 0, 0, 0, 0)) for j in range(gp)]
    grid_spec = pltpu.PrefetchScalarGridSpec(
        num_scalar_prefetch=1,
        grid=(n, npages // gp),
        in_specs=page_specs + [
            pl.BlockSpec((2, CMP_LEN, HEAD_DIM), lambda i, g, pt: (0, 0, 0)),
            pl.BlockSpec((2, CMP_LEN, HEAD_DIM, hidden), lambda i, g, pt: (0, 0, 0, 0)),
            pl.BlockSpec((2, 1, hidden), lambda i, g, pt: (0, 0, 0)),
            pl.BlockSpec((2, hidden, HEAD_DIM), lambda i, g, pt: (0, 0, 0)),
            pl.BlockSpec((2, 1, HEAD_DIM), lambda i, g, pt: (0, 0, 0)),
        ],
        out_specs=pl.BlockSpec((1, 2, N_KV, nh, HEAD_DIM), lambda i, g, pt: (i, 0, 0, 0, 0)),
        scratch_shapes=[pltpu.VMEM((2 * N_KV, 2, nh, hidden), F32)],
    )
    return pl.pallas_call(
        kern,
        grid_spec=grid_spec,
        out_shape=jax.ShapeDtypeStruct((n, 2, N_KV, nh, HEAD_DIM), F32),
        compiler_params=pltpu.CompilerParams(dimension_semantics=("parallel", "arbitrary"),
                                             vmem_limit_bytes=V7X_VMEM_LIMIT_BYTES),
        name="nsa_compress_paged",
    )(page_table, *([cache] * gp), cmp_pe, cmp_w1.astype(BF16), cmp_b1.reshape(2, 1, hidden),
      cmp_w2.astype(BF16), cmp_b2.reshape(2, 1, HEAD_DIM))


def _decode_select_kernel(q_ref, kc_ref, vc_ref, ovl_ref, ocmp_ref, sel_ref, *, past, n_sel, n_top):
    nt_dims = (((1,), (1,)), ((), ()))
    qs = (q_ref[0] * (HEAD_DIM ** -0.5)).astype(BF16)
    nh = kc_ref.shape[3]
    s = lax.dot_general(qs, kc_ref[0, 0, 0].astype(BF16), nt_dims, preferred_element_type=F32)
    cpos = lax.broadcasted_iota(jnp.int32, (Q_PER_KV, nh), 1) * CMP_STRIDE + (CMP_LEN - 1)
    p_cmp = _masked_softmax_rows(s, cpos <= past)
    ocmp_ref[0, 0] = jnp.dot(p_cmp.astype(BF16), vc_ref[0, 0, 0].astype(BF16), preferred_element_type=F32)
    psum = jnp.sum(p_cmp, axis=0, keepdims=True)
    p_hi = psum.astype(BF16)
    p_lo = (psum - p_hi.astype(F32)).astype(BF16)
    imp = (jnp.dot(p_hi, ovl_ref[...], preferred_element_type=F32)
           + jnp.dot(p_lo, ovl_ref[...], preferred_element_type=F32))
    nsp = imp.shape[1]
    lane = lax.broadcasted_iota(jnp.int32, (1, nsp), 1)
    cur = past // SLC_BLOCK
    forced = (lane == 0) | (lane == cur) | (lane == cur - 1)
    imp = jnp.where(forced, FORCE_SCORE, jnp.where(lane > cur, -1.0, imp))
    imp = jnp.where(lane < n_sel, imp, -2.0)
    rank = jnp.zeros((1, nsp), F32)
    for b in range(n_sel):
        col = jnp.broadcast_to(imp[:, b:b + 1], (1, nsp))
        tie = jnp.where(lane > b, 1.0, 0.0)
        rank = rank + jnp.where(col > imp, 1.0, jnp.where(col == imp, tie, 0.0))
    sel_ref[0, 0] = jnp.where(rank < n_top, 1.0, 0.0)


def nsa_decode_select(q, kvc, past):
    n = q.shape[0]
    nh = kvc.shape[3]
    n_cmp = nh - 1
    n_sel = past // SLC_BLOCK + 1
    n_top = min(SLC_TOP, n_sel)
    nsp = _round_up(n_sel, LANES)
    c = np.arange(nh)[:, None]
    sblk = np.arange(nsp)[None, :]
    ovl = ((c * CMP_STRIDE < sblk * SLC_BLOCK + SLC_BLOCK) & (c * CMP_STRIDE + CMP_LEN > sblk * SLC_BLOCK)
           & (sblk < n_sel) & (c < n_cmp))
    ovl = jnp.asarray(ovl, BF16)
    kern = functools.partial(_decode_select_kernel, past=past, n_sel=n_sel, n_top=n_top)
    return pl.pallas_call(
        kern,
        grid=(n, N_KV),
        in_specs=[
            pl.BlockSpec((1, Q_PER_KV, HEAD_DIM), lambda i, h: (i, h, 0)),
            pl.BlockSpec((1, 1, 1, nh, HEAD_DIM), lambda i, h: (i, 0, h, 0, 0)),
            pl.BlockSpec((1, 1, 1, nh, HEAD_DIM), lambda i, h: (i, 1, h, 0, 0)),
            pl.BlockSpec((nh, nsp), lambda i, h: (0, 0)),
        ],
        out_specs=[pl.BlockSpec((1, 1, Q_PER_KV, HEAD_DIM), lambda i, h: (i, h, 0, 0)),
                   pl.BlockSpec((1, 1, 1, nsp), lambda i, h: (i, h, 0, 0))],
        out_shape=[jax.ShapeDtypeStruct((n, N_KV, Q_PER_KV, HEAD_DIM), F32),
                   jax.ShapeDtypeStruct((n, N_KV, 1, nsp), F32)],
        compiler_params=pltpu.CompilerParams(dimension_semantics=("parallel", "parallel"),
                                             vmem_limit_bytes=V7X_VMEM_LIMIT_BYTES),
        name="nsa_decode_select",
    )(q, kvc, kvc, ovl)


def _decode_slc_kernel(pt_ref, q_ref, sel_ref, *refs, page):
    page_refs = refs[:PAGES_PER_STEP]
    knew_ref, vnew_ref, o_ref, m_scr, l_scr, acc_scr = refs[PAGES_PER_STEP:]
    p = pl.program_id(1)
    nt_dims = (((1,), (1,)), ((), ()))
    nsp = sel_ref.shape[3]
    keys = PAGES_PER_STEP * page
    reps = keys // LANES

    @pl.when(p == 0)
    def _():
        m_scr[...] = jnp.full(m_scr.shape, NEG, F32)
        l_scr[...] = jnp.zeros(l_scr.shape, F32)
        acc_scr[...] = jnp.zeros(acc_scr.shape, F32)

    blk = lax.broadcasted_iota(jnp.int32, (nsp, keys), 0)
    key = lax.broadcasted_iota(jnp.int32, (nsp, keys), 1)
    expand = jnp.where(blk == (p * keys + key) // SLC_BLOCK, 1.0, 0.0).astype(BF16)
    for h in range(N_KV):
        qs = (q_ref[0, h * Q_PER_KV:(h + 1) * Q_PER_KV, :] * (HEAD_DIM ** -0.5)).astype(BF16)
        k = jnp.concatenate([r[:, 0, h, :] for r in page_refs], axis=0).astype(BF16)
        v = jnp.concatenate([r[:, 1, h, :] for r in page_refs], axis=0).astype(BF16)
        s = lax.dot_general(qs, k, nt_dims, preferred_element_type=F32)
        sel = jnp.broadcast_to(sel_ref[0, h], (Q_PER_KV, nsp)).astype(BF16)
        ok = jnp.dot(sel, expand, preferred_element_type=F32) > 0.5
        s = jnp.where(ok, s, NEG)
        m_old = m_scr[h]
        m_new = jnp.maximum(m_old, jnp.max(s, axis=-1, keepdims=True))
        alpha = jnp.exp(m_old - m_new)
        pr = jnp.where(ok, jnp.exp(s - jnp.concatenate([m_new] * reps, axis=1)), 0.0)
        l_scr[h] = alpha * l_scr[h] + jnp.sum(pr, axis=-1, keepdims=True)
        acc_scr[h] = alpha * acc_scr[h] + jnp.dot(pr.astype(BF16), v, preferred_element_type=F32)
        m_scr[h] = m_new

    @pl.when(p == pl.num_programs(1) - 1)
    def _():
        for h in range(N_KV):
            qf = q_ref[0, h * Q_PER_KV:(h + 1) * Q_PER_KV, :] * (HEAD_DIM ** -0.5)
            s_new = jnp.sum(qf * knew_ref[0, h:h + 1, :], axis=-1, keepdims=True)
            m_old = m_scr[h]
            m_new = jnp.maximum(m_old, s_new)
            alpha = jnp.exp(m_old - m_new)
            p_new = jnp.exp(s_new - m_new)
            den = alpha * l_scr[h] + p_new
            acc = alpha * acc_scr[h] + p_new * vnew_ref[0, h:h + 1, :]
            o_ref[0, h] = acc / jnp.maximum(den, TINY)


def nsa_decode_selected(q, sel, cache, page_table, k_new, v_new):
    n, npages = page_table.shape
    page = cache.shape[1]
    nsp = sel.shape[3]
    gp = PAGES_PER_STEP
    kern = functools.partial(_decode_slc_kernel, page=page)
    page_specs = [pl.BlockSpec((None, page, 2, N_KV, HEAD_DIM),
                               lambda i, p, pt, j=j: (pt[i, p * gp + j], 0, 1, 0, 0)) for j in range(gp)]
    grid_spec = pltpu.PrefetchScalarGridSpec(
        num_scalar_prefetch=1,
        grid=(n, npages // gp),
        in_specs=[
            pl.BlockSpec((1, N_HEADS, HEAD_DIM), lambda i, p, pt: (i, 0, 0)),
            pl.BlockSpec((1, N_KV, 1, nsp), lambda i, p, pt: (i, 0, 0, 0)),
        ] + page_specs + [
            pl.BlockSpec((1, N_KV, HEAD_DIM), lambda i, p, pt: (i, 0, 0)),
            pl.BlockSpec((1, N_KV, HEAD_DIM), lambda i, p, pt: (i, 0, 0)),
        ],
        out_specs=pl.BlockSpec((1, N_KV, Q_PER_KV, HEAD_DIM), lambda i, p, pt: (i, 0, 0, 0)),
        scratch_shapes=[pltpu.VMEM((N_KV, Q_PER_KV, LANES), F32), pltpu.VMEM((N_KV, Q_PER_KV, LANES), F32),
                        pltpu.VMEM((N_KV, Q_PER_KV, HEAD_DIM), F32)],
    )
    return pl.pallas_call(
        kern,
        grid_spec=grid_spec,
        out_shape=jax.ShapeDtypeStruct((n, N_KV, Q_PER_KV, HEAD_DIM), F32),
        compiler_params=pltpu.CompilerParams(dimension_semantics=("parallel", "arbitrary"),
                                             vmem_limit_bytes=V7X_VMEM_LIMIT_BYTES),
        name="nsa_decode_selected",
    )(page_table, q, sel, *([cache] * gp), k_new, v_new)


def _decode_combine_kernel(q_ref, win_ref, kwn_ref, vwn_ref, ocmp_ref, oslc_ref, gl_ref, bg_ref, z_ref, o_ref):
    nt_dims = (((1,), (1,)), ((), ()))
    for h in range(N_KV):
        qf = q_ref[0, h * Q_PER_KV:(h + 1) * Q_PER_KV, :] * (HEAD_DIM ** -0.5)
        k = win_ref[0, :, 0, h, :].astype(BF16)
        v = win_ref[0, :, 1, h, :].astype(BF16)
        s = lax.dot_general(qf.astype(BF16), k, nt_dims, preferred_element_type=F32)
        s_new = jnp.sum(qf * kwn_ref[0, h:h + 1, :], axis=-1, keepdims=True)
        m = jnp.maximum(jnp.max(s, axis=-1, keepdims=True), s_new)
        pr = jnp.exp(s - m)
        p_new = jnp.exp(s_new - m)
        den = jnp.maximum(jnp.sum(pr, axis=-1, keepdims=True) + p_new, TINY)
        o_win = (jnp.dot(pr.astype(BF16), v, preferred_element_type=F32) + p_new * vwn_ref[0, h:h + 1, :]) / den
        g_cmp = jax.nn.sigmoid(gl_ref[0, h, 0] + bg_ref[h, 0])
        g_slc = jax.nn.sigmoid(gl_ref[0, h, 1] + bg_ref[h, 1])
        g_win = jax.nn.sigmoid(gl_ref[0, h, 2] + bg_ref[h, 2])
        o = g_cmp * ocmp_ref[0, h] + g_slc * oslc_ref[0, h] + g_win * o_win
        rows = slice(h * Q_PER_KV, (h + 1) * Q_PER_KV)
        o_ref[0, rows, :] = o * jax.nn.silu(z_ref[0, rows, :])


def nsa_decode_combine(q, win_cache, kw_new, vw_new, o_cmp, o_slc, gl, b_gate, z):
    n, lb = win_cache.shape[:2]
    glt = gl.reshape(n, N_KV, Q_PER_KV, 3).transpose(0, 1, 3, 2)[..., None]
    bgt = b_gate.astype(F32).reshape(N_KV, Q_PER_KV, 3).transpose(0, 2, 1)[..., None]
    return pl.pallas_call(
        _decode_combine_kernel,
        grid=(n,),
        in_specs=[
            pl.BlockSpec((1, N_HEADS, HEAD_DIM), lambda i: (i, 0, 0)),
            pl.BlockSpec((1, lb, 2, N_KV, HEAD_DIM), lambda i: (i, 0, 0, 0, 0)),
            pl.BlockSpec((1, N_KV, HEAD_DIM), lambda i: (i, 0, 0)),
            pl.BlockSpec((1, N_KV, HEAD_DIM), lambda i: (i, 0, 0)),
            pl.BlockSpec((1, N_KV, Q_PER_KV, HEAD_DIM), lambda i: (i, 0, 0, 0)),
            pl.BlockSpec((1, N_KV, Q_PER_KV, HEAD_DIM), lambda i: (i, 0, 0, 0)),
            pl.BlockSpec((1, N_KV, 3, Q_PER_KV, 1), lambda i: (i, 0, 0, 0, 0)),
            pl.BlockSpec((N_KV, 3, Q_PER_KV, 1), lambda i: (0, 0, 0, 0)),
            pl.BlockSpec((1, N_HEADS, HEAD_DIM), lambda i: (i, 0, 0)),
        ],
        out_specs=pl.BlockSpec((1, N_HEADS, HEAD_DIM), lambda i: (i, 0, 0)),
        out_shape=jax.ShapeDtypeStruct((n, N_HEADS, HEAD_DIM), F32),
        compiler_params=pltpu.CompilerParams(dimension_semantics=("parallel",),
                                             vmem_limit_bytes=V7X_VMEM_LIMIT_BYTES),
        name="nsa_decode_combine",
    )(q, win_cache, kw_new, vw_new, o_cmp, o_slc, glt, bgt, z)


def nsa_decode(pj, cache_kv, page_table, cache_win, b_gate, cmp_pe, cmp_w1, cmp_b1, cmp_w2, cmp_b2):
    n = pj.shape[0]
    pool, page = cache_kv.shape[:2]
    past = page_table.shape[1] * page
    lb = cache_win.shape[1]
    assert lb <= WINDOW and lb <= past and page % SLC_BLOCK == 0
    cache = cache_kv
    win = cache_win
    q = pj[:, :NSA_Q_COLS].reshape(n, N_HEADS, HEAD_DIM)
    kv_new = pj[:, COL_KV * LANES:COL_WIN * LANES].reshape(n, 4, N_KV, HEAD_DIM)
    win_new = pj[:, COL_WIN * LANES:COL_Z * LANES].reshape(n, 2, N_KV, HEAD_DIM)
    z = pj[:, COL_Z * LANES:COL_GATE * LANES].reshape(n, N_HEADS, HEAD_DIM)
    gl = pj[:, COL_GATE * LANES:].reshape(n, N_KV, LANES)[..., :3 * Q_PER_KV]
    kvc = nsa_compress_paged(cache, page_table, cmp_pe, cmp_w1, cmp_b1, cmp_w2, cmp_b2)
    o_cmp, sel = nsa_decode_select(q, kvc, past)
    o_slc = nsa_decode_selected(q, sel, cache, page_table, kv_new[:, 2], kv_new[:, 3])
    o = nsa_decode_combine(q, win, win_new[:, 0], win_new[:, 1], o_cmp, o_slc, gl, b_gate, z)
    return o.reshape(n, NSA_Q_COLS)


def nsa_layer(x, kv_cache, win_past, page_table, w_in, b_gate, cmp_pe, cmp_w1, cmp_b1, cmp_w2, cmp_b2, w_out):
    n, t, _ = x.shape
    pj = proj(x, w_in)
    kv_new = pj[..., COL_KV * LANES:COL_WIN * LANES].reshape(n, t, 4, N_KV, HEAD_DIM)
    win_new = pj[..., COL_WIN * LANES:COL_Z * LANES].reshape(n, t, 2, N_KV, HEAD_DIM)
    if kv_cache is None:
        kvc = nsa_compress_prompt(pj, cmp_pe, cmp_w1, cmp_b1, cmp_w2, cmp_b2)
        o = nsa_prompt_attention(pj, kvc, b_gate)
        win_state = win_new[:, -min(WINDOW, t):]
    else:
        assert t == 1
        o = nsa_decode(pj[:, 0], kv_cache, page_table, win_past, b_gate, cmp_pe, cmp_w1, cmp_b1, cmp_w2, cmp_b2)
        o = o.astype(BF16)
        keys = jnp.concatenate([win_past.astype(win_new.dtype), win_new], axis=1)
        win_state = keys[:, -min(WINDOW, win_past.shape[1] + t):]
    y = matmul(o.reshape(n * t, NSA_Q_COLS), w_out).reshape(n, t, -1)
    return y, (kv_new, win_state)


def _nsa_w_in_layout(w_in):
    nsa_gate = 3 * N_HEADS
    a = COL_Z * LANES
    d = w_in.shape[0]
    w_gate = w_in[:, a:a + nsa_gate].reshape(d, N_KV, 3 * Q_PER_KV)
    w_gate = jnp.pad(w_gate, ((0, 0), (0, 0), (0, LANES - 3 * Q_PER_KV))).reshape(d, N_KV * LANES)
    return jnp.concatenate([w_in[:, :a], w_in[:, a + nsa_gate:], w_gate], axis=1)


def kernel(x_prompt, x_sample, state_l0_ssm, state_l1_conv, state_l1_rnn, cache_l2_kv, cache_l2_win, state_l3_ssm, page_table, l0_norm_pre, l0_norm_post, l0_w_in, l0_a_re, l0_a_im, l0_log_dt, l0_b, l0_c, l0_d, l0_w_glu, l0_b_glu, l0_w_out, l1_norm_pre, l1_norm_post, l1_w_in, l1_conv_w, l1_conv_b, l1_w_rg, l1_b_rg, l1_w_ig, l1_b_ig, l1_lam, l1_w_out, l2_norm_pre, l2_norm_post, l2_w_in, l2_b_gate, l2_cmp_pe, l2_cmp_w1, l2_cmp_b1, l2_cmp_w2, l2_cmp_b2, l2_w_out, l3_norm_pre, l3_norm_post, l3_w_in, l3_a_re, l3_a_im, l3_log_dt, l3_b, l3_c, l3_d, l3_w_glu, l3_b_glu, l3_w_out):
    bp = x_prompt.shape[0]
    dt = x_prompt.dtype
    d_model = x_prompt.shape[-1]
    s5_groups = l0_a_re.shape[0]
    rg_width = l1_lam.shape[0]
    bf = lambda w: w.astype(BF16)
    l0_p = s5_params(l0_w_in, l0_a_re, l0_a_im, l0_log_dt, l0_b, l0_c, l0_d, l0_w_glu, l0_b_glu, l0_w_out)
    l3_p = s5_params(l3_w_in, l3_a_re, l3_a_im, l3_log_dt, l3_b, l3_c, l3_d, l3_w_glu, l3_b_glu, l3_w_out)
    l1_p = rg_params(l1_w_in, l1_conv_w, l1_conv_b, l1_w_rg, l1_b_rg, l1_w_ig, l1_b_ig, l1_lam, l1_w_out)
    l2_wi, l2_wo = bf(_nsa_w_in_layout(l2_w_in)), bf(l2_w_out)
    layers = (
        (l0_norm_pre, l0_norm_post, lambda h, st: s5_layer(h, st[0], l0_p)),
        (l1_norm_pre, l1_norm_post, lambda h, st: rglru_layer(h, st[0], st[1], l1_p)),
        (l2_norm_pre, l2_norm_post,
         lambda h, st: nsa_layer(h, st[0], st[1], page_table, l2_wi, l2_b_gate, l2_cmp_pe, l2_cmp_w1, l2_cmp_b1,
                                 l2_cmp_w2, l2_cmp_b2, l2_wo)),
        (l3_norm_pre, l3_norm_post, lambda h, st: s5_layer(h, st[0], l3_p)),
    )
    prompt_state = (
        (jnp.zeros((bp, 2, s5_groups, S5_STATE), dt),),
        (jnp.zeros((bp, RG_CONV - 1, rg_width), dt), jnp.zeros((bp, rg_width), dt)),
        (None, None),
        (jnp.zeros((bp, 2, s5_groups, S5_STATE), dt),),
    )
    sample_state = (
        (state_l0_ssm,),
        (state_l1_conv, state_l1_rnn),
        (cache_l2_kv, cache_l2_win),
        (state_l3_ssm,),
    )
    n_layers = len(layers)
    streams = []
    for h, states in ((x_prompt, prompt_state), (x_sample, sample_state)):
        n, t, _ = h.shape
        h2 = h.reshape(n * t, d_model)
        xn = rmsnorm(h2, layers[0][0], BF16)
        new_states = []
        for i in range(n_layers):
            _, g_post, mix = layers[i]
            y, s_new = mix(xn.reshape(n, t, d_model), states[i])
            g_next = layers[i + 1][0] if i + 1 < n_layers else None
            h2, xn = post_norm_residual(h2, y.reshape(n * t, d_model), g_post, g_next)
            new_states.append(s_new)
        streams.append((h2.reshape(n, t, d_model), new_states))
    (hp, new_p), (hs, new_s) = streams
    (l0_ssm_p,), (l1_conv_p, l1_rnn_p), (l2_kv_p, l2_win_p), (l3_ssm_p,) = new_p
    (l0_ssm_s,), (l1_conv_s, l1_rnn_s), (l2_kv_s, l2_win_s), (l3_ssm_s,) = new_s
    return (hp, hs, l0_ssm_p, l0_ssm_s, l1_conv_p, l1_rnn_p, l1_conv_s, l1_rnn_s,
            l2_kv_p, l2_win_p, l2_kv_s, l2_win_s, l3_ssm_p, l3_ssm_s)
```
